```python
import math, functools
import jax, jax.numpy as jnp
from jax import lax
import numpy as np

D_MODEL = 2048
BATCH = 16
SEQ = 2048
DEPTH = 2

GRID_W = 64
CTX_LEN = 256
Q_BLOCK = 128
ROPE_THETA = 10000.0
NORM_EPS = 1e-6
D_FF = 5632
N_SUB = 3
HALF_STEP = 0.5
N_EVEN = (DEPTH + 1) // 2
N_ODD = DEPTH // 2
DEEPNORM_ALPHA = (2 * DEPTH) ** 0.25
DEEPNORM_BETA = (8 * DEPTH) ** -0.25

MLA_HEADS = 8
MLA_Q_RANK = 512
MLA_KV_RANK = 256
MLA_NOPE = 128
MLA_ROPE = 64
MLA_V = 128
MLA_SCALE = (MLA_NOPE + MLA_ROPE) ** -0.5
GQA_HEADS = 8
GQA_KV_HEADS = 2
GQA_HEAD_DIM = 128
GQA_SCALE = GQA_HEAD_DIM ** -0.5
DIFF_HEADS = 8
DIFF_HEAD_DIM = 128
DIFF_SCALE = DIFF_HEAD_DIM ** -0.5

EVEN_SPLITS = (MLA_Q_RANK, MLA_KV_RANK, MLA_ROPE, GQA_HEADS * GQA_HEAD_DIM,
               GQA_KV_HEADS * GQA_HEAD_DIM, GQA_KV_HEADS * GQA_HEAD_DIM)
EVEN_IN = MLA_Q_RANK + MLA_KV_RANK + MLA_ROPE + (GQA_HEADS + 2 * GQA_KV_HEADS) * GQA_HEAD_DIM
EVEN_OUT = MLA_HEADS * MLA_V + GQA_HEADS * GQA_HEAD_DIM
DIFF_IN = 3 * DIFF_HEADS * 2 * DIFF_HEAD_DIM
DIFF_OUT = DIFF_HEADS * 2 * DIFF_HEAD_DIM

kernel_name = "hybrid_mla_gqa_diffattn_macaron_dit"


def layer_norm(x, g, b):
    xf = x.astype(jnp.float32)
    mu = jnp.mean(xf, -1, keepdims=True)
    var = jnp.mean(jnp.square(xf - mu), -1, keepdims=True)
    return ((xf - mu) * lax.rsqrt(var + NORM_EPS) * g + b).astype(x.dtype)


def rms_norm(x, g):
    xf = x.astype(jnp.float32)
    return (xf * lax.rsqrt(jnp.mean(xf * xf, -1, keepdims=True) + NORM_EPS) * g).astype(x.dtype)


def swiglu(h, w1, w3, w2):
    return (jax.nn.silu(h @ w1) * (h @ w3)) @ w2


def modulate(h, mod, j):
    return h * (1 + mod[:, j, 1]) + mod[:, j, 0]


def post_norm_residual(xs, mod, j, y, g, b):
    return layer_norm(DEEPNORM_ALPHA * xs + mod[:, j, 2] * y, g, b)


def ffn_half_step(xs, mod, j, w1, w3, w2, g, b):
    y = HALF_STEP * swiglu(modulate(xs, mod, j), w1, w3, w2)
    return post_norm_residual(xs, mod, j, y, g, b)


def axial_rope(rows, rot_dim):
    r, col = jnp.meshgrid(jnp.arange(rows, dtype=jnp.float32),
                          jnp.arange(GRID_W, dtype=jnp.float32), indexing="ij")
    n_freq = rot_dim // 4
    inv_freq = ROPE_THETA ** (-jnp.arange(n_freq, dtype=jnp.float32) / n_freq)
    ang = jnp.concatenate([r.reshape(-1, 1) * inv_freq, col.reshape(-1, 1) * inv_freq], -1)
    return jnp.cos(ang), jnp.sin(ang)


def apply_rope(x, cos, sin):
    half = x.shape[-1] // 2
    xf = x.astype(jnp.float32)
    x1, x2 = xf[..., :half], xf[..., half:]
    cs, sn = cos[None, :, None, :], sin[None, :, None, :]
    return jnp.concatenate([x1 * cs - x2 * sn, x2 * cs + x1 * sn], -1).astype(x.dtype)


def flatten_heads(y):
    return y.reshape(y.shape[0], y.shape[1], -1)


def sweep_query_blocks(fn, q):
    b, n = q.shape[:2]
    nb = n // Q_BLOCK
    blocks = jnp.moveaxis(q.reshape(b, nb, Q_BLOCK, *q.shape[2:]), 1, 0)
    out = lax.map(fn, blocks)
    return jnp.moveaxis(out, 0, 1).reshape(b, n, *out.shape[3:])


def grouped_softmax_attention(q, k, v, scale):
    s = jnp.einsum("bqhgd,bkhd->bhgqk", q, k, preferred_element_type=jnp.float32) * scale
    p = jax.nn.softmax(s, axis=-1).astype(v.dtype)
    return jnp.einsum("bhgqk,bkhd->bqhgd", p, v)


def diff_softmax_attention(q, k, v, lam):
    s = jnp.einsum("bqhjd,bkhjd->bhjqk", q, k, preferred_element_type=jnp.float32) * DIFF_SCALE
    p = jax.nn.softmax(s, axis=-1)
    a = (p[:, :, 0] - lam * p[:, :, 1]).astype(v.dtype)
    return jnp.einsum("bhqk,bkhd->bqhd", a, v)


def two_stream_attention(attend, q, k, v, q_c, k_c, v_c, need_ctx):
    k_all = jnp.concatenate([k_c, k], axis=1)
    v_all = jnp.concatenate([v_c, v], axis=1)
    y = sweep_query_blocks(lambda qb: attend(qb, k_all, v_all), q)
    y_c = attend(q_c, k_c, v_c) if need_ctx else None
    return y, y_c


def split_even(z):
    cuts = np.cumsum(EVEN_SPLITS)[:-1].tolist()
    return jnp.split(z, cuts, axis=-1)


def mla_qkv(cq, ckv, kr, g_cq, g_ckv, w_uq, w_ukv, rope):
    b, n = cq.shape[:2]
    q = (rms_norm(cq, g_cq) @ w_uq).reshape(b, n, MLA_HEADS, MLA_NOPE + MLA_ROPE)
    kv = (rms_norm(ckv, g_ckv) @ w_ukv).reshape(b, n, MLA_HEADS, MLA_NOPE + MLA_V)
    q_nope, q_rot = q[..., :MLA_NOPE], q[..., MLA_NOPE:]
    k_nope, v = kv[..., :MLA_NOPE], kv[..., MLA_NOPE:]
    k_rot = kr[:, :, None, :]
    if rope is not None:
        q_rot = apply_rope(q_rot, *rope)
        k_rot = apply_rope(k_rot, *rope)
    k = jnp.concatenate([k_nope, jnp.broadcast_to(k_rot, (b, n, MLA_HEADS, MLA_ROPE))], -1)
    q = jnp.concatenate([q_nope, q_rot], -1)
    return q[:, :, :, None, :], k, v


def gqa_qkv(q, k, v, g_q, g_k, rope):
    b, n = q.shape[:2]
    q = rms_norm(q.reshape(b, n, GQA_HEADS, GQA_HEAD_DIM), g_q)
    k = rms_norm(k.reshape(b, n, GQA_KV_HEADS, GQA_HEAD_DIM), g_k)
    v = v.reshape(b, n, GQA_KV_HEADS, GQA_HEAD_DIM)
    if rope is not None:
        q = apply_rope(q, *rope)
        k = apply_rope(k, *rope)
    q = q.reshape(b, n, GQA_KV_HEADS, GQA_HEADS // GQA_KV_HEADS, GQA_HEAD_DIM)
    return q, k, v


def mla_gqa_mixer(h_lat, h_ctx, rope_mla, rope_gqa, w_in, g_cq, g_ckv, w_uq, w_ukv,
                  g_q, g_k, w_o, need_ctx):
    def project(h, rope_a, rope_b):
        cq, ckv, kr, qb, kb, vb = split_even(h @ w_in)
        return (mla_qkv(cq, ckv, kr, g_cq, g_ckv, w_uq, w_ukv, rope_a),
                gqa_qkv(qb, kb, vb, g_q, g_k, rope_b))

    (qa, ka, va), (qb, kb, vb) = project(h_lat, rope_mla, rope_gqa)
    (qa_c, ka_c, va_c), (qb_c, kb_c, vb_c) = project(h_ctx, None, None)
    att_a = functools.partial(grouped_softmax_attention, scale=MLA_SCALE)
    att_b = functools.partial(grouped_softmax_attention, scale=GQA_SCALE)
    ya, ya_c = two_stream_attention(att_a, qa, ka, va, qa_c, ka_c, va_c, need_ctx)
    yb, yb_c = two_stream_attention(att_b, qb, kb, vb, qb_c, kb_c, vb_c, need_ctx)

    def merge(a, bb):
        return jnp.concatenate([flatten_heads(a), flatten_heads(bb)], -1) @ w_o

    return merge(ya, yb), (merge(ya_c, yb_c) if need_ctx else None)


def diff_mixer(h_lat, h_ctx, rope, w_in, lq1, lk1, lq2, lk2, g_sub, w_o, lambda_init, need_ctx):
    lam = (jnp.exp(jnp.sum(lq1.astype(jnp.float32) * lk1.astype(jnp.float32)))
           - jnp.exp(jnp.sum(lq2.astype(jnp.float32) * lk2.astype(jnp.float32))) + lambda_init)

    def project(h, rp):
        b, n = h.shape[:2]
        q, k, v = jnp.split(h @ w_in, 3, axis=-1)
        q = q.reshape(b, n, 2 * DIFF_HEADS, DIFF_HEAD_DIM)
        k = k.reshape(b, n, 2 * DIFF_HEADS, DIFF_HEAD_DIM)
        if rp is not None:
            q = apply_rope(q, *rp)
            k = apply_rope(k, *rp)
        q = q.reshape(b, n, DIFF_HEADS, 2, DIFF_HEAD_DIM)
        k = k.reshape(b, n, DIFF_HEADS, 2, DIFF_HEAD_DIM)
        v = v.reshape(b, n, DIFF_HEADS, 2 * DIFF_HEAD_DIM)
        return q, k, v

    q, k, v = project(h_lat, rope)
    q_c, k_c, v_c = project(h_ctx, None)
    attend = lambda qq, kk, vv: diff_softmax_attention(qq, kk, vv, lam)
    y, y_c = two_stream_attention(attend, q, k, v, q_c, k_c, v_c, need_ctx)

    def finish(yy):
        return flatten_heads(rms_norm(yy, g_sub) * (1.0 - lambda_init)) @ w_o

    return finish(y), (finish(y_c) if need_ctx else None)


def setup_inputs(seed: int = 0) -> dict:
    key = jax.random.key(seed)
    ks = iter(jax.random.split(key, 32))
    nrm = lambda shape, scale: jax.random.normal(next(ks), shape, jnp.float32) * scale
    gain = lambda shape: 1.0 + nrm(shape, 0.02)
    D, F = D_MODEL, D_FF
    return {
        "x": nrm((BATCH, SEQ, D), 1.0),
        "c": nrm((BATCH, D), 1.0),
        "ctx": nrm((BATCH, CTX_LEN, D), 1.0),
        "c_ctx": nrm((D,), 1.0),
        "w_ada": nrm((DEPTH, D, N_SUB * 3 * D), D ** -0.5),
        "b_ada": nrm((DEPTH, N_SUB * 3 * D), 0.02),
        "ln_g": gain((DEPTH, N_SUB, D)),
        "ln_b": nrm((DEPTH, N_SUB, D), 0.02),
        "ffn_w1": nrm((DEPTH, 2, D, F), D ** -0.5),
        "ffn_w3": nrm((DEPTH, 2, D, F), D ** -0.5),
        "ffn_w2": nrm((DEPTH, 2, F, D), DEEPNORM_BETA * F ** -0.5),
        "mg_w_in": nrm((N_EVEN, D, EVEN_IN), D ** -0.5),
        "mla_g_cq": gain((N_EVEN, MLA_Q_RANK)),
        "mla_g_ckv": gain((N_EVEN, MLA_KV_RANK)),
        "mla_w_uq": nrm((N_EVEN, MLA_Q_RANK, MLA_HEADS * (MLA_NOPE + MLA_ROPE)), MLA_Q_RANK ** -0.5),
        "mla_w_ukv": nrm((N_EVEN, MLA_KV_RANK, MLA_HEADS * (MLA_NOPE + MLA_V)), MLA_KV_RANK ** -0.5),
        "gqa_g_q": gain((N_EVEN, GQA_HEAD_DIM)),
        "gqa_g_k": gain((N_EVEN, GQA_HEAD_DIM)),
        "mg_w_o": nrm((N_EVEN, EVEN_OUT, D), DEEPNORM_BETA * EVEN_OUT ** -0.5),
        "diff_w_in": nrm((N_ODD, D, DIFF_IN), D ** -0.5),
        "diff_lq1": nrm((N_ODD, DIFF_HEAD_DIM), 0.1),
        "diff_lk1": nrm((N_ODD, DIFF_HEAD_DIM), 0.1),
        "diff_lq2": nrm((N_ODD, DIFF_HEAD_DIM), 0.1),
        "diff_lk2": nrm((N_ODD, DIFF_HEAD_DIM), 0.1),
        "diff_g_sub": gain((N_ODD, 2 * DIFF_HEAD_DIM)),
        "diff_w_o": nrm((N_ODD, DIFF_OUT, D), DEEPNORM_BETA * DIFF_OUT ** -0.5),
    }


def reference(x, c, ctx, c_ctx, w_ada, b_ada, ln_g, ln_b, ffn_w1, ffn_w3, ffn_w2,
              mg_w_in, mla_g_cq, mla_g_ckv, mla_w_uq, mla_w_ukv, gqa_g_q, gqa_g_k, mg_w_o,
              diff_w_in, diff_lq1, diff_lk1, diff_lq2, diff_lk2, diff_g_sub, diff_w_o):
    b, n, d = x.shape
    rows = n // GRID_W
    rope_mla = axial_rope(rows, MLA_ROPE)
    rope_gqa = axial_rope(rows, GQA_HEAD_DIM)
    rope_diff = axial_rope(rows, DIFF_HEAD_DIM)
    s_lat = jax.nn.silu(c)
    s_ctx = jax.nn.silu(c_ctx)[None]
    x_lat, x_ctx = x, ctx
    for i in range(DEPTH):
        need_ctx = i < DEPTH - 1
        mod_lat = (s_lat @ w_ada[i] + b_ada[i]).reshape(b, N_SUB, 3, 1, d)
        mod_ctx = (s_ctx @ w_ada[i] + b_ada[i]).reshape(1, N_SUB, 3, 1, d)

        x_lat = ffn_half_step(x_lat, mod_lat, 0, ffn_w1[i, 0], ffn_w3[i, 0], ffn_w2[i, 0], ln_g[i, 0], ln_b[i, 0])
        x_ctx = ffn_half_step(x_ctx, mod_ctx, 0, ffn_w1[i, 0], ffn_w3[i, 0], ffn_w2[i, 0], ln_g[i, 0], ln_b[i, 0])

        h_lat = modulate(x_lat, mod_lat, 1)
        h_ctx = modulate(x_ctx, mod_ctx, 1)
        if i % 2 == 0:
            e = i // 2
            y_lat, y_ctx = mla_gqa_mixer(h_lat, h_ctx, rope_mla, rope_gqa, mg_w_in[e], mla_g_cq[e],
                                         mla_g_ckv[e], mla_w_uq[e], mla_w_ukv[e], gqa_g_q[e],
                                         gqa_g_k[e], mg_w_o[e], need_ctx)
        else:
            o = i // 2
            lambda_init = 0.8 - 0.6 * math.exp(-0.3 * i)
            y_lat, y_ctx = diff_mixer(h_lat, h_ctx, rope_diff, diff_w_in[o], diff_lq1[o], diff_lk1[o],
                                      diff_lq2[o], diff_lk2[o], diff_g_sub[o], diff_w_o[o],
                                      lambda_init, need_ctx)
        x_lat = post_norm_residual(x_lat, mod_lat, 1, y_lat, ln_g[i, 1], ln_b[i, 1])

        x_lat = ffn_half_step(x_lat, mod_lat, 2, ffn_w1[i, 1], ffn_w3[i, 1], ffn_w2[i, 1], ln_g[i, 2], ln_b[i, 2])
        if need_ctx:
            x_ctx = post_norm_residual(x_ctx, mod_ctx, 1, y_ctx, ln_g[i, 1], ln_b[i, 1])
            x_ctx = ffn_half_step(x_ctx, mod_ctx, 2, ffn_w1[i, 1], ffn_w3[i, 1], ffn_w2[i, 1], ln_g[i, 2], ln_b[i, 2])
    return x_lat
```

```python
import functools
import math

import jax
import jax.numpy as jnp
from jax import lax
from jax.experimental import pallas as pl
from jax.experimental.pallas import tpu as pltpu

F32 = jnp.float32
BF16 = jnp.bfloat16

GRID_W = 64
ROPE_THETA = 10000.0
NORM_EPS = 1e-6
N_SUB = 3
HALF_STEP = 0.5
MLA_HEADS = 8
MLA_Q_RANK = 512
MLA_KV_RANK = 256
MLA_NOPE = 128
MLA_ROPE = 64
MLA_V = 128
MLA_SCALE = (MLA_NOPE + MLA_ROPE) ** -0.5
GQA_HEADS = 8
GQA_KV_HEADS = 2
GQA_HEAD_DIM = 128
GQA_SCALE = GQA_HEAD_DIM ** -0.5
DIFF_HEADS = 8
DIFF_HEAD_DIM = 128
DIFF_SCALE = DIFF_HEAD_DIM ** -0.5

LANES = 128
MOD_ROWS = 32
VMEM_LIMIT_BYTES = 56 * 1024 * 1024

TOKEN_TILE = 512
FF_TILE = 512
PROJ_TILE = 1024
ADA_TILE = 1024
Q_TILE = 512


def _divisor_tile(n, pref, mult):
    if n <= pref:
        return n
    t = (pref // mult) * mult
    while t >= mult:
        if n % t == 0:
            return t
        t -= mult
    raise ValueError(f"no tile for {n} (pref {pref}, mult {mult})")


def _params(*sem):
    return pltpu.CompilerParams(dimension_semantics=sem, vmem_limit_bytes=VMEM_LIMIT_BYTES)


def _layer_norm(r, g, b):
    mu = jnp.mean(r, axis=-1, keepdims=True)
    d = r - mu
    var = jnp.mean(d * d, axis=-1, keepdims=True)
    return d * lax.rsqrt(var + NORM_EPS) * g + b


def _rms_norm(x, g):
    return x * lax.rsqrt(jnp.mean(x * x, axis=-1, keepdims=True) + NORM_EPS) * g


def _rope_block(blk, cos, sin):
    return blk * cos + pltpu.roll(blk, LANES // 2, 1) * sin


def _dot(a, b):
    return jnp.dot(a, b, preferred_element_type=F32)


def _dot_nt(a, b):
    return lax.dot_general(a, b, (((1,), (1,)), ((), ())), preferred_element_type=F32)


def _ada_kernel(c_ref, w_ref, b_ref, o_ref):
    c = c_ref[...]
    s = (c * jax.nn.sigmoid(c)).astype(BF16)
    o_ref[...] = _dot(s, w_ref[...].astype(BF16)) + b_ref[...]


def _ada(cc, w_ada, b_ada):
    depth, d, n = w_ada.shape
    tn = _divisor_tile(n, ADA_TILE, LANES)
    return pl.pallas_call(
        _ada_kernel,
        grid=(depth, n // tn),
        in_specs=[
            pl.BlockSpec((MOD_ROWS, d), lambda l, j: (0, 0)),
            pl.BlockSpec((None, d, tn), lambda l, j: (l, 0, j)),
            pl.BlockSpec((None, 1, tn), lambda l, j: (l, 0, j)),
        ],
        out_specs=pl.BlockSpec((None, MOD_ROWS, tn), lambda l, j: (l, 0, j)),
        out_shape=jax.ShapeDtypeStruct((depth, MOD_ROWS, n), F32),
        compiler_params=_params("arbitrary", "arbitrary"),
        name="ada_mod",
    )(cc, w_ada, b_ada.reshape(depth, 1, n))


class _Mod:
    def __init__(self, table, layer, batch, seq, is_ctx):
        self.table, self.layer, self.batch, self.seq, self.is_ctx = table, layer, batch, seq, is_ctx

    def spec(self, d, sub, kind, tm):
        base = self.layer * MOD_ROWS
        off = 3 * sub + kind
        if self.is_ctx:
            row = (base + self.batch) * 3 * N_SUB + off
            return pl.BlockSpec((None, 1, d), lambda i, *_: (row, 0, 0))
        tpb = self.seq // tm
        return pl.BlockSpec((None, 1, d), lambda i, *_: ((base + i // tpb) * 3 * N_SUB + off, 0, 0))


def _ffn_kernel(x_ref, shift_ref, scale_ref, gate_ref, w1_ref, w3_ref, w2_ref, g_ref, b_ref, o_ref,
                hb_ref, acc_ref, *, alpha):
    f = pl.program_id(1)

    @pl.when(f == 0)
    def _():
        hb_ref[...] = (x_ref[...] * (1.0 + scale_ref[...]) + shift_ref[...]).astype(BF16)
        acc_ref[...] = jnp.zeros_like(acc_ref)

    hb = hb_ref[...]
    a = _dot(hb, w1_ref[...])
    u = _dot(hb, w3_ref[...])
    act = (a * jax.nn.sigmoid(a) * u).astype(BF16)
    acc_ref[...] += _dot(act, w2_ref[...])

    @pl.when(f == pl.num_programs(1) - 1)
    def _():
        r = alpha * x_ref[...] + gate_ref[...] * (HALF_STEP * acc_ref[...])
        o_ref[...] = _layer_norm(r, g_ref[...], b_ref[...])


def _ffn(x, mod, sub, w1, w3, w2, g, b, alpha):
    t, d = x.shape
    f = w1.shape[1]
    tm = _divisor_tile(t if mod.is_ctx else mod.seq, TOKEN_TILE, 8)
    tf = _divisor_tile(f, FF_TILE, LANES)
    row = pl.BlockSpec((tm, d), lambda i, k: (i, 0))
    vec = pl.BlockSpec((1, d), lambda i, k: (0, 0))
    return pl.pallas_call(
        functools.partial(_ffn_kernel, alpha=alpha),
        grid=(t // tm, f // tf),
        in_specs=[
            row,
            mod.spec(d, sub, 0, tm), mod.spec(d, sub, 1, tm), mod.spec(d, sub, 2, tm),
            pl.BlockSpec((d, tf), lambda i, k: (0, k)),
            pl.BlockSpec((d, tf), lambda i, k: (0, k)),
            pl.BlockSpec((tf, d), lambda i, k: (k, 0)),
            vec, vec,
        ],
        out_specs=row,
        out_shape=jax.ShapeDtypeStruct((t, d), F32),
        scratch_shapes=[pltpu.VMEM((tm, d), BF16), pltpu.VMEM((tm, d), F32)],
        compiler_params=_params("parallel", "arbitrary"),
        name="ffn_half_step",
    )(x, mod.table, mod.table, mod.table, w1, w3, w2, g.reshape(1, d), b.reshape(1, d))


def _proj_kernel(*refs, headnorm, rope):
    x_ref, shift_ref, scale_ref, w_ref = refs[:4]
    rest = list(refs[4:])
    gain_ref = rest.pop(0) if headnorm else None
    cos_ref, sin_ref = (rest.pop(0), rest.pop(0)) if rope else (None, None)
    o_ref, hb_ref = rest

    @pl.when(pl.program_id(1) == 0)
    def _():
        hb_ref[...] = (x_ref[...] * (1.0 + scale_ref[...]) + shift_ref[...]).astype(BF16)

    z = _dot(hb_ref[...], w_ref[...])
    if not (headnorm or rope):
        o_ref[...] = z.astype(o_ref.dtype)
        return
    for kb in range(z.shape[1] // LANES):
        cols = slice(kb * LANES, (kb + 1) * LANES)
        blk = z[:, cols]
        if headnorm:
            blk = _rms_norm(blk, gain_ref[:, cols])
        if rope:
            blk = _rope_block(blk, cos_ref[...], sin_ref[...])
        o_ref[:, cols] = blk.astype(o_ref.dtype)


def _proj(x, mod, sub, w, col0, ncols, out_dtype, gain=None, rope=None):
    t, d = x.shape
    tm = _divisor_tile(t if mod.is_ctx else mod.seq, TOKEN_TILE, 8)
    tn = _divisor_tile(ncols, PROJ_TILE, LANES)
    assert col0 % tn == 0
    joff = col0 // tn
    in_specs = [
        pl.BlockSpec((tm, d), lambda i, j: (i, 0)),
        mod.spec(d, sub, 0, tm), mod.spec(d, sub, 1, tm),
        pl.BlockSpec((d, tn), lambda i, j: (0, j + joff)),
    ]
    args = [x, mod.table, mod.table, w]
    if gain is not None:
        in_specs.append(pl.BlockSpec((1, tn), lambda i, j: (0, j)))
        args.append(gain)
    if rope is not None:
        tpb = mod.seq // tm
        tab = pl.BlockSpec((tm, LANES), lambda i, j: (i % tpb, 0))
        in_specs += [tab, tab]
        args += list(rope)
    return pl.pallas_call(
        functools.partial(_proj_kernel, headnorm=gain is not None, rope=rope is not None),
        grid=(t // tm, ncols // tn),
        in_specs=in_specs,
        out_specs=pl.BlockSpec((tm, tn), lambda i, j: (i, j)),
        out_shape=jax.ShapeDtypeStruct((t, ncols), out_dtype),
        scratch_shapes=[pltpu.VMEM((tm, d), BF16)],
        compiler_params=_params("parallel", "arbitrary"),
        name="mixer_in_proj",
    )(*args)


def _mla_up_kernel(*refs, rope):
    zm_ref, gcq_ref, gckv_ref, wuq_ref, wkn_ref, wv_ref = refs[:6]
    rest = list(refs[6:])
    cos_ref, sin_ref = (rest.pop(0), rest.pop(0)) if rope else (None, None)
    qa_ref, ka_ref, va_ref = rest
    hd = 2 * LANES

    def rot(blk):
        return _rope_block(blk, cos_ref[...], sin_ref[...]) if rope else blk

    nq = _rms_norm(zm_ref[:, :MLA_Q_RANK], gcq_ref[...]).astype(BF16)
    q = _dot(nq, wuq_ref[...])
    for h in range(MLA_HEADS):
        qa_ref[:, h * hd:h * hd + LANES] = q[:, h * hd:h * hd + LANES].astype(BF16)
        qa_ref[:, h * hd + LANES:(h + 1) * hd] = rot(q[:, h * hd + LANES:(h + 1) * hd]).astype(BF16)

    nkv = _rms_norm(zm_ref[:, MLA_Q_RANK:MLA_Q_RANK + MLA_KV_RANK], gckv_ref[...]).astype(BF16)
    kn = _dot(nkv, wkn_ref[...])
    va_ref[...] = _dot(nkv, wv_ref[...]).astype(BF16)
    kr = rot(zm_ref[:, MLA_Q_RANK + MLA_KV_RANK:MLA_Q_RANK + MLA_KV_RANK + LANES]).astype(BF16)
    for h in range(MLA_HEADS):
        ka_ref[:, h * hd:h * hd + LANES] = kn[:, h * LANES:(h + 1) * LANES].astype(BF16)
        ka_ref[:, h * hd + LANES:(h + 1) * hd] = kr


def _mla_up(zm, seq, g_cq, g_ckv, wuq, wkn, wv, rope):
    t, nz = zm.shape
    tm = _divisor_tile(seq if rope is not None else t, TOKEN_TILE, 8)
    hq = MLA_HEADS * 2 * LANES
    hv = MLA_HEADS * MLA_V
    full = lambda a: pl.BlockSpec(a.shape, lambda i: (0, 0))
    g_cq = g_cq.reshape(1, -1)
    g_ckv = g_ckv.reshape(1, -1)
    in_specs = [pl.BlockSpec((tm, nz), lambda i: (i, 0)), full(g_cq), full(g_ckv), full(wuq), full(wkn), full(wv)]
    args = [zm, g_cq, g_ckv, wuq, wkn, wv]
    if rope is not None:
        tpb = seq // tm
        tab = pl.BlockSpec((tm, LANES), lambda i: (i % tpb, 0))
        in_specs += [tab, tab]
        args += list(rope)
    return pl.pallas_call(
        functools.partial(_mla_up_kernel, rope=rope is not None),
        grid=(t // tm,),
        in_specs=in_specs,
        out_specs=[pl.BlockSpec((tm, hq), lambda i: (i, 0)), pl.BlockSpec((tm, hq), lambda i: (i, 0)),
                   pl.BlockSpec((tm, hv), lambda i: (i, 0))],
        out_shape=[jax.ShapeDtypeStruct((t, hq), BF16), jax.ShapeDtypeStruct((t, hq), BF16),
                   jax.ShapeDtypeStruct((t, hv), BF16)],
        compiler_params=_params("parallel"),
        name="mla_up_proj",
    )(*args)


def _attn_kernel(*refs, scale, has_lat):
    if has_lat:
        q_ref, kc_ref, vc_ref, kl_ref, vl_ref, o_ref = refs
    else:
        q_ref, kc_ref, vc_ref, o_ref = refs
    q = q_ref[...]
    sc = _dot_nt(q, kc_ref[...]) * scale
    m = jnp.max(sc, axis=-1, keepdims=True)
    if has_lat:
        sl = _dot_nt(q, kl_ref[...]) * scale
        m = jnp.maximum(m, jnp.max(sl, axis=-1, keepdims=True))
    ec = jnp.exp(sc - m)
    den = jnp.sum(ec, axis=-1, keepdims=True)
    o = _dot(ec.astype(BF16), vc_ref[...])
    if has_lat:
        el = jnp.exp(sl - m)
        den = den + jnp.sum(el, axis=-1, keepdims=True)
        o = o + _dot(el.astype(BF16), vl_ref[...])
    o_ref[...] = (o / den).astype(o_ref.dtype)


def _attention(q, kc, vc, kl, vl, *, heads, group, dqk, dv, q_col0, k_col0, v_col0, scale):
    b, lq, _ = q.shape
    lc = kc.shape[1]
    tq = _divisor_tile(lq, Q_TILE, 8)
    qo, ko, vo = q_col0 // dqk, k_col0 // dqk, v_col0 // dv
    in_specs = [
        pl.BlockSpec((None, tq, dqk), lambda bi, h, i: (bi, i, qo + h)),
        pl.BlockSpec((None, lc, dqk), lambda bi, h, i: (bi, 0, ko + h // group)),
        pl.BlockSpec((None, lc, dv), lambda bi, h, i: (bi, 0, vo + h // group)),
    ]
    args = [q, kc, vc]
    if kl is not None:
        ll = kl.shape[1]
        in_specs += [
            pl.BlockSpec((None, ll, dqk), lambda bi, h, i: (bi, 0, ko + h // group)),
            pl.BlockSpec((None, ll, dv), lambda bi, h, i: (bi, 0, vo + h // group)),
        ]
        args += [kl, vl]
    return pl.pallas_call(
        functools.partial(_attn_kernel, scale=scale, has_lat=kl is not None),
        grid=(b, heads, lq // tq),
        in_specs=in_specs,
        out_specs=pl.BlockSpec((None, tq, dv), lambda bi, h, i: (bi, i, h)),
        out_shape=jax.ShapeDtypeStruct((b, lq, heads * dv), BF16),
        compiler_params=_params("parallel", "arbitrary", "arbitrary"),
        name="softmax_attention",
    )(*args)


def _diff_attn_kernel(*refs, lambda_init, has_lat):
    if has_lat:
        q_ref, kc_ref, vc_ref, kl_ref, vl_ref, lq1, lk1, lq2, lk2, gs_ref, o_ref = refs
    else:
        q_ref, kc_ref, vc_ref, lq1, lk1, lq2, lk2, gs_ref, o_ref = refs
    lam = (jnp.exp(jnp.sum(lq1[...] * lk1[...], axis=-1, keepdims=True))
           - jnp.exp(jnp.sum(lq2[...] * lk2[...], axis=-1, keepdims=True)) + lambda_init)
    hd = DIFF_HEAD_DIM

    def probs(j):
        qj = q_ref[:, j * hd:(j + 1) * hd]
        sc = _dot_nt(qj, kc_ref[:, j * hd:(j + 1) * hd]) * DIFF_SCALE
        m = jnp.max(sc, axis=-1, keepdims=True)
        if has_lat:
            sl = _dot_nt(qj, kl_ref[:, j * hd:(j + 1) * hd]) * DIFF_SCALE
            m = jnp.maximum(m, jnp.max(sl, axis=-1, keepdims=True))
        ec = jnp.exp(sc - m)
        den = jnp.sum(ec, axis=-1, keepdims=True)
        el = None
        if has_lat:
            el = jnp.exp(sl - m)
            den = den + jnp.sum(el, axis=-1, keepdims=True)
        return ec, el, 1.0 / den

    ec0, el0, r0 = probs(0)
    ec1, el1, r1 = probs(1)
    r1 = lam * r1
    y = _dot((ec0 * r0 - ec1 * r1).astype(BF16), vc_ref[...])
    if has_lat:
        y = y + _dot((el0 * r0 - el1 * r1).astype(BF16), vl_ref[...])
    o_ref[...] = (_rms_norm(y, gs_ref[...]) * (1.0 - lambda_init)).astype(o_ref.dtype)


def _diff_attention(q, kc, vc, kl, vl, lams, g_sub, lambda_init):
    b, lq, _ = q.shape
    lc = kc.shape[1]
    tq = _divisor_tile(lq, Q_TILE, 8)
    w = 2 * DIFF_HEAD_DIM
    in_specs = [
        pl.BlockSpec((None, tq, w), lambda bi, h, i: (bi, i, h)),
        pl.BlockSpec((None, lc, w), lambda bi, h, i: (bi, 0, DIFF_HEADS + h)),
        pl.BlockSpec((None, lc, w), lambda bi, h, i: (bi, 0, h)),
    ]
    args = [q, kc, vc]
    if kl is not None:
        ll = kl.shape[1]
        in_specs += [
            pl.BlockSpec((None, ll, w), lambda bi, h, i: (bi, 0, DIFF_HEADS + h)),
            pl.BlockSpec((None, ll, w), lambda bi, h, i: (bi, 0, h)),
        ]
        args += [kl, vl]
    small = lambda n: pl.BlockSpec((1, n), lambda bi, h, i: (0, 0))
    in_specs += [small(DIFF_HEAD_DIM)] * 4 + [small(w)]
    args += [a.reshape(1, -1) for a in lams] + [g_sub.reshape(1, -1)]
    return pl.pallas_call(
        functools.partial(_diff_attn_kernel, lambda_init=lambda_init, has_lat=kl is not None),
        grid=(b, DIFF_HEADS, lq // tq),
        in_specs=in_specs,
        out_specs=pl.BlockSpec((None, tq, w), lambda bi, h, i: (bi, i, h)),
        out_shape=jax.ShapeDtypeStruct((b, lq, DIFF_HEADS * w), BF16),
        compiler_params=_params("parallel", "arbitrary", "arbitrary"),
        name="diff_attention",
    )(*args)


def _outproj_kernel(*refs, nparts, alpha):
    y_refs = refs[:nparts]
    w_refs = refs[nparts:2 * nparts]
    x_ref, gate_ref, g_ref, b_ref, o_ref = refs[2 * nparts:]
    y = _dot(y_refs[0][...], w_refs[0][...])
    for p in range(1, nparts):
        y = y + _dot(y_refs[p][...], w_refs[p][...])
    r = alpha * x_ref[...] + gate_ref[...] * y
    o_ref[...] = _layer_norm(r, g_ref[...], b_ref[...])


def _outproj(ys, w_o, x, mod, sub, g, b, alpha):
    t, d = x.shape
    tm = _divisor_tile(t if mod.is_ctx else mod.seq, TOKEN_TILE, 8)
    kp = ys[0].shape[1]
    assert all(y.shape[1] == kp for y in ys) and kp * len(ys) == w_o.shape[0]
    vec = pl.BlockSpec((1, d), lambda i: (0, 0))
    in_specs = [pl.BlockSpec((tm, kp), lambda i: (i, 0)) for _ in ys]
    in_specs += [pl.BlockSpec((kp, d), functools.partial(lambda i, p: (p, 0), p=p)) for p in range(len(ys))]
    in_specs += [pl.BlockSpec((tm, d), lambda i: (i, 0)), mod.spec(d, sub, 2, tm), vec, vec]
    return pl.pallas_call(
        functools.partial(_outproj_kernel, nparts=len(ys), alpha=alpha),
        grid=(t // tm,),
        in_specs=in_specs,
        out_specs=pl.BlockSpec((tm, d), lambda i: (i, 0)),
        out_shape=jax.ShapeDtypeStruct((t, d), F32),
        compiler_params=_params("parallel"),
        name="mixer_out_proj",
    )(*ys, *([w_o] * len(ys)), x, mod.table, g.reshape(1, d), b.reshape(1, d))


def _rope_tables(rows, rot_dim):
    r, col = jnp.meshgrid(jnp.arange(rows, dtype=F32), jnp.arange(GRID_W, dtype=F32), indexing="ij")
    n_freq = rot_dim // 4
    inv_freq = ROPE_THETA ** (-jnp.arange(n_freq, dtype=F32) / n_freq)
    ang = jnp.concatenate([r.reshape(-1, 1) * inv_freq, col.reshape(-1, 1) * inv_freq], -1)
    cos, sin = jnp.cos(ang), jnp.sin(ang)
    pad = LANES // 2 - rot_dim // 2
    cos = jnp.pad(cos, ((0, 0), (0, pad)), constant_values=1.0)
    sin = jnp.pad(sin, ((0, 0), (0, pad)))
    return jnp.concatenate([cos, cos], -1), jnp.concatenate([-sin, sin], -1)


def _spread_rot(w):
    half = MLA_ROPE // 2
    z = jnp.zeros((w.shape[0], LANES // 2 - half), w.dtype)
    return jnp.concatenate([w[:, :half], z, w[:, half:], z], -1)


def kernel(x, c, ctx, c_ctx, w_ada, b_ada, ln_g, ln_b, ffn_w1, ffn_w3, ffn_w2, mg_w_in, mla_g_cq, mla_g_ckv,
           mla_w_uq, mla_w_ukv, gqa_g_q, gqa_g_k, mg_w_o, diff_w_in, diff_lq1, diff_lk1, diff_lq2, diff_lk2,
           diff_g_sub, diff_w_o):
    bsz, seq, d = x.shape
    lc = ctx.shape[1]
    depth = w_ada.shape[0]
    assert bsz + 1 <= MOD_ROWS and seq % GRID_W == 0
    alpha = (2 * depth) ** 0.25
    rows = seq // GRID_W
    rope_mla = _rope_tables(rows, MLA_ROPE)
    rope_hd = _rope_tables(rows, GQA_HEAD_DIM)

    cc = jnp.concatenate([c, c_ctx[None], jnp.zeros((MOD_ROWS - bsz - 1, d), F32)], 0)
    mod_table = _ada(cc, w_ada, b_ada).reshape(depth * MOD_ROWS * 3 * N_SUB, 1, d)

    x_lat = x.reshape(bsz * seq, d)
    x_ctx = ctx.reshape(bsz * lc, d)
    w1b, w3b, w2b = ffn_w1.astype(BF16), ffn_w3.astype(BF16), ffn_w2.astype(BF16)

    for i in range(depth):
        need_ctx = i < depth - 1
        mod_l = _Mod(mod_table, i, bsz, seq, False)
        mod_c = _Mod(mod_table, i, bsz, seq, True)
        ffn = lambda xs, m, sub, k: _ffn(xs, m, sub, w1b[i, k], w3b[i, k], w2b[i, k], ln_g[i, sub], ln_b[i, sub], alpha)

        x_lat = ffn(x_lat, mod_l, 0, 0)
        x_ctx = ffn(x_ctx, mod_c, 0, 0)

        if i % 2 == 0:
            e = i // 2
            w_in = mg_w_in[e]
            o_kr = MLA_Q_RANK + MLA_KV_RANK
            o_gq = o_kr + MLA_ROPE
            n_gqk = (GQA_HEADS + GQA_KV_HEADS) * GQA_HEAD_DIM
            w_mla = jnp.concatenate([w_in[:, :o_kr], _spread_rot(w_in[:, o_kr:o_gq])], -1).astype(BF16)
            w_gqa = w_in[:, o_gq:].astype(BF16)
            wuq = mla_w_uq[e].reshape(MLA_Q_RANK, MLA_HEADS, MLA_NOPE + MLA_ROPE)
            wuq = jnp.concatenate(
                [wuq[:, :, :MLA_NOPE],
                 _spread_rot(wuq[:, :, MLA_NOPE:].reshape(-1, MLA_ROPE)).reshape(MLA_Q_RANK, MLA_HEADS, LANES)],
                -1).reshape(MLA_Q_RANK, MLA_HEADS * 2 * LANES).astype(BF16)
            wukv = mla_w_ukv[e].reshape(MLA_KV_RANK, MLA_HEADS, MLA_NOPE + MLA_V)
            wkn = wukv[:, :, :MLA_NOPE].reshape(MLA_KV_RANK, -1).astype(BF16)
            wv = wukv[:, :, MLA_NOPE:].reshape(MLA_KV_RANK, -1).astype(BF16)
            gain = jnp.concatenate([jnp.tile(gqa_g_q[e], GQA_HEADS), jnp.tile(gqa_g_k[e], GQA_KV_HEADS)])[None]
            w_o = mg_w_o[e].astype(BF16)

            def project(xs, m, rp_mla, rp_hd):
                zm = _proj(xs, m, 1, w_mla, 0, w_mla.shape[1], F32)
                qk = _proj(xs, m, 1, w_gqa, 0, n_gqk, BF16, gain=gain, rope=rp_hd)
                v = _proj(xs, m, 1, w_gqa, n_gqk, w_gqa.shape[1] - n_gqk, BF16)
                qa, ka, va = _mla_up(zm, seq, mla_g_cq[e], mla_g_ckv[e], wuq, wkn, wv, rp_mla)
                n = xs.shape[0] // bsz
                return [a.reshape(bsz, n, -1) for a in (qa, ka, va, qk, v)]

            qa, ka, va, qkb, vb = project(x_lat, mod_l, rope_mla, rope_hd)
            qa_c, ka_c, va_c, qkb_c, vb_c = project(x_ctx, mod_c, None, None)
            mla = dict(heads=MLA_HEADS, group=1, dqk=2 * LANES, dv=MLA_V, q_col0=0, k_col0=0, v_col0=0,
                       scale=MLA_SCALE)
            gqa = dict(heads=GQA_HEADS, group=GQA_HEADS // GQA_KV_HEADS, dqk=GQA_HEAD_DIM, dv=GQA_HEAD_DIM,
                       q_col0=0, k_col0=GQA_HEADS * GQA_HEAD_DIM, v_col0=0, scale=GQA_SCALE)
            ya = _attention(qa, ka_c, va_c, ka, va, **mla)
            yb = _attention(qkb, qkb_c, vb_c, qkb, vb, **gqa)
            ys_lat = [ya.reshape(bsz * seq, -1), yb.reshape(bsz * seq, -1)]
            if need_ctx:
                ya_c = _attention(qa_c, ka_c, va_c, None, None, **mla)
                yb_c = _attention(qkb_c, qkb_c, vb_c, None, None, **gqa)
                ys_ctx = [ya_c.reshape(bsz * lc, -1), yb_c.reshape(bsz * lc, -1)]
        else:
            o = i // 2
            lambda_init = 0.8 - 0.6 * math.exp(-0.3 * i)
            w_in = diff_w_in[o].astype(BF16)
            n_qk = 2 * DIFF_HEADS * 2 * DIFF_HEAD_DIM
            w_o = diff_w_o[o].astype(BF16)
            lams = (diff_lq1[o], diff_lk1[o], diff_lq2[o], diff_lk2[o])

            def project(xs, m, rp):
                qk = _proj(xs, m, 1, w_in, 0, n_qk, BF16, rope=rp)
                v = _proj(xs, m, 1, w_in, n_qk, w_in.shape[1] - n_qk, BF16)
                n = xs.shape[0] // bsz
                return qk.reshape(bsz, n, -1), v.reshape(bsz, n, -1)

            qk, v = project(x_lat, mod_l, rope_hd)
            qk_c, v_c = project(x_ctx, mod_c, None)
            y = _diff_attention(qk, qk_c, v_c, qk, v, lams, diff_g_sub[o], lambda_init)
            ys_lat = [y.reshape(bsz * seq, -1)]
            if need_ctx:
                y_c = _diff_attention(qk_c, qk_c, v_c, None, None, lams, diff_g_sub[o], lambda_init)
                ys_ctx = [y_c.reshape(bsz * lc, -1)]

        x_lat = _outproj(ys_lat, w_o, x_lat, mod_l, 1, ln_g[i, 1], ln_b[i, 1], alpha)
        x_lat = ffn(x_lat, mod_l, 2, 1)
        if need_ctx:
            x_ctx = _outproj(ys_ctx, w_o, x_ctx, mod_c, 1, ln_g[i, 1], ln_b[i, 1], alpha)
            x_ctx = ffn(x_ctx, mod_c, 2, 1)
    return x_lat.reshape(bsz, seq, d)
```

```python
import functools
import math

import jax
import jax.numpy as jnp
from jax import lax
from jax.experimental import pallas as pl
from jax.experimental.pallas import tpu as pltpu

F32 = jnp.float32
BF16 = jnp.bfloat16

GRID_W = 64
ROPE_THETA = 10000.0
NORM_EPS = 1e-6
N_SUB = 3
HALF_STEP = 0.5
MLA_HEADS = 8
MLA_Q_RANK = 512
MLA_KV_RANK = 256
MLA_NOPE = 128
MLA_ROPE = 64
MLA_V = 128
MLA_SCALE = (MLA_NOPE + MLA_ROPE) ** -0.5
GQA_HEADS = 8
GQA_KV_HEADS = 2
GQA_HEAD_DIM = 128
GQA_SCALE = GQA_HEAD_DIM ** -0.5
DIFF_HEADS = 8
DIFF_HEAD_DIM = 128
DIFF_SCALE = DIFF_HEAD_DIM ** -0.5

LANES = 128
MOD_ROWS = 32
VMEM_LIMIT_BYTES = 56 * 1024 * 1024

TOKEN_TILE = 512
FF_TILE = 512
PROJ_TILE = 1024
ADA_TILE = 1024
Q_TILE = 2048
ROW_CHUNK = 256
LOG2_E = math.log2(math.e)


def _divisor_tile(n, pref, mult):
    if n <= pref:
        return n
    t = (pref // mult) * mult
    while t >= mult:
        if n % t == 0:
            return t
        t -= mult
    raise ValueError(f"no tile for {n} (pref {pref}, mult {mult})")


def _params(*sem):
    return pltpu.CompilerParams(dimension_semantics=sem, vmem_limit_bytes=VMEM_LIMIT_BYTES)


def _layer_norm(r, g, b):
    mu = jnp.mean(r, axis=-1, keepdims=True)
    d = r - mu
    var = jnp.mean(d * d, axis=-1, keepdims=True)
    return d * lax.rsqrt(var + NORM_EPS) * g + b


def _rms_norm(x, g):
    return x * lax.rsqrt(jnp.mean(x * x, axis=-1, keepdims=True) + NORM_EPS) * g


def _rope_block(blk, cos, sin):
    return blk * cos + pltpu.roll(blk, LANES // 2, 1) * sin


def _dot(a, b):
    return jnp.dot(a, b, preferred_element_type=F32)


def _dot_nt(a, b):
    return lax.dot_general(a, b, (((1,), (1,)), ((), ())), preferred_element_type=F32)


def _ada_kernel(c_ref, w_ref, b_ref, o_ref):
    c = c_ref[...]
    s = (c * jax.nn.sigmoid(c)).astype(BF16)
    o_ref[...] = _dot(s, w_ref[...].astype(BF16)) + b_ref[...]


def _ada(cc, w_ada, b_ada):
    depth, d, n = w_ada.shape
    tn = _divisor_tile(n, ADA_TILE, LANES)
    return pl.pallas_call(
        _ada_kernel,
        grid=(depth, n // tn),
        in_specs=[
            pl.BlockSpec((MOD_ROWS, d), lambda l, j: (0, 0)),
            pl.BlockSpec((None, d, tn), lambda l, j: (l, 0, j)),
            pl.BlockSpec((None, 1, tn), lambda l, j: (l, 0, j)),
        ],
        out_specs=pl.BlockSpec((None, MOD_ROWS, tn), lambda l, j: (l, 0, j)),
        out_shape=jax.ShapeDtypeStruct((depth, MOD_ROWS, n), F32),
        compiler_params=_params("arbitrary", "arbitrary"),
        name="ada_mod",
    )(cc, w_ada, b_ada.reshape(depth, 1, n))


class _Mod:
    def __init__(self, table, layer, batch, seq, is_ctx):
        self.table, self.layer, self.batch, self.seq, self.is_ctx = table, layer, batch, seq, is_ctx

    def spec(self, d, sub, kind, tm):
        base = self.layer * MOD_ROWS
        off = 3 * sub + kind
        if self.is_ctx:
            row = (base + self.batch) * 3 * N_SUB + off
            return pl.BlockSpec((None, 1, d), lambda i, *_: (row, 0, 0))
        tpb = self.seq // tm
        return pl.BlockSpec((None, 1, d), lambda i, *_: ((base + i // tpb) * 3 * N_SUB + off, 0, 0))


def _ffn_kernel(x_ref, shift_ref, scale_ref, gate_ref, w1_ref, w3_ref, w2_ref, g_ref, b_ref, o_ref,
                hb_ref, acc_ref, *, alpha):
    f = pl.program_id(1)

    @pl.when(f == 0)
    def _():
        hb_ref[...] = (x_ref[...] * (1.0 + scale_ref[...]) + shift_ref[...]).astype(BF16)
        acc_ref[...] = jnp.zeros_like(acc_ref)

    hb = hb_ref[...]
    a = _dot(hb, w1_ref[...])
    u = _dot(hb, w3_ref[...])
    act = (a * jax.nn.sigmoid(a) * u).astype(BF16)
    acc_ref[...] += _dot(act, w2_ref[...])

    @pl.when(f == pl.num_programs(1) - 1)
    def _():
        r = alpha * x_ref[...] + gate_ref[...] * (HALF_STEP * acc_ref[...])
        o_ref[...] = _layer_norm(r, g_ref[...], b_ref[...])


def _ffn(x, mod, sub, w1, w3, w2, g, b, alpha):
    t, d = x.shape
    f = w1.shape[1]
    tm = _divisor_tile(t if mod.is_ctx else mod.seq, TOKEN_TILE, 8)
    tf = _divisor_tile(f, FF_TILE, LANES)
    row = pl.BlockSpec((tm, d), lambda i, k: (i, 0))
    vec = pl.BlockSpec((1, d), lambda i, k: (0, 0))
    return pl.pallas_call(
        functools.partial(_ffn_kernel, alpha=alpha),
        grid=(t // tm, f // tf),
        in_specs=[
            row,
            mod.spec(d, sub, 0, tm), mod.spec(d, sub, 1, tm), mod.spec(d, sub, 2, tm),
            pl.BlockSpec((d, tf), lambda i, k: (0, k)),
            pl.BlockSpec((d, tf), lambda i, k: (0, k)),
            pl.BlockSpec((tf, d), lambda i, k: (k, 0)),
            vec, vec,
        ],
        out_specs=row,
        out_shape=jax.ShapeDtypeStruct((t, d), F32),
        scratch_shapes=[pltpu.VMEM((tm, d), BF16), pltpu.VMEM((tm, d), F32)],
        compiler_params=_params("parallel", "arbitrary"),
        name="ffn_half_step",
    )(x, mod.table, mod.table, mod.table, w1, w3, w2, g.reshape(1, d), b.reshape(1, d))


def _proj_kernel(*refs, headnorm, rope):
    x_ref, shift_ref, scale_ref, w_ref = refs[:4]
    rest = list(refs[4:])
    gain_ref = rest.pop(0) if headnorm else None
    cos_ref, sin_ref = (rest.pop(0), rest.pop(0)) if rope else (None, None)
    o_ref, hb_ref = rest

    @pl.when(pl.program_id(1) == 0)
    def _():
        hb_ref[...] = (x_ref[...] * (1.0 + scale_ref[...]) + shift_ref[...]).astype(BF16)

    z = _dot(hb_ref[...], w_ref[...])
    if not (headnorm or rope):
        o_ref[...] = z.astype(o_ref.dtype)
        return
    for kb in range(z.shape[1] // LANES):
        cols = slice(kb * LANES, (kb + 1) * LANES)
        blk = z[:, cols]
        if headnorm:
            blk = _rms_norm(blk, gain_ref[:, cols])
        if rope:
            blk = _rope_block(blk, cos_ref[...], sin_ref[...])
        o_ref[:, cols] = blk.astype(o_ref.dtype)


def _proj(x, mod, sub, w, col0, ncols, out_dtype, gain=None, rope=None):
    t, d = x.shape
    tm = _divisor_tile(t if mod.is_ctx else mod.seq, TOKEN_TILE, 8)
    tn = _divisor_tile(ncols, PROJ_TILE, LANES)
    assert col0 % tn == 0
    joff = col0 // tn
    in_specs = [
        pl.BlockSpec((tm, d), lambda i, j: (i, 0)),
        mod.spec(d, sub, 0, tm), mod.spec(d, sub, 1, tm),
        pl.BlockSpec((d, tn), lambda i, j: (0, j + joff)),
    ]
    args = [x, mod.table, mod.table, w]
    if gain is not None:
        in_specs.append(pl.BlockSpec((1, tn), lambda i, j: (0, j)))
        args.append(gain)
    if rope is not None:
        tpb = mod.seq // tm
        tab = pl.BlockSpec((tm, LANES), lambda i, j: (i % tpb, 0))
        in_specs += [tab, tab]
        args += list(rope)
    return pl.pallas_call(
        functools.partial(_proj_kernel, headnorm=gain is not None, rope=rope is not None),
        grid=(t // tm, ncols // tn),
        in_specs=in_specs,
        out_specs=pl.BlockSpec((tm, tn), lambda i, j: (i, j)),
        out_shape=jax.ShapeDtypeStruct((t, ncols), out_dtype),
        scratch_shapes=[pltpu.VMEM((tm, d), BF16)],
        compiler_params=_params("parallel", "arbitrary"),
        name="mixer_in_proj",
    )(*args)


def _mla_up_kernel(*refs, rope):
    zm_ref, gcq_ref, gckv_ref, wuq_ref, wkn_ref, wv_ref = refs[:6]
    rest = list(refs[6:])
    cos_ref, sin_ref = (rest.pop(0), rest.pop(0)) if rope else (None, None)
    qa_ref, ka_ref, va_ref = rest
    hd = 2 * LANES

    def rot(blk):
        return _rope_block(blk, cos_ref[...], sin_ref[...]) if rope else blk

    nq = _rms_norm(zm_ref[:, :MLA_Q_RANK], gcq_ref[...]).astype(BF16)
    q = _dot(nq, wuq_ref[...])
    for h in range(MLA_HEADS):
        qa_ref[:, h * hd:h * hd + LANES] = q[:, h * hd:h * hd + LANES].astype(BF16)
        qa_ref[:, h * hd + LANES:(h + 1) * hd] = rot(q[:, h * hd + LANES:(h + 1) * hd]).astype(BF16)

    nkv = _rms_norm(zm_ref[:, MLA_Q_RANK:MLA_Q_RANK + MLA_KV_RANK], gckv_ref[...]).astype(BF16)
    kn = _dot(nkv, wkn_ref[...])
    va_ref[...] = _dot(nkv, wv_ref[...]).astype(BF16)
    kr = rot(zm_ref[:, MLA_Q_RANK + MLA_KV_RANK:MLA_Q_RANK + MLA_KV_RANK + LANES]).astype(BF16)
    for h in range(MLA_HEADS):
        ka_ref[:, h * hd:h * hd + LANES] = kn[:, h * LANES:(h + 1) * LANES].astype(BF16)
        ka_ref[:, h * hd + LANES:(h + 1) * hd] = kr


def _mla_up(zm, seq, g_cq, g_ckv, wuq, wkn, wv, rope):
    t, nz = zm.shape
    tm = _divisor_tile(seq if rope is not None else t, TOKEN_TILE, 8)
    hq = MLA_HEADS * 2 * LANES
    hv = MLA_HEADS * MLA_V
    full = lambda a: pl.BlockSpec(a.shape, lambda i: (0, 0))
    g_cq = g_cq.reshape(1, -1)
    g_ckv = g_ckv.reshape(1, -1)
    in_specs = [pl.BlockSpec((tm, nz), lambda i: (i, 0)), full(g_cq), full(g_ckv), full(wuq), full(wkn), full(wv)]
    args = [zm, g_cq, g_ckv, wuq, wkn, wv]
    if rope is not None:
        tpb = seq // tm
        tab = pl.BlockSpec((tm, LANES), lambda i: (i % tpb, 0))
        in_specs += [tab, tab]
        args += list(rope)
    return pl.pallas_call(
        functools.partial(_mla_up_kernel, rope=rope is not None),
        grid=(t // tm,),
        in_specs=in_specs,
        out_specs=[pl.BlockSpec((tm, hq), lambda i: (i, 0)), pl.BlockSpec((tm, hq), lambda i: (i, 0)),
                   pl.BlockSpec((tm, hv), lambda i: (i, 0))],
        out_shape=[jax.ShapeDtypeStruct((t, hq), BF16), jax.ShapeDtypeStruct((t, hq), BF16),
                   jax.ShapeDtypeStruct((t, hv), BF16)],
        compiler_params=_params("parallel"),
        name="mla_up_proj",
    )(*args)


def _raw_scores(q, kc_ref, kl_ref, kcols):
    sc = _dot_nt(q, kc_ref[:, kcols])
    sl = None if kl_ref is None else _dot_nt(q, kl_ref[:, kcols])
    return sc, sl


def _softmax_numerators(scores, coef):
    sc, sl = scores
    m = jnp.max(sc, axis=-1, keepdims=True)
    if sl is not None:
        m = jnp.maximum(m, jnp.max(sl, axis=-1, keepdims=True))
    ec = jnp.exp2((sc - m) * coef)
    den = jnp.sum(ec, axis=-1, keepdims=True)
    el = None
    if sl is not None:
        el = jnp.exp2((sl - m) * coef)
        den = den + jnp.sum(el, axis=-1, keepdims=True)
    return ec, el, den


def _skewed_chunks(n, score_fn, finish_fn):
    nxt = score_fn(0)
    for r in range(n):
        cur, nxt = nxt, (score_fn(r + 1) if r + 1 < n else None)
        finish_fn(r, cur)


def _attn_kernel(*refs, scale, has_lat, row_chunk):
    if has_lat:
        q_ref, kc_ref, vc_ref, kl_ref, vl_ref, o_ref = refs
    else:
        (q_ref, kc_ref, vc_ref, o_ref), kl_ref, vl_ref = refs, None, None
    coef = scale * LOG2_E
    rows = lambda r: slice(r * row_chunk, (r + 1) * row_chunk)

    def finish(r, scores):
        ec, el, den = _softmax_numerators(scores, coef)
        o = _dot(ec.astype(BF16), vc_ref[...])
        if has_lat:
            o = o + _dot(el.astype(BF16), vl_ref[...])
        o_ref[rows(r), :] = (o / den).astype(o_ref.dtype)

    _skewed_chunks(q_ref.shape[0] // row_chunk,
                   lambda r: _raw_scores(q_ref[rows(r), :], kc_ref, kl_ref, slice(None)), finish)


def _attention(q, kc, vc, kl, vl, *, heads, group, dqk, dv, q_col0, k_col0, v_col0, scale):
    b, lq, _ = q.shape
    lc = kc.shape[1]
    tq = _divisor_tile(lq, Q_TILE, 8)
    qo, ko, vo = q_col0 // dqk, k_col0 // dqk, v_col0 // dv
    in_specs = [
        pl.BlockSpec((None, tq, dqk), lambda bi, h, i: (bi, i, qo + h)),
        pl.BlockSpec((None, lc, dqk), lambda bi, h, i: (bi, 0, ko + h // group)),
        pl.BlockSpec((None, lc, dv), lambda bi, h, i: (bi, 0, vo + h // group)),
    ]
    args = [q, kc, vc]
    if kl is not None:
        ll = kl.shape[1]
        in_specs += [
            pl.BlockSpec((None, ll, dqk), lambda bi, h, i: (bi, 0, ko + h // group)),
            pl.BlockSpec((None, ll, dv), lambda bi, h, i: (bi, 0, vo + h // group)),
        ]
        args += [kl, vl]
    return pl.pallas_call(
        functools.partial(_attn_kernel, scale=scale, has_lat=kl is not None, row_chunk=min(ROW_CHUNK, tq)),
        grid=(b, heads, lq // tq),
        in_specs=in_specs,
        out_specs=pl.BlockSpec((None, tq, dv), lambda bi, h, i: (bi, i, h)),
        out_shape=jax.ShapeDtypeStruct((b, lq, heads * dv), BF16),
        compiler_params=_params("parallel", "arbitrary", "arbitrary"),
        name="softmax_attention",
    )(*args)


def _diff_attn_kernel(*refs, lambda_init, has_lat, row_chunk):
    if has_lat:
        q_ref, kc_ref, vc_ref, kl_ref, vl_ref, lq1, lk1, lq2, lk2, gs_ref, o_ref = refs
    else:
        (q_ref, kc_ref, vc_ref, lq1, lk1, lq2, lk2, gs_ref, o_ref), kl_ref, vl_ref = refs, None, None
    lam = (jnp.exp(jnp.sum(lq1[...] * lk1[...], axis=-1, keepdims=True))
           - jnp.exp(jnp.sum(lq2[...] * lk2[...], axis=-1, keepdims=True)) + lambda_init)
    hd = DIFF_HEAD_DIM
    coef = DIFF_SCALE * LOG2_E
    rows = lambda r: slice(r * row_chunk, (r + 1) * row_chunk)

    def scores(r):
        return (_raw_scores(q_ref[rows(r), :hd], kc_ref, kl_ref, slice(0, hd)),
                _raw_scores(q_ref[rows(r), hd:], kc_ref, kl_ref, slice(hd, 2 * hd)))

    def finish(r, s01):
        ec0, el0, den0 = _softmax_numerators(s01[0], coef)
        ec1, el1, den1 = _softmax_numerators(s01[1], coef)
        ratio = lam * den0 / den1
        y = _dot((ec0 - ec1 * ratio).astype(BF16), vc_ref[...])
        if has_lat:
            y = y + _dot((el0 - el1 * ratio).astype(BF16), vl_ref[...])
        y = y / den0
        o_ref[rows(r), :] = (_rms_norm(y, gs_ref[...]) * (1.0 - lambda_init)).astype(o_ref.dtype)

    _skewed_chunks(q_ref.shape[0] // row_chunk, scores, finish)


def _diff_attention(q, kc, vc, kl, vl, lams, g_sub, lambda_init):
    b, lq, _ = q.shape
    lc = kc.shape[1]
    tq = _divisor_tile(lq, Q_TILE, 8)
    w = 2 * DIFF_HEAD_DIM
    in_specs = [
        pl.BlockSpec((None, tq, w), lambda bi, h, i: (bi, i, h)),
        pl.BlockSpec((None, lc, w), lambda bi, h, i: (bi, 0, DIFF_HEADS + h)),
        pl.BlockSpec((None, lc, w), lambda bi, h, i: (bi, 0, h)),
    ]
    args = [q, kc, vc]
    if kl is not None:
        ll = kl.shape[1]
        in_specs += [
            pl.BlockSpec((None, ll, w), lambda bi, h, i: (bi, 0, DIFF_HEADS + h)),
            pl.BlockSpec((None, ll, w), lambda bi, h, i: (bi, 0, h)),
        ]
        args += [kl, vl]
    small = lambda n: pl.BlockSpec((1, n), lambda bi, h, i: (0, 0))
    in_specs += [small(DIFF_HEAD_DIM)] * 4 + [small(w)]
    args += [a.reshape(1, -1) for a in lams] + [g_sub.reshape(1, -1)]
    return pl.pallas_call(
        functools.partial(_diff_attn_kernel, lambda_init=lambda_init, has_lat=kl is not None,
                          row_chunk=min(ROW_CHUNK, tq)),
        grid=(b, DIFF_HEADS, lq // tq),
        in_specs=in_specs,
        out_specs=pl.BlockSpec((None, tq, w), lambda bi, h, i: (bi, i, h)),
        out_shape=jax.ShapeDtypeStruct((b, lq, DIFF_HEADS * w), BF16),
        compiler_params=_params("parallel", "arbitrary", "arbitrary"),
        name="diff_attention",
    )(*args)


def _outproj_kernel(*refs, nparts, alpha):
    y_refs = refs[:nparts]
    w_refs = refs[nparts:2 * nparts]
    x_ref, gate_ref, g_ref, b_ref, o_ref = refs[2 * nparts:]
    y = _dot(y_refs[0][...], w_refs[0][...])
    for p in range(1, nparts):
        y = y + _dot(y_refs[p][...], w_refs[p][...])
    r = alpha * x_ref[...] + gate_ref[...] * y
    o_ref[...] = _layer_norm(r, g_ref[...], b_ref[...])


def _outproj(ys, w_o, x, mod, sub, g, b, alpha):
    t, d = x.shape
    tm = _divisor_tile(t if mod.is_ctx else mod.seq, TOKEN_TILE, 8)
    kp = ys[0].shape[1]
    assert all(y.shape[1] == kp for y in ys) and kp * len(ys) == w_o.shape[0]
    vec = pl.BlockSpec((1, d), lambda i: (0, 0))
    in_specs = [pl.BlockSpec((tm, kp), lambda i: (i, 0)) for _ in ys]
    in_specs += [pl.BlockSpec((kp, d), functools.partial(lambda i, p: (p, 0), p=p)) for p in range(len(ys))]
    in_specs += [pl.BlockSpec((tm, d), lambda i: (i, 0)), mod.spec(d, sub, 2, tm), vec, vec]
    return pl.pallas_call(
        functools.partial(_outproj_kernel, nparts=len(ys), alpha=alpha),
        grid=(t // tm,),
        in_specs=in_specs,
        out_specs=pl.BlockSpec((tm, d), lambda i: (i, 0)),
        out_shape=jax.ShapeDtypeStruct((t, d), F32),
        compiler_params=_params("parallel"),
        name="mixer_out_proj",
    )(*ys, *([w_o] * len(ys)), x, mod.table, g.reshape(1, d), b.reshape(1, d))


def _rope_tables(rows, rot_dim):
    r, col = jnp.meshgrid(jnp.arange(rows, dtype=F32), jnp.arange(GRID_W, dtype=F32), indexing="ij")
    n_freq = rot_dim // 4
    inv_freq = ROPE_THETA ** (-jnp.arange(n_freq, dtype=F32) / n_freq)
    ang = jnp.concatenate([r.reshape(-1, 1) * inv_freq, col.reshape(-1, 1) * inv_freq], -1)
    cos, sin = jnp.cos(ang), jnp.sin(ang)
    pad = LANES // 2 - rot_dim // 2
    cos = jnp.pad(cos, ((0, 0), (0, pad)), constant_values=1.0)
    sin = jnp.pad(sin, ((0, 0), (0, pad)))
    return jnp.concatenate([cos, cos], -1), jnp.concatenate([-sin, sin], -1)


def _spread_rot(w):
    half = MLA_ROPE // 2
    z = jnp.zeros((w.shape[0], LANES // 2 - half), w.dtype)
    return jnp.concatenate([w[:, :half], z, w[:, half:], z], -1)


def kernel(x, c, ctx, c_ctx, w_ada, b_ada, ln_g, ln_b, ffn_w1, ffn_w3, ffn_w2, mg_w_in, mla_g_cq, mla_g_ckv,
           mla_w_uq, mla_w_ukv, gqa_g_q, gqa_g_k, mg_w_o, diff_w_in, diff_lq1, diff_lk1, diff_lq2, diff_lk2,
           diff_g_sub, diff_w_o):
    bsz, seq, d = x.shape
    lc = ctx.shape[1]
    depth = w_ada.shape[0]
    assert bsz + 1 <= MOD_ROWS and seq % GRID_W == 0
    alpha = (2 * depth) ** 0.25
    rows = seq // GRID_W
    rope_mla = _rope_tables(rows, MLA_ROPE)
    rope_hd = _rope_tables(rows, GQA_HEAD_DIM)

    cc = jnp.concatenate([c, c_ctx[None], jnp.zeros((MOD_ROWS - bsz - 1, d), F32)], 0)
    mod_table = _ada(cc, w_ada, b_ada).reshape(depth * MOD_ROWS * 3 * N_SUB, 1, d)

    x_lat = x.reshape(bsz * seq, d)
    x_ctx = ctx.reshape(bsz * lc, d)
    w1b, w3b, w2b = ffn_w1.astype(BF16), ffn_w3.astype(BF16), ffn_w2.astype(BF16)

    for i in range(depth):
        need_ctx = i < depth - 1
        mod_l = _Mod(mod_table, i, bsz, seq, False)
        mod_c = _Mod(mod_table, i, bsz, seq, True)
        ffn = lambda xs, m, sub, k: _ffn(xs, m, sub, w1b[i, k], w3b[i, k], w2b[i, k], ln_g[i, sub], ln_b[i, sub], alpha)

        x_lat = ffn(x_lat, mod_l, 0, 0)
        x_ctx = ffn(x_ctx, mod_c, 0, 0)

        if i % 2 == 0:
            e = i // 2
            w_in = mg_w_in[e]
            o_kr = MLA_Q_RANK + MLA_KV_RANK
            o_gq = o_kr + MLA_ROPE
            n_gqk = (GQA_HEADS + GQA_KV_HEADS) * GQA_HEAD_DIM
            w_mla = jnp.concatenate([w_in[:, :o_kr], _spread_rot(w_in[:, o_kr:o_gq])], -1).astype(BF16)
            w_gqa = w_in[:, o_gq:].astype(BF16)
            wuq = mla_w_uq[e].reshape(MLA_Q_RANK, MLA_HEADS, MLA_NOPE + MLA_ROPE)
            wuq = jnp.concatenate(
                [wuq[:, :, :MLA_NOPE],
                 _spread_rot(wuq[:, :, MLA_NOPE:].reshape(-1, MLA_ROPE)).reshape(MLA_Q_RANK, MLA_HEADS, LANES)],
                -1).reshape(MLA_Q_RANK, MLA_HEADS * 2 * LANES).astype(BF16)
            wukv = mla_w_ukv[e].reshape(MLA_KV_RANK, MLA_HEADS, MLA_NOPE + MLA_V)
            wkn = wukv[:, :, :MLA_NOPE].reshape(MLA_KV_RANK, -1).astype(BF16)
            wv = wukv[:, :, MLA_NOPE:].reshape(MLA_KV_RANK, -1).astype(BF16)
            gain = jnp.concatenate([jnp.tile(gqa_g_q[e], GQA_HEADS), jnp.tile(gqa_g_k[e], GQA_KV_HEADS)])[None]
            w_o = mg_w_o[e].astype(BF16)

            def project(xs, m, rp_mla, rp_hd):
                zm = _proj(xs, m, 1, w_mla, 0, w_mla.shape[1], F32)
                qk = _proj(xs, m, 1, w_gqa, 0, n_gqk, BF16, gain=gain, rope=rp_hd)
                v = _proj(xs, m, 1, w_gqa, n_gqk, w_gqa.shape[1] - n_gqk, BF16)
                qa, ka, va = _mla_up(zm, seq, mla_g_cq[e], mla_g_ckv[e], wuq, wkn, wv, rp_mla)
                n = xs.shape[0] // bsz
                return [a.reshape(bsz, n, -1) for a in (qa, ka, va, qk, v)]

            qa, ka, va, qkb, vb = project(x_lat, mod_l, rope_mla, rope_hd)
            qa_c, ka_c, va_c, qkb_c, vb_c = project(x_ctx, mod_c, None, None)
            mla = dict(heads=MLA_HEADS, group=1, dqk=2 * LANES, dv=MLA_V, q_col0=0, k_col0=0, v_col0=0,
                       scale=MLA_SCALE)
            gqa = dict(heads=GQA_HEADS, group=GQA_HEADS // GQA_KV_HEADS, dqk=GQA_HEAD_DIM, dv=GQA_HEAD_DIM,
                       q_col0=0, k_col0=GQA_HEADS * GQA_HEAD_DIM, v_col0=0, scale=GQA_SCALE)
            ya = _attention(qa, ka_c, va_c, ka, va, **mla)
            yb = _attention(qkb, qkb_c, vb_c, qkb, vb, **gqa)
            ys_lat = [ya.reshape(bsz * seq, -1), yb.reshape(bsz * seq, -1)]
            if need_ctx:
                ya_c = _attention(qa_c, ka_c, va_c, None, None, **mla)
                yb_c = _attention(qkb_c, qkb_c, vb_c, None, None, **gqa)
                ys_ctx = [ya_c.reshape(bsz * lc, -1), yb_c.reshape(bsz * lc, -1)]
        else:
            o = i // 2
            lambda_init = 0.8 - 0.6 * math.exp(-0.3 * i)
            w_in = diff_w_in[o].astype(BF16)
            n_qk = 2 * DIFF_HEADS * 2 * DIFF_HEAD_DIM
            w_o = diff_w_o[o].astype(BF16)
            lams = (diff_lq1[o], diff_lk1[o], diff_lq2[o], diff_lk2[o])

            def project(xs, m, rp):
                qk = _proj(xs, m, 1, w_in, 0, n_qk, BF16, rope=rp)
                v = _proj(xs, m, 1, w_in, n_qk, w_in.shape[1] - n_qk, BF16)
                n = xs.shape[0] // bsz
                return qk.reshape(bsz, n, -1), v.reshape(bsz, n, -1)

            qk, v = project(x_lat, mod_l, rope_hd)
            qk_c, v_c = project(x_ctx, mod_c, None)
            y = _diff_attention(qk, qk_c, v_c, qk, v, lams, diff_g_sub[o], lambda_init)
            ys_lat = [y.reshape(bsz * seq, -1)]
            if need_ctx:
                y_c = _diff_attention(qk_c, qk_c, v_c, None, None, lams, diff_g_sub[o], lambda_init)
                ys_ctx = [y_c.reshape(bsz * lc, -1)]

        x_lat = _outproj(ys_lat, w_o, x_lat, mod_l, 1, ln_g[i, 1], ln_b[i, 1], alpha)
        x_lat = ffn(x_lat, mod_l, 2, 1)
        if need_ctx:
            x_ctx = _outproj(ys_ctx, w_o, x_ctx, mod_c, 1, ln_g[i, 1], ln_b[i, 1], alpha)
            x_ctx = ffn(x_ctx, mod_c, 2, 1)
    return x_lat.reshape(bsz, seq, d)
```

```python
import functools
import math

import jax
import jax.numpy as jnp
from jax import lax
from jax.experimental import pallas as pl
from jax.experimental.pallas import tpu as pltpu

F32 = jnp.float32
BF16 = jnp.bfloat16

GRID_W = 64
ROPE_THETA = 10000.0
NORM_EPS = 1e-6
N_SUB = 3
HALF_STEP = 0.5
MLA_HEADS = 8
MLA_Q_RANK = 512
MLA_KV_RANK = 256
MLA_NOPE = 128
MLA_ROPE = 64
MLA_V = 128
MLA_SCALE = (MLA_NOPE + MLA_ROPE) ** -0.5
GQA_HEADS = 8
GQA_KV_HEADS = 2
GQA_HEAD_DIM = 128
GQA_SCALE = GQA_HEAD_DIM ** -0.5
DIFF_HEADS = 8
DIFF_HEAD_DIM = 128
DIFF_SCALE = DIFF_HEAD_DIM ** -0.5

LANES = 128
MOD_ROWS = 32
VMEM_LIMIT_BYTES = 56 * 1024 * 1024

TOKEN_TILE = 512
FF_TILE = 512
PROJ_TILE = 1024
ADA_TILE = 1024
Q_TILE = 2048
ROW_CHUNK = 256
LOG2_E = math.log2(math.e)


def _divisor_tile(n, pref, mult):
    if n <= pref:
        return n
    t = (pref // mult) * mult
    while t >= mult:
        if n % t == 0:
            return t
        t -= mult
    raise ValueError(f"no tile for {n} (pref {pref}, mult {mult})")


def _params(*sem):
    return pltpu.CompilerParams(dimension_semantics=sem, vmem_limit_bytes=VMEM_LIMIT_BYTES)


def _layer_norm(r, g, b):
    mu = jnp.mean(r, axis=-1, keepdims=True)
    d = r - mu
    var = jnp.mean(d * d, axis=-1, keepdims=True)
    return d * lax.rsqrt(var + NORM_EPS) * g + b


def _rms_norm(x, g):
    return x * lax.rsqrt(jnp.mean(x * x, axis=-1, keepdims=True) + NORM_EPS) * g


def _rope_block(blk, cos, sin):
    return blk * cos + pltpu.roll(blk, LANES // 2, 1) * sin


def _dot(a, b):
    return jnp.dot(a, b, preferred_element_type=F32)


def _dot_nt(a, b):
    return lax.dot_general(a, b, (((1,), (1,)), ((), ())), preferred_element_type=F32)


def _ada_kernel(c_ref, w_ref, b_ref, o_ref):
    c = c_ref[...]
    s = (c * jax.nn.sigmoid(c)).astype(BF16)
    o_ref[...] = _dot(s, w_ref[...].astype(BF16)) + b_ref[...]


def _ada(cc, w_ada, b_ada):
    depth, d, n = w_ada.shape
    tn = _divisor_tile(n, ADA_TILE, LANES)
    return pl.pallas_call(
        _ada_kernel,
        grid=(depth, n // tn),
        in_specs=[
            pl.BlockSpec((MOD_ROWS, d), lambda l, j: (0, 0)),
            pl.BlockSpec((None, d, tn), lambda l, j: (l, 0, j)),
            pl.BlockSpec((None, 1, tn), lambda l, j: (l, 0, j)),
        ],
        out_specs=pl.BlockSpec((None, MOD_ROWS, tn), lambda l, j: (l, 0, j)),
        out_shape=jax.ShapeDtypeStruct((depth, MOD_ROWS, n), F32),
        compiler_params=_params("arbitrary", "arbitrary"),
        name="ada_mod",
    )(cc, w_ada, b_ada.reshape(depth, 1, n))


class _Mod:
    def __init__(self, table, layer, batch, seq, is_ctx):
        self.table, self.layer, self.batch, self.seq, self.is_ctx = table, layer, batch, seq, is_ctx

    def spec(self, d, sub, kind, tm):
        base = self.layer * MOD_ROWS
        off = 3 * sub + kind
        if self.is_ctx:
            row = (base + self.batch) * 3 * N_SUB + off
            return pl.BlockSpec((None, 1, d), lambda i, *_: (row, 0, 0))
        tpb = self.seq // tm
        return pl.BlockSpec((None, 1, d), lambda i, *_: ((base + i // tpb) * 3 * N_SUB + off, 0, 0))


def _ffn_kernel(x_ref, shift_ref, scale_ref, gate_ref, w1_ref, w3_ref, w2_ref, g_ref, b_ref, o_ref,
                hb_ref, acc_ref, *, alpha):
    f = pl.program_id(1)

    @pl.when(f == 0)
    def _():
        hb_ref[...] = (x_ref[...] * (1.0 + scale_ref[...]) + shift_ref[...]).astype(BF16)
        acc_ref[...] = jnp.zeros_like(acc_ref)

    hb = hb_ref[...]
    a = _dot(hb, w1_ref[...])
    u = _dot(hb, w3_ref[...])
    act = (a * jax.nn.sigmoid(a) * u).astype(BF16)
    acc_ref[...] += _dot(act, w2_ref[...])

    @pl.when(f == pl.num_programs(1) - 1)
    def _():
        r = alpha * x_ref[...] + gate_ref[...] * (HALF_STEP * acc_ref[...])
        o_ref[...] = _layer_norm(r, g_ref[...], b_ref[...])


def _ffn(x, mod, sub, w1, w3, w2, g, b, alpha):
    t, d = x.shape
    f = w1.shape[1]
    tm = _divisor_tile(t if mod.is_ctx else mod.seq, TOKEN_TILE, 8)
    tf = _divisor_tile(f, FF_TILE, LANES)
    row = pl.BlockSpec((tm, d), lambda i, k: (i, 0))
    vec = pl.BlockSpec((1, d), lambda i, k: (0, 0))
    return pl.pallas_call(
        functools.partial(_ffn_kernel, alpha=alpha),
        grid=(t // tm, f // tf),
        in_specs=[
            row,
            mod.spec(d, sub, 0, tm), mod.spec(d, sub, 1, tm), mod.spec(d, sub, 2, tm),
            pl.BlockSpec((d, tf), lambda i, k: (0, k)),
            pl.BlockSpec((d, tf), lambda i, k: (0, k)),
            pl.BlockSpec((tf, d), lambda i, k: (k, 0)),
            vec, vec,
        ],
        out_specs=row,
        out_shape=jax.ShapeDtypeStruct((t, d), F32),
        scratch_shapes=[pltpu.VMEM((tm, d), BF16), pltpu.VMEM((tm, d), F32)],
        compiler_params=_params("parallel", "arbitrary"),
        name="ffn_half_step",
    )(x, mod.table, mod.table, mod.table, w1, w3, w2, g.reshape(1, d), b.reshape(1, d))


def _proj_kernel(*refs, headnorm, colscale, rope):
    x_ref, shift_ref, scale_ref, w_ref = refs[:4]
    rest = list(refs[4:])
    gain_ref = rest.pop(0) if (headnorm or colscale) else None
    cos_ref, sin_ref = (rest.pop(0), rest.pop(0)) if rope else (None, None)
    o_ref, hb_ref = rest

    @pl.when(pl.program_id(1) == 0)
    def _():
        hb_ref[...] = (x_ref[...] * (1.0 + scale_ref[...]) + shift_ref[...]).astype(BF16)

    z = _dot(hb_ref[...], w_ref[...])
    if not (headnorm or colscale or rope):
        o_ref[...] = z.astype(o_ref.dtype)
        return
    for kb in range(z.shape[1] // LANES):
        cols = slice(kb * LANES, (kb + 1) * LANES)
        blk = z[:, cols]
        if headnorm:
            blk = _rms_norm(blk, gain_ref[:, cols])
        elif colscale:
            blk = blk * gain_ref[:, cols]
        if rope:
            blk = _rope_block(blk, cos_ref[...], sin_ref[...])
        o_ref[:, cols] = blk.astype(o_ref.dtype)


def _proj(x, mod, sub, w, col0, ncols, out_dtype, gain=None, headnorm=False, rope=None):
    t, d = x.shape
    tm = _divisor_tile(t if mod.is_ctx else mod.seq, TOKEN_TILE, 8)
    tn = _divisor_tile(ncols, PROJ_TILE, LANES)
    assert col0 % tn == 0
    joff = col0 // tn
    in_specs = [
        pl.BlockSpec((tm, d), lambda i, j: (i, 0)),
        mod.spec(d, sub, 0, tm), mod.spec(d, sub, 1, tm),
        pl.BlockSpec((d, tn), lambda i, j: (0, j + joff)),
    ]
    args = [x, mod.table, mod.table, w]
    if gain is not None:
        in_specs.append(pl.BlockSpec((1, tn), lambda i, j: (0, j)))
        args.append(gain)
    if rope is not None:
        tpb = mod.seq // tm
        tab = pl.BlockSpec((tm, LANES), lambda i, j: (i % tpb, 0))
        in_specs += [tab, tab]
        args += list(rope)
    return pl.pallas_call(
        functools.partial(_proj_kernel, headnorm=headnorm, colscale=gain is not None and not headnorm,
                          rope=rope is not None),
        grid=(t // tm, ncols // tn),
        in_specs=in_specs,
        out_specs=pl.BlockSpec((tm, tn), lambda i, j: (i, j)),
        out_shape=jax.ShapeDtypeStruct((t, ncols), out_dtype),
        scratch_shapes=[pltpu.VMEM((tm, d), BF16)],
        compiler_params=_params("parallel", "arbitrary"),
        name="mixer_in_proj",
    )(*args)


def _mla_up_kernel(*refs, rope, qscale):
    zm_ref, gcq_ref, gckv_ref, wuq_ref, wkn_ref, wv_ref = refs[:6]
    rest = list(refs[6:])
    cos_ref, sin_ref = (rest.pop(0), rest.pop(0)) if rope else (None, None)
    qa_ref, ka_ref, va_ref = rest
    hd = 2 * LANES

    def rot(blk):
        return _rope_block(blk, cos_ref[...], sin_ref[...]) if rope else blk

    nq = _rms_norm(zm_ref[:, :MLA_Q_RANK], gcq_ref[...]).astype(BF16)
    q = _dot(nq, wuq_ref[...]) * qscale
    for h in range(MLA_HEADS):
        qa_ref[:, h * hd:h * hd + LANES] = q[:, h * hd:h * hd + LANES].astype(BF16)
        qa_ref[:, h * hd + LANES:(h + 1) * hd] = rot(q[:, h * hd + LANES:(h + 1) * hd]).astype(BF16)

    nkv = _rms_norm(zm_ref[:, MLA_Q_RANK:MLA_Q_RANK + MLA_KV_RANK], gckv_ref[...]).astype(BF16)
    kn = _dot(nkv, wkn_ref[...])
    va_ref[...] = _dot(nkv, wv_ref[...]).astype(BF16)
    kr = rot(zm_ref[:, MLA_Q_RANK + MLA_KV_RANK:MLA_Q_RANK + MLA_KV_RANK + LANES]).astype(BF16)
    for h in range(MLA_HEADS):
        ka_ref[:, h * hd:h * hd + LANES] = kn[:, h * LANES:(h + 1) * LANES].astype(BF16)
        ka_ref[:, h * hd + LANES:(h + 1) * hd] = kr


def _mla_up(zm, seq, g_cq, g_ckv, wuq, wkn, wv, rope, qscale):
    t, nz = zm.shape
    tm = _divisor_tile(seq if rope is not None else t, TOKEN_TILE, 8)
    hq = MLA_HEADS * 2 * LANES
    hv = MLA_HEADS * MLA_V
    full = lambda a: pl.BlockSpec(a.shape, lambda i: (0, 0))
    g_cq = g_cq.reshape(1, -1)
    g_ckv = g_ckv.reshape(1, -1)
    in_specs = [pl.BlockSpec((tm, nz), lambda i: (i, 0)), full(g_cq), full(g_ckv), full(wuq), full(wkn), full(wv)]
    args = [zm, g_cq, g_ckv, wuq, wkn, wv]
    if rope is not None:
        tpb = seq // tm
        tab = pl.BlockSpec((tm, LANES), lambda i: (i % tpb, 0))
        in_specs += [tab, tab]
        args += list(rope)
    return pl.pallas_call(
        functools.partial(_mla_up_kernel, rope=rope is not None, qscale=qscale),
        grid=(t // tm,),
        in_specs=in_specs,
        out_specs=[pl.BlockSpec((tm, hq), lambda i: (i, 0)), pl.BlockSpec((tm, hq), lambda i: (i, 0)),
                   pl.BlockSpec((tm, hv), lambda i: (i, 0))],
        out_shape=[jax.ShapeDtypeStruct((t, hq), BF16), jax.ShapeDtypeStruct((t, hq), BF16),
                   jax.ShapeDtypeStruct((t, hv), BF16)],
        compiler_params=_params("parallel"),
        name="mla_up_proj",
    )(*args)


def _raw_scores(q, kc_ref, kl_ref, kcols):
    sc = _dot_nt(q, kc_ref[:, kcols])
    sl = None if kl_ref is None else _dot_nt(q, kl_ref[:, kcols])
    return sc, sl


def _softmax_numerators(scores):
    sc, sl = scores
    m = jnp.max(sc, axis=-1, keepdims=True)
    if sl is None:
        return jnp.exp2(sc - m), None
    m = jnp.maximum(m, jnp.max(sl, axis=-1, keepdims=True))
    return jnp.exp2(sc - m), jnp.exp2(sl - m)


def _row_sum(ec, el):
    den = jnp.sum(ec, axis=-1, keepdims=True)
    return den if el is None else den + jnp.sum(el, axis=-1, keepdims=True)


def _skewed_chunks(n, score_fn, finish_fn):
    nxt = score_fn(0)
    for r in range(n):
        cur, nxt = nxt, (score_fn(r + 1) if r + 1 < n else None)
        finish_fn(r, cur)


def _attn_kernel(*refs, has_lat, row_chunk):
    if has_lat:
        q_ref, kc_ref, vc_ref, kl_ref, vl_ref, o_ref = refs
    else:
        (q_ref, kc_ref, vc_ref, o_ref), kl_ref, vl_ref = refs, None, None
    rows = lambda r: slice(r * row_chunk, (r + 1) * row_chunk)
    dv = vc_ref.shape[1]
    with_ones = lambda v: jnp.concatenate([v, jnp.ones_like(v)], axis=1)
    vc1 = with_ones(vc_ref[...])
    vl1 = with_ones(vl_ref[...]) if has_lat else None

    def finish(r, scores):
        ec, el = _softmax_numerators(scores)
        o = _dot(ec.astype(BF16), vc1)
        if has_lat:
            o = o + _dot(el.astype(BF16), vl1)
        o_ref[rows(r), :] = (o[:, :dv] / o[:, dv:]).astype(o_ref.dtype)

    _skewed_chunks(q_ref.shape[0] // row_chunk,
                   lambda r: _raw_scores(q_ref[rows(r), :], kc_ref, kl_ref, slice(None)), finish)


def _attention(q, kc, vc, kl, vl, *, heads, group, dqk, dv, q_col0, k_col0, v_col0):
    assert dv == LANES
    b, lq, _ = q.shape
    lc = kc.shape[1]
    tq = _divisor_tile(lq, Q_TILE, 8)
    qo, ko, vo = q_col0 // dqk, k_col0 // dqk, v_col0 // dv
    in_specs = [
        pl.BlockSpec((None, tq, dqk), lambda bi, h, i: (bi, i, qo + h)),
        pl.BlockSpec((None, lc, dqk), lambda bi, h, i: (bi, 0, ko + h // group)),
        pl.BlockSpec((None, lc, dv), lambda bi, h, i: (bi, 0, vo + h // group)),
    ]
    args = [q, kc, vc]
    if kl is not None:
        ll = kl.shape[1]
        in_specs += [
            pl.BlockSpec((None, ll, dqk), lambda bi, h, i: (bi, 0, ko + h // group)),
            pl.BlockSpec((None, ll, dv), lambda bi, h, i: (bi, 0, vo + h // group)),
        ]
        args += [kl, vl]
    return pl.pallas_call(
        functools.partial(_attn_kernel, has_lat=kl is not None, row_chunk=min(ROW_CHUNK, tq)),
        grid=(b, heads, lq // tq),
        in_specs=in_specs,
        out_specs=pl.BlockSpec((None, tq, dv), lambda bi, h, i: (bi, i, h)),
        out_shape=jax.ShapeDtypeStruct((b, lq, heads * dv), BF16),
        compiler_params=_params("parallel", "arbitrary", "arbitrary"),
        name="softmax_attention",
    )(*args)


def _diff_attn_kernel(*refs, lambda_init, has_lat, row_chunk):
    if has_lat:
        q_ref, kc_ref, vc_ref, kl_ref, vl_ref, lq1, lk1, lq2, lk2, gs_ref, o_ref = refs
    else:
        (q_ref, kc_ref, vc_ref, lq1, lk1, lq2, lk2, gs_ref, o_ref), kl_ref, vl_ref = refs, None, None
    lam = (jnp.exp(jnp.sum(lq1[...] * lk1[...], axis=-1, keepdims=True))
           - jnp.exp(jnp.sum(lq2[...] * lk2[...], axis=-1, keepdims=True)) + lambda_init)
    hd = DIFF_HEAD_DIM
    rows = lambda r: slice(r * row_chunk, (r + 1) * row_chunk)

    def scores(r):
        return (_raw_scores(q_ref[rows(r), :hd], kc_ref, kl_ref, slice(0, hd)),
                _raw_scores(q_ref[rows(r), hd:], kc_ref, kl_ref, slice(hd, 2 * hd)))

    def finish(r, s01):
        ec0, el0 = _softmax_numerators(s01[0])
        ec1, el1 = _softmax_numerators(s01[1])
        den0 = _row_sum(ec0, el0)
        ratio = lam * den0 / _row_sum(ec1, el1)
        y = _dot((ec0 - ec1 * ratio).astype(BF16), vc_ref[...])
        if has_lat:
            y = y + _dot((el0 - el1 * ratio).astype(BF16), vl_ref[...])
        y = y / den0
        o_ref[rows(r), :] = (_rms_norm(y, gs_ref[...]) * (1.0 - lambda_init)).astype(o_ref.dtype)

    _skewed_chunks(q_ref.shape[0] // row_chunk, scores, finish)


def _diff_attention(q, kc, vc, kl, vl, lams, g_sub, lambda_init):
    b, lq, _ = q.shape
    lc = kc.shape[1]
    tq = _divisor_tile(lq, Q_TILE, 8)
    w = 2 * DIFF_HEAD_DIM
    in_specs = [
        pl.BlockSpec((None, tq, w), lambda bi, h, i: (bi, i, h)),
        pl.BlockSpec((None, lc, w), lambda bi, h, i: (bi, 0, DIFF_HEADS + h)),
        pl.BlockSpec((None, lc, w), lambda bi, h, i: (bi, 0, h)),
    ]
    args = [q, kc, vc]
    if kl is not None:
        ll = kl.shape[1]
        in_specs += [
            pl.BlockSpec((None, ll, w), lambda bi, h, i: (bi, 0, DIFF_HEADS + h)),
            pl.BlockSpec((None, ll, w), lambda bi, h, i: (bi, 0, h)),
        ]
        args += [kl, vl]
    small = lambda n: pl.BlockSpec((1, n), lambda bi, h, i: (0, 0))
    in_specs += [small(DIFF_HEAD_DIM)] * 4 + [small(w)]
    args += [a.reshape(1, -1) for a in lams] + [g_sub.reshape(1, -1)]
    return pl.pallas_call(
        functools.partial(_diff_attn_kernel, lambda_init=lambda_init, has_lat=kl is not None,
                          row_chunk=min(ROW_CHUNK, tq)),
        grid=(b, DIFF_HEADS, lq // tq),
        in_specs=in_specs,
        out_specs=pl.BlockSpec((None, tq, w), lambda bi, h, i: (bi, i, h)),
        out_shape=jax.ShapeDtypeStruct((b, lq, DIFF_HEADS * w), BF16),
        compiler_params=_params("parallel", "arbitrary", "arbitrary"),
        name="diff_attention",
    )(*args)


def _outproj_kernel(*refs, nparts, alpha):
    y_refs = refs[:nparts]
    w_refs = refs[nparts:2 * nparts]
    x_ref, gate_ref, g_ref, b_ref, o_ref = refs[2 * nparts:]
    y = _dot(y_refs[0][...], w_refs[0][...])
    for p in range(1, nparts):
        y = y + _dot(y_refs[p][...], w_refs[p][...])
    r = alpha * x_ref[...] + gate_ref[...] * y
    o_ref[...] = _layer_norm(r, g_ref[...], b_ref[...])


def _outproj(ys, w_o, x, mod, sub, g, b, alpha):
    t, d = x.shape
    tm = _divisor_tile(t if mod.is_ctx else mod.seq, TOKEN_TILE, 8)
    kp = ys[0].shape[1]
    assert all(y.shape[1] == kp for y in ys) and kp * len(ys) == w_o.shape[0]
    vec = pl.BlockSpec((1, d), lambda i: (0, 0))
    in_specs = [pl.BlockSpec((tm, kp), lambda i: (i, 0)) for _ in ys]
    in_specs += [pl.BlockSpec((kp, d), functools.partial(lambda i, p: (p, 0), p=p)) for p in range(len(ys))]
    in_specs += [pl.BlockSpec((tm, d), lambda i: (i, 0)), mod.spec(d, sub, 2, tm), vec, vec]
    return pl.pallas_call(
        functools.partial(_outproj_kernel, nparts=len(ys), alpha=alpha),
        grid=(t // tm,),
        in_specs=in_specs,
        out_specs=pl.BlockSpec((tm, d), lambda i: (i, 0)),
        out_shape=jax.ShapeDtypeStruct((t, d), F32),
        compiler_params=_params("parallel"),
        name="mixer_out_proj",
    )(*ys, *([w_o] * len(ys)), x, mod.table, g.reshape(1, d), b.reshape(1, d))


def _rope_tables(rows, rot_dim):
    r, col = jnp.meshgrid(jnp.arange(rows, dtype=F32), jnp.arange(GRID_W, dtype=F32), indexing="ij")
    n_freq = rot_dim // 4
    inv_freq = ROPE_THETA ** (-jnp.arange(n_freq, dtype=F32) / n_freq)
    ang = jnp.concatenate([r.reshape(-1, 1) * inv_freq, col.reshape(-1, 1) * inv_freq], -1)
    cos, sin = jnp.cos(ang), jnp.sin(ang)
    pad = LANES // 2 - rot_dim // 2
    cos = jnp.pad(cos, ((0, 0), (0, pad)), constant_values=1.0)
    sin = jnp.pad(sin, ((0, 0), (0, pad)))
    return jnp.concatenate([cos, cos], -1), jnp.concatenate([-sin, sin], -1)


def _spread_rot(w):
    half = MLA_ROPE // 2
    z = jnp.zeros((w.shape[0], LANES // 2 - half), w.dtype)
    return jnp.concatenate([w[:, :half], z, w[:, half:], z], -1)


def kernel(x, c, ctx, c_ctx, w_ada, b_ada, ln_g, ln_b, ffn_w1, ffn_w3, ffn_w2, mg_w_in, mla_g_cq, mla_g_ckv,
           mla_w_uq, mla_w_ukv, gqa_g_q, gqa_g_k, mg_w_o, diff_w_in, diff_lq1, diff_lk1, diff_lq2, diff_lk2,
           diff_g_sub, diff_w_o):
    bsz, seq, d = x.shape
    lc = ctx.shape[1]
    depth = w_ada.shape[0]
    assert bsz + 1 <= MOD_ROWS and seq % GRID_W == 0
    alpha = (2 * depth) ** 0.25
    rows = seq // GRID_W
    rope_mla = _rope_tables(rows, MLA_ROPE)
    rope_hd = _rope_tables(rows, GQA_HEAD_DIM)

    cc = jnp.concatenate([c, c_ctx[None], jnp.zeros((MOD_ROWS - bsz - 1, d), F32)], 0)
    mod_table = _ada(cc, w_ada, b_ada).reshape(depth * MOD_ROWS * 3 * N_SUB, 1, d)

    x_lat = x.reshape(bsz * seq, d)
    x_ctx = ctx.reshape(bsz * lc, d)
    w1b, w3b, w2b = ffn_w1.astype(BF16), ffn_w3.astype(BF16), ffn_w2.astype(BF16)

    for i in range(depth):
        need_ctx = i < depth - 1
        mod_l = _Mod(mod_table, i, bsz, seq, False)
        mod_c = _Mod(mod_table, i, bsz, seq, True)
        ffn = lambda xs, m, sub, k: _ffn(xs, m, sub, w1b[i, k], w3b[i, k], w2b[i, k], ln_g[i, sub], ln_b[i, sub], alpha)

        x_lat = ffn(x_lat, mod_l, 0, 0)
        x_ctx = ffn(x_ctx, mod_c, 0, 0)

        if i % 2 == 0:
            e = i // 2
            w_in = mg_w_in[e]
            o_kr = MLA_Q_RANK + MLA_KV_RANK
            o_gq = o_kr + MLA_ROPE
            n_gqk = (GQA_HEADS + GQA_KV_HEADS) * GQA_HEAD_DIM
            w_mla = jnp.concatenate([w_in[:, :o_kr], _spread_rot(w_in[:, o_kr:o_gq])], -1).astype(BF16)
            w_gqa = w_in[:, o_gq:].astype(BF16)
            wuq = mla_w_uq[e].reshape(MLA_Q_RANK, MLA_HEADS, MLA_NOPE + MLA_ROPE)
            wuq = jnp.concatenate(
                [wuq[:, :, :MLA_NOPE],
                 _spread_rot(wuq[:, :, MLA_NOPE:].reshape(-1, MLA_ROPE)).reshape(MLA_Q_RANK, MLA_HEADS, LANES)],
                -1).reshape(MLA_Q_RANK, MLA_HEADS * 2 * LANES).astype(BF16)
            wukv = mla_w_ukv[e].reshape(MLA_KV_RANK, MLA_HEADS, MLA_NOPE + MLA_V)
            wkn = wukv[:, :, :MLA_NOPE].reshape(MLA_KV_RANK, -1).astype(BF16)
            wv = wukv[:, :, MLA_NOPE:].reshape(MLA_KV_RANK, -1).astype(BF16)
            gain = jnp.concatenate([jnp.tile(gqa_g_q[e] * (GQA_SCALE * LOG2_E), GQA_HEADS),
                                    jnp.tile(gqa_g_k[e], GQA_KV_HEADS)])[None]
            w_o = mg_w_o[e].astype(BF16)

            def project(xs, m, rp_mla, rp_hd):
                zm = _proj(xs, m, 1, w_mla, 0, w_mla.shape[1], F32)
                qk = _proj(xs, m, 1, w_gqa, 0, n_gqk, BF16, gain=gain, headnorm=True, rope=rp_hd)
                v = _proj(xs, m, 1, w_gqa, n_gqk, w_gqa.shape[1] - n_gqk, BF16)
                qa, ka, va = _mla_up(zm, seq, mla_g_cq[e], mla_g_ckv[e], wuq, wkn, wv, rp_mla,
                                     MLA_SCALE * LOG2_E)
                n = xs.shape[0] // bsz
                return [a.reshape(bsz, n, -1) for a in (qa, ka, va, qk, v)]

            qa, ka, va, qkb, vb = project(x_lat, mod_l, rope_mla, rope_hd)
            qa_c, ka_c, va_c, qkb_c, vb_c = project(x_ctx, mod_c, None, None)
            mla = dict(heads=MLA_HEADS, group=1, dqk=2 * LANES, dv=MLA_V, q_col0=0, k_col0=0, v_col0=0)
            gqa = dict(heads=GQA_HEADS, group=GQA_HEADS // GQA_KV_HEADS, dqk=GQA_HEAD_DIM, dv=GQA_HEAD_DIM,
                       q_col0=0, k_col0=GQA_HEADS * GQA_HEAD_DIM, v_col0=0)
            ya = _attention(qa, ka_c, va_c, ka, va, **mla)
            yb = _attention(qkb, qkb_c, vb_c, qkb, vb, **gqa)
            ys_lat = [ya.reshape(bsz * seq, -1), yb.reshape(bsz * seq, -1)]
            if need_ctx:
                ya_c = _attention(qa_c, ka_c, va_c, None, None, **mla)
                yb_c = _attention(qkb_c, qkb_c, vb_c, None, None, **gqa)
                ys_ctx = [ya_c.reshape(bsz * lc, -1), yb_c.reshape(bsz * lc, -1)]
        else:
            o = i // 2
            lambda_init = 0.8 - 0.6 * math.exp(-0.3 * i)
            w_in = diff_w_in[o].astype(BF16)
            n_qk = 2 * DIFF_HEADS * 2 * DIFF_HEAD_DIM
            w_o = diff_w_o[o].astype(BF16)
            lams = (diff_lq1[o], diff_lk1[o], diff_lq2[o], diff_lk2[o])
            qk_scale = jnp.concatenate([jnp.full((n_qk // 2,), DIFF_SCALE * LOG2_E, F32),
                                        jnp.ones((n_qk // 2,), F32)])[None]

            def project(xs, m, rp):
                qk = _proj(xs, m, 1, w_in, 0, n_qk, BF16, gain=qk_scale, rope=rp)
                v = _proj(xs, m, 1, w_in, n_qk, w_in.shape[1] - n_qk, BF16)
                n = xs.shape[0] // bsz
                return qk.reshape(bsz, n, -1), v.reshape(bsz, n, -1)

            qk, v = project(x_lat, mod_l, rope_hd)
            qk_c, v_c = project(x_ctx, mod_c, None)
            y = _diff_attention(qk, qk_c, v_c, qk, v, lams, diff_g_sub[o], lambda_init)
            ys_lat = [y.reshape(bsz * seq, -1)]
            if need_ctx:
                y_c = _diff_attention(qk_c, qk_c, v_c, None, None, lams, diff_g_sub[o], lambda_init)
                ys_ctx = [y_c.reshape(bsz * lc, -1)]

        x_lat = _outproj(ys_lat, w_o, x_lat, mod_l, 1, ln_g[i, 1], ln_b[i, 1], alpha)
        x_lat = ffn(x_lat, mod_l, 2, 1)
        if need_ctx:
            x_ctx = _outproj(ys_ctx, w_o, x_ctx, mod_c, 1, ln_g[i, 1], ln_b[i, 1], alpha)
            x_ctx = ffn(x_ctx, mod_c, 2, 1)
    return x_lat.reshape(bsz, seq, d)
```

```python
import functools
import math

import jax
import jax.numpy as jnp
from jax import lax
from jax.experimental import pallas as pl
from jax.experimental.pallas import tpu as pltpu

F32 = jnp.float32
BF16 = jnp.bfloat16

GRID_W = 64
ROPE_THETA = 10000.0
NORM_EPS = 1e-6
N_SUB = 3
HALF_STEP = 0.5
MLA_HEADS = 8
MLA_Q_RANK = 512
MLA_KV_RANK = 256
MLA_NOPE = 128
MLA_ROPE = 64
MLA_V = 128
MLA_SCALE = (MLA_NOPE + MLA_ROPE) ** -0.5
GQA_HEADS = 8
GQA_KV_HEADS = 2
GQA_HEAD_DIM = 128
GQA_SCALE = GQA_HEAD_DIM ** -0.5
DIFF_HEADS = 8
DIFF_HEAD_DIM = 128
DIFF_SCALE = DIFF_HEAD_DIM ** -0.5

LANES = 128
MOD_ROWS = 32
VMEM_LIMIT_BYTES = 56 * 1024 * 1024

TOKEN_TILE = 512
FF_TILE = 512
PROJ_TILE = 1024
ADA_TILE = 1024
Q_TILE = 2048
ROW_CHUNK = 256
COL_CHUNK = 256
LOG2_E = math.log2(math.e)


def _divisor_tile(n, pref, mult):
    if n <= pref:
        return n
    t = (pref // mult) * mult
    while t >= mult:
        if n % t == 0:
            return t
        t -= mult
    raise ValueError(f"no tile for {n} (pref {pref}, mult {mult})")


def _params(*sem):
    return pltpu.CompilerParams(dimension_semantics=sem, vmem_limit_bytes=VMEM_LIMIT_BYTES)


def _layer_norm(r, g, b):
    mu = jnp.mean(r, axis=-1, keepdims=True)
    d = r - mu
    var = jnp.mean(d * d, axis=-1, keepdims=True)
    return d * lax.rsqrt(var + NORM_EPS) * g + b


def _rms_norm(x, g):
    return x * lax.rsqrt(jnp.mean(x * x, axis=-1, keepdims=True) + NORM_EPS) * g


def _rope_block(blk, cos, sin):
    return blk * cos + pltpu.roll(blk, LANES // 2, 1) * sin


def _skewed_chunks(n, matmul_fn, finish_fn):
    nxt = matmul_fn(0)
    for c in range(n):
        cur, nxt = nxt, (matmul_fn(c + 1) if c + 1 < n else None)
        finish_fn(c, cur)


def _dot(a, b):
    return jnp.dot(a, b, preferred_element_type=F32)


def _dot_nt(a, b):
    return lax.dot_general(a, b, (((1,), (1,)), ((), ())), preferred_element_type=F32)


def _ada_kernel(c_ref, w_ref, b_ref, o_ref):
    c = c_ref[...]
    s = (c * jax.nn.sigmoid(c)).astype(BF16)
    o_ref[...] = _dot(s, w_ref[...].astype(BF16)) + b_ref[...]


def _ada(cc, w_ada, b_ada):
    depth, d, n = w_ada.shape
    tn = _divisor_tile(n, ADA_TILE, LANES)
    return pl.pallas_call(
        _ada_kernel,
        grid=(depth, n // tn),
        in_specs=[
            pl.BlockSpec((MOD_ROWS, d), lambda l, j: (0, 0)),
            pl.BlockSpec((None, d, tn), lambda l, j: (l, 0, j)),
            pl.BlockSpec((None, 1, tn), lambda l, j: (l, 0, j)),
        ],
        out_specs=pl.BlockSpec((None, MOD_ROWS, tn), lambda l, j: (l, 0, j)),
        out_shape=jax.ShapeDtypeStruct((depth, MOD_ROWS, n), F32),
        compiler_params=_params("arbitrary", "arbitrary"),
        name="ada_mod",
    )(cc, w_ada, b_ada.reshape(depth, 1, n))


class _Mod:
    def __init__(self, table, layer, batch, seq, is_ctx):
        self.table, self.layer, self.batch, self.seq, self.is_ctx = table, layer, batch, seq, is_ctx

    def spec(self, d, sub, kind, tm):
        base = self.layer * MOD_ROWS
        off = 3 * sub + kind
        if self.is_ctx:
            row = (base + self.batch) * 3 * N_SUB + off
            return pl.BlockSpec((None, 1, d), lambda i, *_: (row, 0, 0))
        tpb = self.seq // tm
        return pl.BlockSpec((None, 1, d), lambda i, *_: ((base + i // tpb) * 3 * N_SUB + off, 0, 0))


def _ffn_kernel(x_ref, shift_ref, scale_ref, gate_ref, w1_ref, w3_ref, w2_ref, g_ref, b_ref, o_ref,
                hb_ref, acc_ref, *, alpha, nf):
    f = pl.program_id(1)
    tm = x_ref.shape[0]

    def modulate(rows):
        hb_ref[rows, :] = (x_ref[rows, :] * (1.0 + scale_ref[...]) + shift_ref[...]).astype(BF16)

    def swiglu_chunk(rows):
        hb = hb_ref[rows, :]
        a = _dot(hb, w1_ref[...])
        u = _dot(hb, w3_ref[...])
        act = (a * jax.nn.sigmoid(a) * u).astype(BF16)
        return _dot(act, w2_ref[...])

    def post_norm(rows, y):
        r = alpha * x_ref[rows, :] + gate_ref[...] * (HALF_STEP * y)
        o_ref[rows, :] = _layer_norm(r, g_ref[...], b_ref[...])

    def step(first, last):
        rc = min(ROW_CHUNK, tm) if (first or last) else tm
        rows = lambda r: slice(r * rc, (r + 1) * rc)
        n = tm // rc
        if first:
            modulate(rows(0))

        def matmuls(r):
            if first and r + 1 < n:
                modulate(rows(r + 1))
            y = swiglu_chunk(rows(r))
            return y if first else acc_ref[rows(r), :] + y

        def finish(r, y):
            if last:
                post_norm(rows(r), y)
            else:
                acc_ref[rows(r), :] = y

        _skewed_chunks(n, matmuls, finish)

    if nf == 1:
        step(True, True)
        return
    pl.when(f == 0)(lambda: step(True, False))
    pl.when(f == nf - 1)(lambda: step(False, True))
    if nf > 2:
        pl.when(jnp.logical_and(f > 0, f < nf - 1))(lambda: step(False, False))


def _ffn(x, mod, sub, w1, w3, w2, g, b, alpha):
    t, d = x.shape
    f = w1.shape[1]
    tm = _divisor_tile(t if mod.is_ctx else mod.seq, TOKEN_TILE, 8)
    tf = _divisor_tile(f, FF_TILE, LANES)
    row = pl.BlockSpec((tm, d), lambda i, k: (i, 0))
    vec = pl.BlockSpec((1, d), lambda i, k: (0, 0))
    return pl.pallas_call(
        functools.partial(_ffn_kernel, alpha=alpha, nf=f // tf),
        grid=(t // tm, f // tf),
        in_specs=[
            row,
            mod.spec(d, sub, 0, tm), mod.spec(d, sub, 1, tm), mod.spec(d, sub, 2, tm),
            pl.BlockSpec((d, tf), lambda i, k: (0, k)),
            pl.BlockSpec((d, tf), lambda i, k: (0, k)),
            pl.BlockSpec((tf, d), lambda i, k: (k, 0)),
            vec, vec,
        ],
        out_specs=row,
        out_shape=jax.ShapeDtypeStruct((t, d), F32),
        scratch_shapes=[pltpu.VMEM((tm, d), BF16), pltpu.VMEM((tm, d), F32)],
        compiler_params=_params("parallel", "arbitrary"),
        name="ffn_half_step",
    )(x, mod.table, mod.table, mod.table, w1, w3, w2, g.reshape(1, d), b.reshape(1, d))


def _proj_kernel(*refs, headnorm, colscale, rope):
    x_ref, shift_ref, scale_ref, w_ref = refs[:4]
    rest = list(refs[4:])
    gain_ref = rest.pop(0) if (headnorm or colscale) else None
    cos_ref, sin_ref = (rest.pop(0), rest.pop(0)) if rope else (None, None)
    o_ref, hb_ref = rest

    @pl.when(pl.program_id(1) == 0)
    def _():
        hb_ref[...] = (x_ref[...] * (1.0 + scale_ref[...]) + shift_ref[...]).astype(BF16)

    if not (headnorm or colscale or rope):
        o_ref[...] = _dot(hb_ref[...], w_ref[...]).astype(o_ref.dtype)
        return
    tn = o_ref.shape[1]
    cw = COL_CHUNK if tn % COL_CHUNK == 0 else LANES

    def finish(c, z):
        for kb in range(cw // LANES):
            cols = slice(c * cw + kb * LANES, c * cw + (kb + 1) * LANES)
            blk = z[:, kb * LANES:(kb + 1) * LANES]
            if headnorm:
                blk = _rms_norm(blk, gain_ref[:, cols])
            elif colscale:
                blk = blk * gain_ref[:, cols]
            if rope:
                blk = _rope_block(blk, cos_ref[...], sin_ref[...])
            o_ref[:, cols] = blk.astype(o_ref.dtype)

    _skewed_chunks(tn // cw, lambda c: _dot(hb_ref[...], w_ref[:, c * cw:(c + 1) * cw]), finish)


def _proj(x, mod, sub, w, col0, ncols, out_dtype, gain=None, headnorm=False, rope=None):
    t, d = x.shape
    tm = _divisor_tile(t if mod.is_ctx else mod.seq, TOKEN_TILE, 8)
    tn = _divisor_tile(ncols, PROJ_TILE, LANES)
    assert col0 % tn == 0
    joff = col0 // tn
    in_specs = [
        pl.BlockSpec((tm, d), lambda i, j: (i, 0)),
        mod.spec(d, sub, 0, tm), mod.spec(d, sub, 1, tm),
        pl.BlockSpec((d, tn), lambda i, j: (0, j + joff)),
    ]
    args = [x, mod.table, mod.table, w]
    if gain is not None:
        in_specs.append(pl.BlockSpec((1, tn), lambda i, j: (0, j)))
        args.append(gain)
    if rope is not None:
        tpb = mod.seq // tm
        tab = pl.BlockSpec((tm, LANES), lambda i, j: (i % tpb, 0))
        in_specs += [tab, tab]
        args += list(rope)
    return pl.pallas_call(
        functools.partial(_proj_kernel, headnorm=headnorm, colscale=gain is not None and not headnorm,
                          rope=rope is not None),
        grid=(t // tm, ncols // tn),
        in_specs=in_specs,
        out_specs=pl.BlockSpec((tm, tn), lambda i, j: (i, j)),
        out_shape=jax.ShapeDtypeStruct((t, ncols), out_dtype),
        scratch_shapes=[pltpu.VMEM((tm, d), BF16)],
        compiler_params=_params("parallel", "arbitrary"),
        name="mixer_in_proj",
    )(*args)


def _mla_up_kernel(*refs, rope, qscale):
    zm_ref, gcq_ref, gckv_ref, wuq_ref, wkn_ref, wv_ref = refs[:6]
    rest = list(refs[6:])
    cos_ref, sin_ref = (rest.pop(0), rest.pop(0)) if rope else (None, None)
    qa_ref, ka_ref, va_ref = rest
    hd = 2 * LANES

    def rot(blk):
        return _rope_block(blk, cos_ref[...], sin_ref[...]) if rope else blk

    nq = _rms_norm(zm_ref[:, :MLA_Q_RANK], gcq_ref[...]).astype(BF16)
    q = _dot(nq, wuq_ref[...]) * qscale
    for h in range(MLA_HEADS):
        qa_ref[:, h * hd:h * hd + LANES] = q[:, h * hd:h * hd + LANES].astype(BF16)
        qa_ref[:, h * hd + LANES:(h + 1) * hd] = rot(q[:, h * hd + LANES:(h + 1) * hd]).astype(BF16)

    nkv = _rms_norm(zm_ref[:, MLA_Q_RANK:MLA_Q_RANK + MLA_KV_RANK], gckv_ref[...]).astype(BF16)
    kn = _dot(nkv, wkn_ref[...])
    va_ref[...] = _dot(nkv, wv_ref[...]).astype(BF16)
    kr = rot(zm_ref[:, MLA_Q_RANK + MLA_KV_RANK:MLA_Q_RANK + MLA_KV_RANK + LANES]).astype(BF16)
    for h in range(MLA_HEADS):
        ka_ref[:, h * hd:h * hd + LANES] = kn[:, h * LANES:(h + 1) * LANES].astype(BF16)
        ka_ref[:, h * hd + LANES:(h + 1) * hd] = kr


def _mla_up(zm, seq, g_cq, g_ckv, wuq, wkn, wv, rope, qscale):
    t, nz = zm.shape
    tm = _divisor_tile(seq if rope is not None else t, TOKEN_TILE, 8)
    hq = MLA_HEADS * 2 * LANES
    hv = MLA_HEADS * MLA_V
    full = lambda a: pl.BlockSpec(a.shape, lambda i: (0, 0))
    g_cq = g_cq.reshape(1, -1)
    g_ckv = g_ckv.reshape(1, -1)
    in_specs = [pl.BlockSpec((tm, nz), lambda i: (i, 0)), full(g_cq), full(g_ckv), full(wuq), full(wkn), full(wv)]
    args = [zm, g_cq, g_ckv, wuq, wkn, wv]
    if rope is not None:
        tpb = seq // tm
        tab = pl.BlockSpec((tm, LANES), lambda i: (i % tpb, 0))
        in_specs += [tab, tab]
        args += list(rope)
    return pl.pallas_call(
        functools.partial(_mla_up_kernel, rope=rope is not None, qscale=qscale),
        grid=(t // tm,),
        in_specs=in_specs,
        out_specs=[pl.BlockSpec((tm, hq), lambda i: (i, 0)), pl.BlockSpec((tm, hq), lambda i: (i, 0)),
                   pl.BlockSpec((tm, hv), lambda i: (i, 0))],
        out_shape=[jax.ShapeDtypeStruct((t, hq), BF16), jax.ShapeDtypeStruct((t, hq), BF16),
                   jax.ShapeDtypeStruct((t, hv), BF16)],
        compiler_params=_params("parallel"),
        name="mla_up_proj",
    )(*args)


def _raw_scores(q, kc_ref, kl_ref, kcols):
    sc = _dot_nt(q, kc_ref[:, kcols])
    sl = None if kl_ref is None else _dot_nt(q, kl_ref[:, kcols])
    return sc, sl


def _softmax_numerators(scores):
    sc, sl = scores
    m = jnp.max(sc, axis=-1, keepdims=True)
    if sl is None:
        return jnp.exp2(sc - m), None
    m = jnp.maximum(m, jnp.max(sl, axis=-1, keepdims=True))
    return jnp.exp2(sc - m), jnp.exp2(sl - m)


def _row_sum(ec, el):
    den = jnp.sum(ec, axis=-1, keepdims=True)
    return den if el is None else den + jnp.sum(el, axis=-1, keepdims=True)


def _attn_kernel(*refs, has_lat, row_chunk):
    if has_lat:
        q_ref, kc_ref, vc_ref, kl_ref, vl_ref, o_ref = refs
    else:
        (q_ref, kc_ref, vc_ref, o_ref), kl_ref, vl_ref = refs, None, None
    rows = lambda r: slice(r * row_chunk, (r + 1) * row_chunk)
    dv = vc_ref.shape[1]
    with_ones = lambda v: jnp.concatenate([v, jnp.ones_like(v)], axis=1)
    vc1 = with_ones(vc_ref[...])
    vl1 = with_ones(vl_ref[...]) if has_lat else None

    def finish(r, scores):
        ec, el = _softmax_numerators(scores)
        o = _dot(ec.astype(BF16), vc1)
        if has_lat:
            o = o + _dot(el.astype(BF16), vl1)
        o_ref[rows(r), :] = (o[:, :dv] / o[:, dv:]).astype(o_ref.dtype)

    _skewed_chunks(q_ref.shape[0] // row_chunk,
                   lambda r: _raw_scores(q_ref[rows(r), :], kc_ref, kl_ref, slice(None)), finish)


def _attention(q, kc, vc, kl, vl, *, heads, group, dqk, dv, q_col0, k_col0, v_col0):
    assert dv == LANES
    b, lq, _ = q.shape
    lc = kc.shape[1]
    tq = _divisor_tile(lq, Q_TILE, 8)
    qo, ko, vo = q_col0 // dqk, k_col0 // dqk, v_col0 // dv
    in_specs = [
        pl.BlockSpec((None, tq, dqk), lambda bi, h, i: (bi, i, qo + h)),
        pl.BlockSpec((None, lc, dqk), lambda bi, h, i: (bi, 0, ko + h // group)),
        pl.BlockSpec((None, lc, dv), lambda bi, h, i: (bi, 0, vo + h // group)),
    ]
    args = [q, kc, vc]
    if kl is not None:
        ll = kl.shape[1]
        in_specs += [
            pl.BlockSpec((None, ll, dqk), lambda bi, h, i: (bi, 0, ko + h // group)),
            pl.BlockSpec((None, ll, dv), lambda bi, h, i: (bi, 0, vo + h // group)),
        ]
        args += [kl, vl]
    return pl.pallas_call(
        functools.partial(_attn_kernel, has_lat=kl is not None, row_chunk=min(ROW_CHUNK, tq)),
        grid=(b, heads, lq // tq),
        in_specs=in_specs,
        out_specs=pl.BlockSpec((None, tq, dv), lambda bi, h, i: (bi, i, h)),
        out_shape=jax.ShapeDtypeStruct((b, lq, heads * dv), BF16),
        compiler_params=_params("parallel", "arbitrary", "arbitrary"),
        name="softmax_attention",
    )(*args)


def _diff_attn_kernel(*refs, lambda_init, has_lat, row_chunk):
    if has_lat:
        q_ref, kc_ref, vc_ref, kl_ref, vl_ref, lq1, lk1, lq2, lk2, gs_ref, o_ref = refs
    else:
        (q_ref, kc_ref, vc_ref, lq1, lk1, lq2, lk2, gs_ref, o_ref), kl_ref, vl_ref = refs, None, None
    lam = (jnp.exp(jnp.sum(lq1[...] * lk1[...], axis=-1, keepdims=True))
           - jnp.exp(jnp.sum(lq2[...] * lk2[...], axis=-1, keepdims=True)) + lambda_init)
    hd = DIFF_HEAD_DIM
    rows = lambda r: slice(r * row_chunk, (r + 1) * row_chunk)

    def scores(r):
        return (_raw_scores(q_ref[rows(r), :hd], kc_ref, kl_ref, slice(0, hd)),
                _raw_scores(q_ref[rows(r), hd:], kc_ref, kl_ref, slice(hd, 2 * hd)))

    def finish(r, s01):
        ec0, el0 = _softmax_numerators(s01[0])
        ec1, el1 = _softmax_numerators(s01[1])
        den0 = _row_sum(ec0, el0)
        ratio = lam * den0 / _row_sum(ec1, el1)
        y = _dot((ec0 - ec1 * ratio).astype(BF16), vc_ref[...])
        if has_lat:
            y = y + _dot((el0 - el1 * ratio).astype(BF16), vl_ref[...])
        y = y / den0
        o_ref[rows(r), :] = (_rms_norm(y, gs_ref[...]) * (1.0 - lambda_init)).astype(o_ref.dtype)

    _skewed_chunks(q_ref.shape[0] // row_chunk, scores, finish)


def _diff_attention(q, kc, vc, kl, vl, lams, g_sub, lambda_init):
    b, lq, _ = q.shape
    lc = kc.shape[1]
    tq = _divisor_tile(lq, Q_TILE, 8)
    w = 2 * DIFF_HEAD_DIM
    in_specs = [
        pl.BlockSpec((None, tq, w), lambda bi, h, i: (bi, i, h)),
        pl.BlockSpec((None, lc, w), lambda bi, h, i: (bi, 0, DIFF_HEADS + h)),
        pl.BlockSpec((None, lc, w), lambda bi, h, i: (bi, 0, h)),
    ]
    args = [q, kc, vc]
    if kl is not None:
        ll = kl.shape[1]
        in_specs += [
            pl.BlockSpec((None, ll, w), lambda bi, h, i: (bi, 0, DIFF_HEADS + h)),
            pl.BlockSpec((None, ll, w), lambda bi, h, i: (bi, 0, h)),
        ]
        args += [kl, vl]
    small = lambda n: pl.BlockSpec((1, n), lambda bi, h, i: (0, 0))
    in_specs += [small(DIFF_HEAD_DIM)] * 4 + [small(w)]
    args += [a.reshape(1, -1) for a in lams] + [g_sub.reshape(1, -1)]
    return pl.pallas_call(
        functools.partial(_diff_attn_kernel, lambda_init=lambda_init, has_lat=kl is not None,
                          row_chunk=min(ROW_CHUNK, tq)),
        grid=(b, DIFF_HEADS, lq // tq),
        in_specs=in_specs,
        out_specs=pl.BlockSpec((None, tq, w), lambda bi, h, i: (bi, i, h)),
        out_shape=jax.ShapeDtypeStruct((b, lq, DIFF_HEADS * w), BF16),
        compiler_params=_params("parallel", "arbitrary", "arbitrary"),
        name="diff_attention",
    )(*args)


def _outproj_kernel(*refs, nparts, alpha):
    y_refs = refs[:nparts]
    w_refs = refs[nparts:2 * nparts]
    x_ref, gate_ref, g_ref, b_ref, o_ref = refs[2 * nparts:]
    row_chunk = min(ROW_CHUNK, o_ref.shape[0])
    rows = lambda r: slice(r * row_chunk, (r + 1) * row_chunk)

    def project(r):
        y = _dot(y_refs[0][rows(r), :], w_refs[0][...])
        for p in range(1, nparts):
            y = y + _dot(y_refs[p][rows(r), :], w_refs[p][...])
        return y

    def finish(r, y):
        res = alpha * x_ref[rows(r), :] + gate_ref[...] * y
        o_ref[rows(r), :] = _layer_norm(res, g_ref[...], b_ref[...])

    _skewed_chunks(o_ref.shape[0] // row_chunk, project, finish)


def _outproj(ys, w_o, x, mod, sub, g, b, alpha):
    t, d = x.shape
    tm = _divisor_tile(t if mod.is_ctx else mod.seq, TOKEN_TILE, 8)
    kp = ys[0].shape[1]
    assert all(y.shape[1] == kp for y in ys) and kp * len(ys) == w_o.shape[0]
    vec = pl.BlockSpec((1, d), lambda i: (0, 0))
    in_specs = [pl.BlockSpec((tm, kp), lambda i: (i, 0)) for _ in ys]
    in_specs += [pl.BlockSpec((kp, d), functools.partial(lambda i, p: (p, 0), p=p)) for p in range(len(ys))]
    in_specs += [pl.BlockSpec((tm, d), lambda i: (i, 0)), mod.spec(d, sub, 2, tm), vec, vec]
    return pl.pallas_call(
        functools.partial(_outproj_kernel, nparts=len(ys), alpha=alpha),
        grid=(t // tm,),
        in_specs=in_specs,
        out_specs=pl.BlockSpec((tm, d), lambda i: (i, 0)),
        out_shape=jax.ShapeDtypeStruct((t, d), F32),
        compiler_params=_params("parallel"),
        name="mixer_out_proj",
    )(*ys, *([w_o] * len(ys)), x, mod.table, g.reshape(1, d), b.reshape(1, d))


def _rope_tables(rows, rot_dim):
    r, col = jnp.meshgrid(jnp.arange(rows, dtype=F32), jnp.arange(GRID_W, dtype=F32), indexing="ij")
    n_freq = rot_dim // 4
    inv_freq = ROPE_THETA ** (-jnp.arange(n_freq, dtype=F32) / n_freq)
    ang = jnp.concatenate([r.reshape(-1, 1) * inv_freq, col.reshape(-1, 1) * inv_freq], -1)
    cos, sin = jnp.cos(ang), jnp.sin(ang)
    pad = LANES // 2 - rot_dim // 2
    cos = jnp.pad(cos, ((0, 0), (0, pad)), constant_values=1.0)
    sin = jnp.pad(sin, ((0, 0), (0, pad)))
    return jnp.concatenate([cos, cos], -1), jnp.concatenate([-sin, sin], -1)


def _spread_rot(w):
    half = MLA_ROPE // 2
    z = jnp.zeros((w.shape[0], LANES // 2 - half), w.dtype)
    return jnp.concatenate([w[:, :half], z, w[:, half:], z], -1)


def kernel(x, c, ctx, c_ctx, w_ada, b_ada, ln_g, ln_b, ffn_w1, ffn_w3, ffn_w2, mg_w_in, mla_g_cq, mla_g_ckv,
           mla_w_uq, mla_w_ukv, gqa_g_q, gqa_g_k, mg_w_o, diff_w_in, diff_lq1, diff_lk1, diff_lq2, diff_lk2,
           diff_g_sub, diff_w_o):
    bsz, seq, d = x.shape
    lc = ctx.shape[1]
    depth = w_ada.shape[0]
    assert bsz + 1 <= MOD_ROWS and seq % GRID_W == 0
    alpha = (2 * depth) ** 0.25
    rows = seq // GRID_W
    rope_mla = _rope_tables(rows, MLA_ROPE)
    rope_hd = _rope_tables(rows, GQA_HEAD_DIM)

    cc = jnp.concatenate([c, c_ctx[None], jnp.zeros((MOD_ROWS - bsz - 1, d), F32)], 0)
    mod_table = _ada(cc, w_ada, b_ada).reshape(depth * MOD_ROWS * 3 * N_SUB, 1, d)

    x_lat = x.reshape(bsz * seq, d)
    x_ctx = ctx.reshape(bsz * lc, d)
    w1b, w3b, w2b = ffn_w1.astype(BF16), ffn_w3.astype(BF16), ffn_w2.astype(BF16)

    for i in range(depth):
        need_ctx = i < depth - 1
        mod_l = _Mod(mod_table, i, bsz, seq, False)
        mod_c = _Mod(mod_table, i, bsz, seq, True)
        ffn = lambda xs, m, sub, k: _ffn(xs, m, sub, w1b[i, k], w3b[i, k], w2b[i, k], ln_g[i, sub], ln_b[i, sub], alpha)

        x_lat = ffn(x_lat, mod_l, 0, 0)
        x_ctx = ffn(x_ctx, mod_c, 0, 0)

        if i % 2 == 0:
            e = i // 2
            w_in = mg_w_in[e]
            o_kr = MLA_Q_RANK + MLA_KV_RANK
            o_gq = o_kr + MLA_ROPE
            n_gqk = (GQA_HEADS + GQA_KV_HEADS) * GQA_HEAD_DIM
            w_mla = jnp.concatenate([w_in[:, :o_kr], _spread_rot(w_in[:, o_kr:o_gq])], -1).astype(BF16)
            w_gqa = w_in[:, o_gq:].astype(BF16)
            wuq = mla_w_uq[e].reshape(MLA_Q_RANK, MLA_HEADS, MLA_NOPE + MLA_ROPE)
            wuq = jnp.concatenate(
                [wuq[:, :, :MLA_NOPE],
                 _spread_rot(wuq[:, :, MLA_NOPE:].reshape(-1, MLA_ROPE)).reshape(MLA_Q_RANK, MLA_HEADS, LANES)],
                -1).reshape(MLA_Q_RANK, MLA_HEADS * 2 * LANES).astype(BF16)
            wukv = mla_w_ukv[e].reshape(MLA_KV_RANK, MLA_HEADS, MLA_NOPE + MLA_V)
            wkn = wukv[:, :, :MLA_NOPE].reshape(MLA_KV_RANK, -1).astype(BF16)
            wv = wukv[:, :, MLA_NOPE:].reshape(MLA_KV_RANK, -1).astype(BF16)
            gain = jnp.concatenate([jnp.tile(gqa_g_q[e] * (GQA_SCALE * LOG2_E), GQA_HEADS),
                                    jnp.tile(gqa_g_k[e], GQA_KV_HEADS)])[None]
            w_o = mg_w_o[e].astype(BF16)

            def project(xs, m, rp_mla, rp_hd):
                zm = _proj(xs, m, 1, w_mla, 0, w_mla.shape[1], F32)
                qk = _proj(xs, m, 1, w_gqa, 0, n_gqk, BF16, gain=gain, headnorm=True, rope=rp_hd)
                v = _proj(xs, m, 1, w_gqa, n_gqk, w_gqa.shape[1] - n_gqk, BF16)
                qa, ka, va = _mla_up(zm, seq, mla_g_cq[e], mla_g_ckv[e], wuq, wkn, wv, rp_mla,
                                     MLA_SCALE * LOG2_E)
                n = xs.shape[0] // bsz
                return [a.reshape(bsz, n, -1) for a in (qa, ka, va, qk, v)]

            qa, ka, va, qkb, vb = project(x_lat, mod_l, rope_mla, rope_hd)
            qa_c, ka_c, va_c, qkb_c, vb_c = project(x_ctx, mod_c, None, None)
            mla = dict(heads=MLA_HEADS, group=1, dqk=2 * LANES, dv=MLA_V, q_col0=0, k_col0=0, v_col0=0)
            gqa = dict(heads=GQA_HEADS, group=GQA_HEADS // GQA_KV_HEADS, dqk=GQA_HEAD_DIM, dv=GQA_HEAD_DIM,
                       q_col0=0, k_col0=GQA_HEADS * GQA_HEAD_DIM, v_col0=0)
            ya = _attention(qa, ka_c, va_c, ka, va, **mla)
            yb = _attention(qkb, qkb_c, vb_c, qkb, vb, **gqa)
            ys_lat = [ya.reshape(bsz * seq, -1), yb.reshape(bsz * seq, -1)]
            if need_ctx:
                ya_c = _attention(qa_c, ka_c, va_c, None, None, **mla)
                yb_c = _attention(qkb_c, qkb_c, vb_c, None, None, **gqa)
                ys_ctx = [ya_c.reshape(bsz * lc, -1), yb_c.reshape(bsz * lc, -1)]
        else:
            o = i // 2
            lambda_init = 0.8 - 0.6 * math.exp(-0.3 * i)
            w_in = diff_w_in[o].astype(BF16)
            n_qk = 2 * DIFF_HEADS * 2 * DIFF_HEAD_DIM
            w_o = diff_w_o[o].astype(BF16)
            lams = (diff_lq1[o], diff_lk1[o], diff_lq2[o], diff_lk2[o])
            qk_scale = jnp.concatenate([jnp.full((n_qk // 2,), DIFF_SCALE * LOG2_E, F32),
                                        jnp.ones((n_qk // 2,), F32)])[None]

            def project(xs, m, rp):
                qk = _proj(xs, m, 1, w_in, 0, n_qk, BF16, gain=qk_scale, rope=rp)
                v = _proj(xs, m, 1, w_in, n_qk, w_in.shape[1] - n_qk, BF16)
                n = xs.shape[0] // bsz
                return qk.reshape(bsz, n, -1), v.reshape(bsz, n, -1)

            qk, v = project(x_lat, mod_l, rope_hd)
            qk_c, v_c = project(x_ctx, mod_c, None)
            y = _diff_attention(qk, qk_c, v_c, qk, v, lams, diff_g_sub[o], lambda_init)
            ys_lat = [y.reshape(bsz * seq, -1)]
            if need_ctx:
                y_c = _diff_attention(qk_c, qk_c, v_c, None, None, lams, diff_g_sub[o], lambda_init)
                ys_ctx = [y_c.reshape(bsz * lc, -1)]

        x_lat = _outproj(ys_lat, w_o, x_lat, mod_l, 1, ln_g[i, 1], ln_b[i, 1], alpha)
        x_lat = ffn(x_lat, mod_l, 2, 1)
        if need_ctx:
            x_ctx = _outproj(ys_ctx, w_o, x_ctx, mod_c, 1, ln_g[i, 1], ln_b[i, 1], alpha)
            x_ctx = ffn(x_ctx, mod_c, 2, 1)
    return x_lat.reshape(bsz, seq, d)
```

```python
import functools
import math

import jax
import jax.numpy as jnp
from jax import lax
from jax.experimental import pallas as pl
from jax.experimental.pallas import tpu as pltpu

F32 = jnp.float32
BF16 = jnp.bfloat16

GRID_W = 64
ROPE_THETA = 10000.0
NORM_EPS = 1e-6
N_SUB = 3
HALF_STEP = 0.5
MLA_HEADS = 8
MLA_Q_RANK = 512
MLA_KV_RANK = 256
MLA_NOPE = 128
MLA_ROPE = 64
MLA_V = 128
MLA_SCALE = (MLA_NOPE + MLA_ROPE) ** -0.5
GQA_HEADS = 8
GQA_KV_HEADS = 2
GQA_HEAD_DIM = 128
GQA_SCALE = GQA_HEAD_DIM ** -0.5
DIFF_HEADS = 8
DIFF_HEAD_DIM = 128
DIFF_SCALE = DIFF_HEAD_DIM ** -0.5

LANES = 128
MOD_ROWS = 32
VMEM_LIMIT_BYTES = 56 * 1024 * 1024

TOKEN_TILE = 512
FF_TILE = 512
PROJ_TILE = 1024
ADA_TILE = 1024
Q_TILE = 2048
ROW_CHUNK = 256
COL_CHUNK = 256
LOG2_E = math.log2(math.e)


def _divisor_tile(n, pref, mult):
    if n <= pref:
        return n
    t = (pref // mult) * mult
    while t >= mult:
        if n % t == 0:
            return t
        t -= mult
    raise ValueError(f"no tile for {n} (pref {pref}, mult {mult})")


def _params(*sem):
    return pltpu.CompilerParams(dimension_semantics=sem, vmem_limit_bytes=VMEM_LIMIT_BYTES)


def _layer_norm(r, g, b):
    mu = jnp.mean(r, axis=-1, keepdims=True)
    d = r - mu
    var = jnp.mean(d * d, axis=-1, keepdims=True)
    return d * lax.rsqrt(var + NORM_EPS) * g + b


def _rms_norm(x, g):
    return x * lax.rsqrt(jnp.mean(x * x, axis=-1, keepdims=True) + NORM_EPS) * g


def _rope_block(blk, cos, sin):
    return blk * cos + pltpu.roll(blk, LANES // 2, 1) * sin


def _skewed_chunks(n, matmul_fn, finish_fn):
    nxt = matmul_fn(0)
    for c in range(n):
        cur, nxt = nxt, (matmul_fn(c + 1) if c + 1 < n else None)
        finish_fn(c, cur)


def _dot(a, b):
    return jnp.dot(a, b, preferred_element_type=F32)


def _dot_nt(a, b):
    return lax.dot_general(a, b, (((1,), (1,)), ((), ())), preferred_element_type=F32)


def _ada_kernel(c_ref, w_ref, b_ref, o_ref):
    c = c_ref[...]
    s = (c * jax.nn.sigmoid(c)).astype(BF16)
    o_ref[...] = _dot(s, w_ref[...].astype(BF16)) + b_ref[...]


def _ada(cc, w_ada, b_ada):
    depth, d, n = w_ada.shape
    tn = _divisor_tile(n, ADA_TILE, LANES)
    return pl.pallas_call(
        _ada_kernel,
        grid=(depth, n // tn),
        in_specs=[
            pl.BlockSpec((MOD_ROWS, d), lambda l, j: (0, 0)),
            pl.BlockSpec((None, d, tn), lambda l, j: (l, 0, j)),
            pl.BlockSpec((None, 1, tn), lambda l, j: (l, 0, j)),
        ],
        out_specs=pl.BlockSpec((None, MOD_ROWS, tn), lambda l, j: (l, 0, j)),
        out_shape=jax.ShapeDtypeStruct((depth, MOD_ROWS, n), F32),
        compiler_params=_params("arbitrary", "arbitrary"),
        name="ada_mod",
    )(cc, w_ada, b_ada.reshape(depth, 1, n))


class _Mod:
    def __init__(self, table, layer, batch, seq, is_ctx):
        self.table, self.layer, self.batch, self.seq, self.is_ctx = table, layer, batch, seq, is_ctx

    def spec(self, d, sub, kind, tm):
        base = self.layer * MOD_ROWS
        off = 3 * sub + kind
        if self.is_ctx:
            row = (base + self.batch) * 3 * N_SUB + off
            return pl.BlockSpec((None, 1, d), lambda i, *_: (row, 0, 0))
        tpb = self.seq // tm
        return pl.BlockSpec((None, 1, d), lambda i, *_: ((base + i // tpb) * 3 * N_SUB + off, 0, 0))


def _ffn_kernel(x_ref, shift_ref, scale_ref, gate_ref, w1_ref, w3_ref, w2_ref, g_ref, b_ref, o_ref,
                hb_ref, acc_ref, *, alpha, nf):
    f = pl.program_id(1)
    tm = x_ref.shape[0]

    def modulate(rows):
        hb_ref[rows, :] = (x_ref[rows, :] * (1.0 + scale_ref[...]) + shift_ref[...]).astype(BF16)

    def swiglu_chunk(rows):
        hb = hb_ref[rows, :]
        a = _dot(hb, w1_ref[...])
        u = _dot(hb, w3_ref[...])
        act = (a * jax.nn.sigmoid(a) * u).astype(BF16)
        return _dot(act, w2_ref[...])

    def post_norm(rows, y):
        r = alpha * x_ref[rows, :] + gate_ref[...] * (HALF_STEP * y)
        o_ref[rows, :] = _layer_norm(r, g_ref[...], b_ref[...])

    def step(first, last):
        rc = min(ROW_CHUNK, tm) if (first or last) else tm
        rows = lambda r: slice(r * rc, (r + 1) * rc)
        n = tm // rc
        if first:
            modulate(rows(0))

        def matmuls(r):
            if first and r + 1 < n:
                modulate(rows(r + 1))
            y = swiglu_chunk(rows(r))
            return y if first else acc_ref[rows(r), :] + y

        def finish(r, y):
            if last:
                post_norm(rows(r), y)
            else:
                acc_ref[rows(r), :] = y

        _skewed_chunks(n, matmuls, finish)

    if nf == 1:
        step(True, True)
        return
    pl.when(f == 0)(lambda: step(True, False))
    pl.when(f == nf - 1)(lambda: step(False, True))
    if nf > 2:
        pl.when(jnp.logical_and(f > 0, f < nf - 1))(lambda: step(False, False))


def _ffn(x, mod, sub, w1, w3, w2, g, b, alpha):
    t, d = x.shape
    nf, _, tf = w1.shape
    tm = _divisor_tile(t if mod.is_ctx else mod.seq, TOKEN_TILE, 8)
    row = pl.BlockSpec((tm, d), lambda i, k: (i, 0))
    vec = pl.BlockSpec((1, d), lambda i, k: (0, 0))
    return pl.pallas_call(
        functools.partial(_ffn_kernel, alpha=alpha, nf=nf),
        grid=(t // tm, nf),
        in_specs=[
            row,
            mod.spec(d, sub, 0, tm), mod.spec(d, sub, 1, tm), mod.spec(d, sub, 2, tm),
            pl.BlockSpec((None, d, tf), lambda i, k: (k, 0, 0)),
            pl.BlockSpec((None, d, tf), lambda i, k: (k, 0, 0)),
            pl.BlockSpec((tf, d), lambda i, k: (k, 0)),
            vec, vec,
        ],
        out_specs=row,
        out_shape=jax.ShapeDtypeStruct((t, d), F32),
        scratch_shapes=[pltpu.VMEM((tm, d), BF16), pltpu.VMEM((tm, d), F32)],
        compiler_params=_params("parallel", "arbitrary"),
        name="ffn_half_step",
    )(x, mod.table, mod.table, mod.table, w1, w3, w2, g.reshape(1, d), b.reshape(1, d))


def _proj_kernel(*refs, headnorm, colscale, rope):
    x_ref, shift_ref, scale_ref, w_ref = refs[:4]
    rest = list(refs[4:])
    gain_ref = rest.pop(0) if (headnorm or colscale) else None
    cos_ref, sin_ref = (rest.pop(0), rest.pop(0)) if rope else (None, None)
    o_ref, hb_ref = rest

    @pl.when(pl.program_id(1) == 0)
    def _():
        hb_ref[...] = (x_ref[...] * (1.0 + scale_ref[...]) + shift_ref[...]).astype(BF16)

    if not (headnorm or colscale or rope):
        o_ref[...] = _dot(hb_ref[...], w_ref[...]).astype(o_ref.dtype)
        return
    tn = o_ref.shape[1]
    cw = COL_CHUNK if tn % COL_CHUNK == 0 else LANES

    def finish(c, z):
        for kb in range(cw // LANES):
            cols = slice(c * cw + kb * LANES, c * cw + (kb + 1) * LANES)
            blk = z[:, kb * LANES:(kb + 1) * LANES]
            if headnorm:
                blk = _rms_norm(blk, gain_ref[:, cols])
            elif colscale:
                blk = blk * gain_ref[:, cols]
            if rope:
                blk = _rope_block(blk, cos_ref[...], sin_ref[...])
            o_ref[:, cols] = blk.astype(o_ref.dtype)

    _skewed_chunks(tn // cw, lambda c: _dot(hb_ref[...], w_ref[:, c * cw:(c + 1) * cw]), finish)


def _col_tiles(w, pref=PROJ_TILE):
    k, n = w.shape
    tn = _divisor_tile(n, pref, LANES)
    return w.reshape(k, n // tn, tn).transpose(1, 0, 2)


def _proj(x, mod, sub, w, out_dtype, gain=None, headnorm=False, rope=None):
    t, d = x.shape
    nt, _, tn = w.shape
    ncols = nt * tn
    tm = _divisor_tile(t if mod.is_ctx else mod.seq, TOKEN_TILE, 8)
    in_specs = [
        pl.BlockSpec((tm, d), lambda i, j: (i, 0)),
        mod.spec(d, sub, 0, tm), mod.spec(d, sub, 1, tm),
        pl.BlockSpec((None, d, tn), lambda i, j: (j, 0, 0)),
    ]
    args = [x, mod.table, mod.table, w]
    if gain is not None:
        in_specs.append(pl.BlockSpec((1, tn), lambda i, j: (0, j)))
        args.append(gain)
    if rope is not None:
        tpb = mod.seq // tm
        tab = pl.BlockSpec((tm, LANES), lambda i, j: (i % tpb, 0))
        in_specs += [tab, tab]
        args += list(rope)
    return pl.pallas_call(
        functools.partial(_proj_kernel, headnorm=headnorm, colscale=gain is not None and not headnorm,
                          rope=rope is not None),
        grid=(t // tm, ncols // tn),
        in_specs=in_specs,
        out_specs=pl.BlockSpec((tm, tn), lambda i, j: (i, j)),
        out_shape=jax.ShapeDtypeStruct((t, ncols), out_dtype),
        scratch_shapes=[pltpu.VMEM((tm, d), BF16)],
        compiler_params=_params("parallel", "arbitrary"),
        name="mixer_in_proj",
    )(*args)


def _mla_up_kernel(*refs, rope, qscale):
    zm_ref, gcq_ref, gckv_ref, wuq_ref, wkn_ref, wv_ref = refs[:6]
    rest = list(refs[6:])
    cos_ref, sin_ref = (rest.pop(0), rest.pop(0)) if rope else (None, None)
    qa_ref, ka_ref, va_ref = rest
    hd = 2 * LANES

    def rot(blk):
        return _rope_block(blk, cos_ref[...], sin_ref[...]) if rope else blk

    nq = _rms_norm(zm_ref[:, :MLA_Q_RANK], gcq_ref[...]).astype(BF16)
    q = _dot(nq, wuq_ref[...]) * qscale
    for h in range(MLA_HEADS):
        qa_ref[:, h * hd:h * hd + LANES] = q[:, h * hd:h * hd + LANES].astype(BF16)
        qa_ref[:, h * hd + LANES:(h + 1) * hd] = rot(q[:, h * hd + LANES:(h + 1) * hd]).astype(BF16)

    nkv = _rms_norm(zm_ref[:, MLA_Q_RANK:MLA_Q_RANK + MLA_KV_RANK], gckv_ref[...]).astype(BF16)
    kn = _dot(nkv, wkn_ref[...])
    va_ref[...] = _dot(nkv, wv_ref[...]).astype(BF16)
    kr = rot(zm_ref[:, MLA_Q_RANK + MLA_KV_RANK:MLA_Q_RANK + MLA_KV_RANK + LANES]).astype(BF16)
    for h in range(MLA_HEADS):
        ka_ref[:, h * hd:h * hd + LANES] = kn[:, h * LANES:(h + 1) * LANES].astype(BF16)
        ka_ref[:, h * hd + LANES:(h + 1) * hd] = kr


def _mla_up(zm, seq, g_cq, g_ckv, wuq, wkn, wv, rope, qscale):
    t, nz = zm.shape
    tm = _divisor_tile(seq if rope is not None else t, TOKEN_TILE, 8)
    hq = MLA_HEADS * 2 * LANES
    hv = MLA_HEADS * MLA_V
    full = lambda a: pl.BlockSpec(a.shape, lambda i: (0, 0))
    g_cq = g_cq.reshape(1, -1)
    g_ckv = g_ckv.reshape(1, -1)
    in_specs = [pl.BlockSpec((tm, nz), lambda i: (i, 0)), full(g_cq), full(g_ckv), full(wuq), full(wkn), full(wv)]
    args = [zm, g_cq, g_ckv, wuq, wkn, wv]
    if rope is not None:
        tpb = seq // tm
        tab = pl.BlockSpec((tm, LANES), lambda i: (i % tpb, 0))
        in_specs += [tab, tab]
        args += list(rope)
    return pl.pallas_call(
        functools.partial(_mla_up_kernel, rope=rope is not None, qscale=qscale),
        grid=(t // tm,),
        in_specs=in_specs,
        out_specs=[pl.BlockSpec((tm, hq), lambda i: (i, 0)), pl.BlockSpec((tm, hq), lambda i: (i, 0)),
                   pl.BlockSpec((tm, hv), lambda i: (i, 0))],
        out_shape=[jax.ShapeDtypeStruct((t, hq), BF16), jax.ShapeDtypeStruct((t, hq), BF16),
                   jax.ShapeDtypeStruct((t, hv), BF16)],
        compiler_params=_params("parallel"),
        name="mla_up_proj",
    )(*args)


def _raw_scores(q, kc_ref, kl_ref, kcols):
    sc = _dot_nt(q, kc_ref[:, kcols])
    sl = None if kl_ref is None else _dot_nt(q, kl_ref[:, kcols])
    return sc, sl


def _softmax_numerators(scores):
    sc, sl = scores
    m = jnp.max(sc, axis=-1, keepdims=True)
    if sl is None:
        return jnp.exp2(sc - m), None
    m = jnp.maximum(m, jnp.max(sl, axis=-1, keepdims=True))
    return jnp.exp2(sc - m), jnp.exp2(sl - m)


def _row_sum(ec, el):
    den = jnp.sum(ec, axis=-1, keepdims=True)
    return den if el is None else den + jnp.sum(el, axis=-1, keepdims=True)


def _attn_kernel(*refs, has_lat, row_chunk):
    if has_lat:
        q_ref, kc_ref, vc_ref, kl_ref, vl_ref, o_ref = refs
    else:
        (q_ref, kc_ref, vc_ref, o_ref), kl_ref, vl_ref = refs, None, None
    rows = lambda r: slice(r * row_chunk, (r + 1) * row_chunk)
    dv = vc_ref.shape[1]
    with_ones = lambda v: jnp.concatenate([v, jnp.ones_like(v)], axis=1)
    vc1 = with_ones(vc_ref[...])
    vl1 = with_ones(vl_ref[...]) if has_lat else None

    def finish(r, scores):
        ec, el = _softmax_numerators(scores)
        o = _dot(ec.astype(BF16), vc1)
        if has_lat:
            o = o + _dot(el.astype(BF16), vl1)
        o_ref[rows(r), :] = (o[:, :dv] / o[:, dv:]).astype(o_ref.dtype)

    _skewed_chunks(q_ref.shape[0] // row_chunk,
                   lambda r: _raw_scores(q_ref[rows(r), :], kc_ref, kl_ref, slice(None)), finish)


def _attention(q, kc, vc, kl, vl, *, heads, group, dqk, dv, q_col0, k_col0, v_col0):
    assert dv == LANES
    b, lq, _ = q.shape
    lc = kc.shape[1]
    tq = _divisor_tile(lq, Q_TILE, 8)
    qo, ko, vo = q_col0 // dqk, k_col0 // dqk, v_col0 // dv
    in_specs = [
        pl.BlockSpec((None, tq, dqk), lambda bi, h, i: (bi, i, qo + h)),
        pl.BlockSpec((None, lc, dqk), lambda bi, h, i: (bi, 0, ko + h // group)),
        pl.BlockSpec((None, lc, dv), lambda bi, h, i: (bi, 0, vo + h // group)),
    ]
    args = [q, kc, vc]
    if kl is not None:
        ll = kl.shape[1]
        in_specs += [
            pl.BlockSpec((None, ll, dqk), lambda bi, h, i: (bi, 0, ko + h // group)),
            pl.BlockSpec((None, ll, dv), lambda bi, h, i: (bi, 0, vo + h // group)),
        ]
        args += [kl, vl]
    return pl.pallas_call(
        functools.partial(_attn_kernel, has_lat=kl is not None, row_chunk=min(ROW_CHUNK, tq)),
        grid=(b, heads, lq // tq),
        in_specs=in_specs,
        out_specs=pl.BlockSpec((None, tq, dv), lambda bi, h, i: (bi, i, h)),
        out_shape=jax.ShapeDtypeStruct((b, lq, heads * dv), BF16),
        compiler_params=_params("parallel", "arbitrary", "arbitrary"),
        name="softmax_attention",
    )(*args)


def _diff_attn_kernel(*refs, lambda_init, has_lat, row_chunk):
    if has_lat:
        q_ref, kc_ref, vc_ref, kl_ref, vl_ref, lq1, lk1, lq2, lk2, gs_ref, o_ref = refs
    else:
        (q_ref, kc_ref, vc_ref, lq1, lk1, lq2, lk2, gs_ref, o_ref), kl_ref, vl_ref = refs, None, None
    lam = (jnp.exp(jnp.sum(lq1[...] * lk1[...], axis=-1, keepdims=True))
           - jnp.exp(jnp.sum(lq2[...] * lk2[...], axis=-1, keepdims=True)) + lambda_init)
    hd = DIFF_HEAD_DIM
    rows = lambda r: slice(r * row_chunk, (r + 1) * row_chunk)

    def scores(r):
        return (_raw_scores(q_ref[rows(r), :hd], kc_ref, kl_ref, slice(0, hd)),
                _raw_scores(q_ref[rows(r), hd:], kc_ref, kl_ref, slice(hd, 2 * hd)))

    def finish(r, s01):
        ec0, el0 = _softmax_numerators(s01[0])
        ec1, el1 = _softmax_numerators(s01[1])
        den0 = _row_sum(ec0, el0)
        ratio = lam * den0 / _row_sum(ec1, el1)
        y = _dot((ec0 - ec1 * ratio).astype(BF16), vc_ref[...])
        if has_lat:
            y = y + _dot((el0 - el1 * ratio).astype(BF16), vl_ref[...])
        y = y / den0
        o_ref[rows(r), :] = (_rms_norm(y, gs_ref[...]) * (1.0 - lambda_init)).astype(o_ref.dtype)

    _skewed_chunks(q_ref.shape[0] // row_chunk, scores, finish)


def _diff_attention(q, kc, vc, kl, vl, lams, g_sub, lambda_init):
    b, lq, _ = q.shape
    lc = kc.shape[1]
    tq = _divisor_tile(lq, Q_TILE, 8)
    w = 2 * DIFF_HEAD_DIM
    in_specs = [
        pl.BlockSpec((None, tq, w), lambda bi, h, i: (bi, i, h)),
        pl.BlockSpec((None, lc, w), lambda bi, h, i: (bi, 0, DIFF_HEADS + h)),
        pl.BlockSpec((None, lc, w), lambda bi, h, i: (bi, 0, h)),
    ]
    args = [q, kc, vc]
    if kl is not None:
        ll = kl.shape[1]
        in_specs += [
            pl.BlockSpec((None, ll, w), lambda bi, h, i: (bi, 0, DIFF_HEADS + h)),
            pl.BlockSpec((None, ll, w), lambda bi, h, i: (bi, 0, h)),
        ]
        args += [kl, vl]
    small = lambda n: pl.BlockSpec((1, n), lambda bi, h, i: (0, 0))
    in_specs += [small(DIFF_HEAD_DIM)] * 4 + [small(w)]
    args += [a.reshape(1, -1) for a in lams] + [g_sub.reshape(1, -1)]
    return pl.pallas_call(
        functools.partial(_diff_attn_kernel, lambda_init=lambda_init, has_lat=kl is not None,
                          row_chunk=min(ROW_CHUNK, tq)),
        grid=(b, DIFF_HEADS, lq // tq),
        in_specs=in_specs,
        out_specs=pl.BlockSpec((None, tq, w), lambda bi, h, i: (bi, i, h)),
        out_shape=jax.ShapeDtypeStruct((b, lq, DIFF_HEADS * w), BF16),
        compiler_params=_params("parallel", "arbitrary", "arbitrary"),
        name="diff_attention",
    )(*args)


def _outproj_kernel(*refs, nparts, alpha):
    y_refs = refs[:nparts]
    w_refs = refs[nparts:2 * nparts]
    x_ref, gate_ref, g_ref, b_ref, o_ref = refs[2 * nparts:]
    row_chunk = min(ROW_CHUNK, o_ref.shape[0])
    rows = lambda r: slice(r * row_chunk, (r + 1) * row_chunk)

    def project(r):
        y = _dot(y_refs[0][rows(r), :], w_refs[0][...])
        for p in range(1, nparts):
            y = y + _dot(y_refs[p][rows(r), :], w_refs[p][...])
        return y

    def finish(r, y):
        res = alpha * x_ref[rows(r), :] + gate_ref[...] * y
        o_ref[rows(r), :] = _layer_norm(res, g_ref[...], b_ref[...])

    _skewed_chunks(o_ref.shape[0] // row_chunk, project, finish)


def _outproj(ys, w_o, x, mod, sub, g, b, alpha):
    t, d = x.shape
    tm = _divisor_tile(t if mod.is_ctx else mod.seq, TOKEN_TILE, 8)
    kp = ys[0].shape[1]
    assert all(y.shape[1] == kp for y in ys) and kp * len(ys) == w_o.shape[0]
    vec = pl.BlockSpec((1, d), lambda i: (0, 0))
    in_specs = [pl.BlockSpec((tm, kp), lambda i: (i, 0)) for _ in ys]
    in_specs += [pl.BlockSpec((kp, d), functools.partial(lambda i, p: (p, 0), p=p)) for p in range(len(ys))]
    in_specs += [pl.BlockSpec((tm, d), lambda i: (i, 0)), mod.spec(d, sub, 2, tm), vec, vec]
    return pl.pallas_call(
        functools.partial(_outproj_kernel, nparts=len(ys), alpha=alpha),
        grid=(t // tm,),
        in_specs=in_specs,
        out_specs=pl.BlockSpec((tm, d), lambda i: (i, 0)),
        out_shape=jax.ShapeDtypeStruct((t, d), F32),
        compiler_params=_params("parallel"),
        name="mixer_out_proj",
    )(*ys, *([w_o] * len(ys)), x, mod.table, g.reshape(1, d), b.reshape(1, d))


def _rope_tables(rows, rot_dim):
    r, col = jnp.meshgrid(jnp.arange(rows, dtype=F32), jnp.arange(GRID_W, dtype=F32), indexing="ij")
    n_freq = rot_dim // 4
    inv_freq = ROPE_THETA ** (-jnp.arange(n_freq, dtype=F32) / n_freq)
    ang = jnp.concatenate([r.reshape(-1, 1) * inv_freq, col.reshape(-1, 1) * inv_freq], -1)
    cos, sin = jnp.cos(ang), jnp.sin(ang)
    pad = LANES // 2 - rot_dim // 2
    cos = jnp.pad(cos, ((0, 0), (0, pad)), constant_values=1.0)
    sin = jnp.pad(sin, ((0, 0), (0, pad)))
    return jnp.concatenate([cos, cos], -1), jnp.concatenate([-sin, sin], -1)


def _spread_rot(w):
    half = MLA_ROPE // 2
    z = jnp.zeros((w.shape[0], LANES // 2 - half), w.dtype)
    return jnp.concatenate([w[:, :half], z, w[:, half:], z], -1)


def kernel(x, c, ctx, c_ctx, w_ada, b_ada, ln_g, ln_b, ffn_w1, ffn_w3, ffn_w2, mg_w_in, mla_g_cq, mla_g_ckv,
           mla_w_uq, mla_w_ukv, gqa_g_q, gqa_g_k, mg_w_o, diff_w_in, diff_lq1, diff_lk1, diff_lq2, diff_lk2,
           diff_g_sub, diff_w_o):
    bsz, seq, d = x.shape
    lc = ctx.shape[1]
    depth = w_ada.shape[0]
    assert bsz + 1 <= MOD_ROWS and seq % GRID_W == 0
    alpha = (2 * depth) ** 0.25
    rows = seq // GRID_W
    rope_mla = _rope_tables(rows, MLA_ROPE)
    rope_hd = _rope_tables(rows, GQA_HEAD_DIM)

    cc = jnp.concatenate([c, c_ctx[None], jnp.zeros((MOD_ROWS - bsz - 1, d), F32)], 0)
    mod_table = _ada(cc, w_ada, b_ada).reshape(depth * MOD_ROWS * 3 * N_SUB, 1, d)

    x_lat = x.reshape(bsz * seq, d)
    x_ctx = ctx.reshape(bsz * lc, d)
    d_ff = ffn_w1.shape[-1]
    tf = _divisor_tile(d_ff, FF_TILE, LANES)
    ff_tiles = lambda w: w.astype(BF16).reshape(depth, 2, d, d_ff // tf, tf).transpose(0, 1, 3, 2, 4)
    w1b, w3b, w2b = ff_tiles(ffn_w1), ff_tiles(ffn_w3), ffn_w2.astype(BF16)

    for i in range(depth):
        need_ctx = i < depth - 1
        mod_l = _Mod(mod_table, i, bsz, seq, False)
        mod_c = _Mod(mod_table, i, bsz, seq, True)
        ffn = lambda xs, m, sub, k: _ffn(xs, m, sub, w1b[i, k], w3b[i, k], w2b[i, k], ln_g[i, sub], ln_b[i, sub], alpha)

        x_lat = ffn(x_lat, mod_l, 0, 0)
        x_ctx = ffn(x_ctx, mod_c, 0, 0)

        if i % 2 == 0:
            e = i // 2
            w_in = mg_w_in[e]
            o_kr = MLA_Q_RANK + MLA_KV_RANK
            o_gq = o_kr + MLA_ROPE
            n_gqk = (GQA_HEADS + GQA_KV_HEADS) * GQA_HEAD_DIM
            w_mla = _col_tiles(
                jnp.concatenate([w_in[:, :o_kr], _spread_rot(w_in[:, o_kr:o_gq])], -1).astype(BF16))
            w_gqk = _col_tiles(w_in[:, o_gq:o_gq + n_gqk].astype(BF16))
            w_gv = _col_tiles(w_in[:, o_gq + n_gqk:].astype(BF16))
            wuq = mla_w_uq[e].reshape(MLA_Q_RANK, MLA_HEADS, MLA_NOPE + MLA_ROPE)
            wuq = jnp.concatenate(
                [wuq[:, :, :MLA_NOPE],
                 _spread_rot(wuq[:, :, MLA_NOPE:].reshape(-1, MLA_ROPE)).reshape(MLA_Q_RANK, MLA_HEADS, LANES)],
                -1).reshape(MLA_Q_RANK, MLA_HEADS * 2 * LANES).astype(BF16)
            wukv = mla_w_ukv[e].reshape(MLA_KV_RANK, MLA_HEADS, MLA_NOPE + MLA_V)
            wkn = wukv[:, :, :MLA_NOPE].reshape(MLA_KV_RANK, -1).astype(BF16)
            wv = wukv[:, :, MLA_NOPE:].reshape(MLA_KV_RANK, -1).astype(BF16)
            gain = jnp.concatenate([jnp.tile(gqa_g_q[e] * (GQA_SCALE * LOG2_E), GQA_HEADS),
                                    jnp.tile(gqa_g_k[e], GQA_KV_HEADS)])[None]
            w_o = mg_w_o[e].astype(BF16)

            def project(xs, m, rp_mla, rp_hd):
                zm = _proj(xs, m, 1, w_mla, F32)
                qk = _proj(xs, m, 1, w_gqk, BF16, gain=gain, headnorm=True, rope=rp_hd)
                v = _proj(xs, m, 1, w_gv, BF16)
                qa, ka, va = _mla_up(zm, seq, mla_g_cq[e], mla_g_ckv[e], wuq, wkn, wv, rp_mla,
                                     MLA_SCALE * LOG2_E)
                n = xs.shape[0] // bsz
                return [a.reshape(bsz, n, -1) for a in (qa, ka, va, qk, v)]

            qa, ka, va, qkb, vb = project(x_lat, mod_l, rope_mla, rope_hd)
            qa_c, ka_c, va_c, qkb_c, vb_c = project(x_ctx, mod_c, None, None)
            mla = dict(heads=MLA_HEADS, group=1, dqk=2 * LANES, dv=MLA_V, q_col0=0, k_col0=0, v_col0=0)
            gqa = dict(heads=GQA_HEADS, group=GQA_HEADS // GQA_KV_HEADS, dqk=GQA_HEAD_DIM, dv=GQA_HEAD_DIM,
                       q_col0=0, k_col0=GQA_HEADS * GQA_HEAD_DIM, v_col0=0)
            ya = _attention(qa, ka_c, va_c, ka, va, **mla)
            yb = _attention(qkb, qkb_c, vb_c, qkb, vb, **gqa)
            ys_lat = [ya.reshape(bsz * seq, -1), yb.reshape(bsz * seq, -1)]
            if need_ctx:
                ya_c = _attention(qa_c, ka_c, va_c, None, None, **mla)
                yb_c = _attention(qkb_c, qkb_c, vb_c, None, None, **gqa)
                ys_ctx = [ya_c.reshape(bsz * lc, -1), yb_c.reshape(bsz * lc, -1)]
        else:
            o = i // 2
            lambda_init = 0.8 - 0.6 * math.exp(-0.3 * i)
            n_qk = 2 * DIFF_HEADS * 2 * DIFF_HEAD_DIM
            w_qk = _col_tiles(diff_w_in[o][:, :n_qk].astype(BF16))
            w_v = _col_tiles(diff_w_in[o][:, n_qk:].astype(BF16))
            w_o = diff_w_o[o].astype(BF16)
            lams = (diff_lq1[o], diff_lk1[o], diff_lq2[o], diff_lk2[o])
            qk_scale = jnp.concatenate([jnp.full((n_qk // 2,), DIFF_SCALE * LOG2_E, F32),
                                        jnp.ones((n_qk // 2,), F32)])[None]

            def project(xs, m, rp):
                qk = _proj(xs, m, 1, w_qk, BF16, gain=qk_scale, rope=rp)
                v = _proj(xs, m, 1, w_v, BF16)
                n = xs.shape[0] // bsz
                return qk.reshape(bsz, n, -1), v.reshape(bsz, n, -1)

            qk, v = project(x_lat, mod_l, rope_hd)
            qk_c, v_c = project(x_ctx, mod_c, None)
            y = _diff_attention(qk, qk_c, v_c, qk, v, lams, diff_g_sub[o], lambda_init)
            ys_lat = [y.reshape(bsz * seq, -1)]
            if need_ctx:
                y_c = _diff_attention(qk_c, qk_c, v_c, None, None, lams, diff_g_sub[o], lambda_init)
                ys_ctx = [y_c.reshape(bsz * lc, -1)]

        x_lat = _outproj(ys_lat, w_o, x_lat, mod_l, 1, ln_g[i, 1], ln_b[i, 1], alpha)
        x_lat = ffn(x_lat, mod_l, 2, 1)
        if need_ctx:
            x_ctx = _outproj(ys_ctx, w_o, x_ctx, mod_c, 1, ln_g[i, 1], ln_b[i, 1], alpha)
            x_ctx = ffn(x_ctx, mod_c, 2, 1)
    return x_lat.reshape(bsz, seq, d)
```

```python
import functools
import math

import jax
import jax.numpy as jnp
from jax import lax
from jax.experimental import pallas as pl
from jax.experimental.pallas import tpu as pltpu

F32 = jnp.float32
BF16 = jnp.bfloat16

GRID_W = 64
ROPE_THETA = 10000.0
NORM_EPS = 1e-6
N_SUB = 3
HALF_STEP = 0.5
MLA_HEADS = 8
MLA_Q_RANK = 512
MLA_KV_RANK = 256
MLA_NOPE = 128
MLA_ROPE = 64
MLA_V = 128
MLA_SCALE = (MLA_NOPE + MLA_ROPE) ** -0.5
GQA_HEADS = 8
GQA_KV_HEADS = 2
GQA_HEAD_DIM = 128
GQA_SCALE = GQA_HEAD_DIM ** -0.5
DIFF_HEADS = 8
DIFF_HEAD_DIM = 128
DIFF_SCALE = DIFF_HEAD_DIM ** -0.5

LANES = 128
MOD_ROWS = 32
VMEM_LIMIT_BYTES = 56 * 1024 * 1024

TOKEN_TILE = 512
FFN_TOKEN_TILE = 1024
FF_TILE = 512
PROJ_TILE = 1024
ADA_TILE = 1024
Q_TILE = 2048
ROW_CHUNK = 256
COL_CHUNK = 256
LOG2_E = math.log2(math.e)


def _divisor_tile(n, pref, mult):
    if n <= pref:
        return n
    t = (pref // mult) * mult
    while t >= mult:
        if n % t == 0:
            return t
        t -= mult
    raise ValueError(f"no tile for {n} (pref {pref}, mult {mult})")


def _params(*sem):
    return pltpu.CompilerParams(dimension_semantics=sem, vmem_limit_bytes=VMEM_LIMIT_BYTES)


def _layer_norm(r, g, b):
    mu = jnp.mean(r, axis=-1, keepdims=True)
    d = r - mu
    var = jnp.mean(d * d, axis=-1, keepdims=True)
    return d * lax.rsqrt(var + NORM_EPS) * g + b


def _rms_norm(x, g):
    return x * lax.rsqrt(jnp.mean(x * x, axis=-1, keepdims=True) + NORM_EPS) * g


def _rope_block(blk, cos, sin):
    return blk * cos + pltpu.roll(blk, LANES // 2, 1) * sin


def _skewed_chunks(n, matmul_fn, finish_fn):
    nxt = matmul_fn(0)
    for c in range(n):
        cur, nxt = nxt, (matmul_fn(c + 1) if c + 1 < n else None)
        finish_fn(c, cur)


def _dot(a, b):
    return jnp.dot(a, b, preferred_element_type=F32)


def _dot_nt(a, b):
    return lax.dot_general(a, b, (((1,), (1,)), ((), ())), preferred_element_type=F32)


def _ada_kernel(c_ref, w_ref, b_ref, o_ref):
    c = c_ref[...]
    s = (c * jax.nn.sigmoid(c)).astype(BF16)
    o_ref[...] = _dot(s, w_ref[...].astype(BF16)) + b_ref[...]


def _ada(cc, w_ada, b_ada):
    depth, d, n = w_ada.shape
    tn = _divisor_tile(n, ADA_TILE, LANES)
    return pl.pallas_call(
        _ada_kernel,
        grid=(depth, n // tn),
        in_specs=[
            pl.BlockSpec((MOD_ROWS, d), lambda l, j: (0, 0)),
            pl.BlockSpec((None, d, tn), lambda l, j: (l, 0, j)),
            pl.BlockSpec((None, 1, tn), lambda l, j: (l, 0, j)),
        ],
        out_specs=pl.BlockSpec((None, MOD_ROWS, tn), lambda l, j: (l, 0, j)),
        out_shape=jax.ShapeDtypeStruct((depth, MOD_ROWS, n), F32),
        compiler_params=_params("arbitrary", "arbitrary"),
        name="ada_mod",
    )(cc, w_ada, b_ada.reshape(depth, 1, n))


class _Mod:
    def __init__(self, table, layer, batch, seq, is_ctx):
        self.table, self.layer, self.batch, self.seq, self.is_ctx = table, layer, batch, seq, is_ctx

    def spec(self, d, sub, kind, tm):
        base = self.layer * MOD_ROWS
        off = 3 * sub + kind
        if self.is_ctx:
            row = (base + self.batch) * 3 * N_SUB + off
            return pl.BlockSpec((None, 1, d), lambda i, *_: (row, 0, 0))
        tpb = self.seq // tm
        return pl.BlockSpec((None, 1, d), lambda i, *_: ((base + i // tpb) * 3 * N_SUB + off, 0, 0))


def _ffn_kernel(x_ref, shift_ref, scale_ref, gate_ref, w1_ref, w3_ref, w2_ref, g_ref, b_ref, o_ref,
                hb_ref, *, alpha, nf):
    f = pl.program_id(1)
    tm = x_ref.shape[0]

    def modulate(rows):
        hb_ref[rows, :] = (x_ref[rows, :] * (1.0 + scale_ref[...]) + shift_ref[...]).astype(BF16)

    def swiglu_chunk(rows):
        hb = hb_ref[rows, :]
        a = _dot(hb, w1_ref[...])
        u = _dot(hb, w3_ref[...])
        act = (a * jax.nn.sigmoid(a) * u).astype(BF16)
        return _dot(act, w2_ref[...])

    def post_norm(rows, y):
        r = alpha * x_ref[rows, :] + gate_ref[...] * (HALF_STEP * y)
        o_ref[rows, :] = _layer_norm(r, g_ref[...], b_ref[...])

    def step(first, last):
        rc = min(ROW_CHUNK, tm) if (first or last) else tm
        rows = lambda r: slice(r * rc, (r + 1) * rc)
        n = tm // rc
        if first:
            modulate(rows(0))

        def matmuls(r):
            if first and r + 1 < n:
                modulate(rows(r + 1))
            y = swiglu_chunk(rows(r))
            return y if first else o_ref[rows(r), :] + y

        def finish(r, y):
            if last:
                post_norm(rows(r), y)
            else:
                o_ref[rows(r), :] = y

        _skewed_chunks(n, matmuls, finish)

    if nf == 1:
        step(True, True)
        return
    pl.when(f == 0)(lambda: step(True, False))
    pl.when(f == nf - 1)(lambda: step(False, True))
    if nf > 2:
        pl.when(jnp.logical_and(f > 0, f < nf - 1))(lambda: step(False, False))


def _ffn(x, mod, sub, w1, w3, w2, g, b, alpha):
    t, d = x.shape
    f = w1.shape[1]
    tm = _divisor_tile(t if mod.is_ctx else mod.seq, FFN_TOKEN_TILE, 8)
    tf = _divisor_tile(f, FF_TILE, LANES)
    nf = f // tf
    row = pl.BlockSpec((tm, d), lambda i, k: (i, 0))
    vec = pl.BlockSpec((1, d), lambda i, k: (0, 0))
    return pl.pallas_call(
        functools.partial(_ffn_kernel, alpha=alpha, nf=nf),
        grid=(t // tm, nf),
        in_specs=[
            row,
            mod.spec(d, sub, 0, tm), mod.spec(d, sub, 1, tm), mod.spec(d, sub, 2, tm),
            pl.BlockSpec((d, tf), lambda i, k: (0, k)),
            pl.BlockSpec((d, tf), lambda i, k: (0, k)),
            pl.BlockSpec((tf, d), lambda i, k: (k, 0)),
            vec, vec,
        ],
        out_specs=row,
        out_shape=jax.ShapeDtypeStruct((t, d), F32),
        scratch_shapes=[pltpu.VMEM((tm, d), BF16)],
        compiler_params=_params("parallel", "arbitrary"),
        name="ffn_half_step",
    )(x, mod.table, mod.table, mod.table, w1, w3, w2, g.reshape(1, d), b.reshape(1, d))


def _proj_kernel(*refs, headnorm, colscale, rope):
    x_ref, shift_ref, scale_ref, w_ref = refs[:4]
    rest = list(refs[4:])
    gain_ref = rest.pop(0) if (headnorm or colscale) else None
    cos_ref, sin_ref = (rest.pop(0), rest.pop(0)) if rope else (None, None)
    o_ref, hb_ref = rest

    @pl.when(pl.program_id(1) == 0)
    def _():
        hb_ref[...] = (x_ref[...] * (1.0 + scale_ref[...]) + shift_ref[...]).astype(BF16)

    if not (headnorm or colscale or rope):
        o_ref[...] = _dot(hb_ref[...], w_ref[...]).astype(o_ref.dtype)
        return
    tn = o_ref.shape[1]
    cw = COL_CHUNK if tn % COL_CHUNK == 0 else LANES

    def finish(c, z):
        for kb in range(cw // LANES):
            cols = slice(c * cw + kb * LANES, c * cw + (kb + 1) * LANES)
            blk = z[:, kb * LANES:(kb + 1) * LANES]
            if headnorm:
                blk = _rms_norm(blk, gain_ref[:, cols])
            elif colscale:
                blk = blk * gain_ref[:, cols]
            if rope:
                blk = _rope_block(blk, cos_ref[...], sin_ref[...])
            o_ref[:, cols] = blk.astype(o_ref.dtype)

    _skewed_chunks(tn // cw, lambda c: _dot(hb_ref[...], w_ref[:, c * cw:(c + 1) * cw]), finish)


def _proj(x, mod, sub, w, col0, ncols, out_dtype, gain=None, headnorm=False, rope=None):
    t, d = x.shape
    tm = _divisor_tile(t if mod.is_ctx else mod.seq, TOKEN_TILE, 8)
    tn = _divisor_tile(ncols, PROJ_TILE, LANES)
    assert col0 % tn == 0
    joff = col0 // tn
    in_specs = [
        pl.BlockSpec((tm, d), lambda i, j: (i, 0)),
        mod.spec(d, sub, 0, tm), mod.spec(d, sub, 1, tm),
        pl.BlockSpec((d, tn), lambda i, j: (0, j + joff)),
    ]
    args = [x, mod.table, mod.table, w]
    if gain is not None:
        in_specs.append(pl.BlockSpec((1, tn), lambda i, j: (0, j)))
        args.append(gain)
    if rope is not None:
        tpb = mod.seq // tm
        tab = pl.BlockSpec((tm, LANES), lambda i, j: (i % tpb, 0))
        in_specs += [tab, tab]
        args += list(rope)
    return pl.pallas_call(
        functools.partial(_proj_kernel, headnorm=headnorm, colscale=gain is not None and not headnorm,
                          rope=rope is not None),
        grid=(t // tm, ncols // tn),
        in_specs=in_specs,
        out_specs=pl.BlockSpec((tm, tn), lambda i, j: (i, j)),
        out_shape=jax.ShapeDtypeStruct((t, ncols), out_dtype),
        scratch_shapes=[pltpu.VMEM((tm, d), BF16)],
        compiler_params=_params("parallel", "arbitrary"),
        name="mixer_in_proj",
    )(*args)


def _mla_up_kernel(*refs, rope, qscale):
    zm_ref, gcq_ref, gckv_ref, wuq_ref, wkn_ref, wv_ref = refs[:6]
    rest = list(refs[6:])
    cos_ref, sin_ref = (rest.pop(0), rest.pop(0)) if rope else (None, None)
    qa_ref, ka_ref, va_ref = rest
    hd = 2 * LANES

    def rot(blk):
        return _rope_block(blk, cos_ref[...], sin_ref[...]) if rope else blk

    nq = _rms_norm(zm_ref[:, :MLA_Q_RANK], gcq_ref[...]).astype(BF16)
    q = _dot(nq, wuq_ref[...]) * qscale
    for h in range(MLA_HEADS):
        qa_ref[:, h * hd:h * hd + LANES] = q[:, h * hd:h * hd + LANES].astype(BF16)
        qa_ref[:, h * hd + LANES:(h + 1) * hd] = rot(q[:, h * hd + LANES:(h + 1) * hd]).astype(BF16)

    nkv = _rms_norm(zm_ref[:, MLA_Q_RANK:MLA_Q_RANK + MLA_KV_RANK], gckv_ref[...]).astype(BF16)
    kn = _dot(nkv, wkn_ref[...])
    va_ref[...] = _dot(nkv, wv_ref[...]).astype(BF16)
    kr = rot(zm_ref[:, MLA_Q_RANK + MLA_KV_RANK:MLA_Q_RANK + MLA_KV_RANK + LANES]).astype(BF16)
    for h in range(MLA_HEADS):
        ka_ref[:, h * hd:h * hd + LANES] = kn[:, h * LANES:(h + 1) * LANES].astype(BF16)
        ka_ref[:, h * hd + LANES:(h + 1) * hd] = kr


def _mla_up(zm, seq, g_cq, g_ckv, wuq, wkn, wv, rope, qscale):
    t, nz = zm.shape
    tm = _divisor_tile(seq if rope is not None else t, TOKEN_TILE, 8)
    hq = MLA_HEADS * 2 * LANES
    hv = MLA_HEADS * MLA_V
    full = lambda a: pl.BlockSpec(a.shape, lambda i: (0, 0))
    g_cq = g_cq.reshape(1, -1)
    g_ckv = g_ckv.reshape(1, -1)
    in_specs = [pl.BlockSpec((tm, nz), lambda i: (i, 0)), full(g_cq), full(g_ckv), full(wuq), full(wkn), full(wv)]
    args = [zm, g_cq, g_ckv, wuq, wkn, wv]
    if rope is not None:
        tpb = seq // tm
        tab = pl.BlockSpec((tm, LANES), lambda i: (i % tpb, 0))
        in_specs += [tab, tab]
        args += list(rope)
    return pl.pallas_call(
        functools.partial(_mla_up_kernel, rope=rope is not None, qscale=qscale),
        grid=(t // tm,),
        in_specs=in_specs,
        out_specs=[pl.BlockSpec((tm, hq), lambda i: (i, 0)), pl.BlockSpec((tm, hq), lambda i: (i, 0)),
                   pl.BlockSpec((tm, hv), lambda i: (i, 0))],
        out_shape=[jax.ShapeDtypeStruct((t, hq), BF16), jax.ShapeDtypeStruct((t, hq), BF16),
                   jax.ShapeDtypeStruct((t, hv), BF16)],
        compiler_params=_params("parallel"),
        name="mla_up_proj",
    )(*args)


def _raw_scores(q, kc_ref, kl_ref, kcols):
    sc = _dot_nt(q, kc_ref[:, kcols])
    sl = None if kl_ref is None else _dot_nt(q, kl_ref[:, kcols])
    return sc, sl


def _softmax_numerators(scores):
    sc, sl = scores
    m = jnp.max(sc, axis=-1, keepdims=True)
    if sl is None:
        return jnp.exp2(sc - m), None
    m = jnp.maximum(m, jnp.max(sl, axis=-1, keepdims=True))
    return jnp.exp2(sc - m), jnp.exp2(sl - m)


def _row_sum(ec, el):
    den = jnp.sum(ec, axis=-1, keepdims=True)
    return den if el is None else den + jnp.sum(el, axis=-1, keepdims=True)


def _attn_kernel(*refs, has_lat, row_chunk):
    if has_lat:
        q_ref, kc_ref, vc_ref, kl_ref, vl_ref, o_ref = refs
    else:
        (q_ref, kc_ref, vc_ref, o_ref), kl_ref, vl_ref = refs, None, None
    rows = lambda r: slice(r * row_chunk, (r + 1) * row_chunk)
    dv = vc_ref.shape[1]
    with_ones = lambda v: jnp.concatenate([v, jnp.ones_like(v)], axis=1)
    vc1 = with_ones(vc_ref[...])
    vl1 = with_ones(vl_ref[...]) if has_lat else None

    def finish(r, scores):
        ec, el = _softmax_numerators(scores)
        o = _dot(ec.astype(BF16), vc1)
        if has_lat:
            o = o + _dot(el.astype(BF16), vl1)
        o_ref[rows(r), :] = (o[:, :dv] / o[:, dv:]).astype(o_ref.dtype)

    _skewed_chunks(q_ref.shape[0] // row_chunk,
                   lambda r: _raw_scores(q_ref[rows(r), :], kc_ref, kl_ref, slice(None)), finish)


def _attention(q, kc, vc, kl, vl, *, heads, group, dqk, dv, q_col0, k_col0, v_col0):
    assert dv == LANES
    b, lq, _ = q.shape
    lc = kc.shape[1]
    tq = _divisor_tile(lq, Q_TILE, 8)
    qo, ko, vo = q_col0 // dqk, k_col0 // dqk, v_col0 // dv
    in_specs = [
        pl.BlockSpec((None, tq, dqk), lambda bi, h, i: (bi, i, qo + h)),
        pl.BlockSpec((None, lc, dqk), lambda bi, h, i: (bi, 0, ko + h // group)),
        pl.BlockSpec((None, lc, dv), lambda bi, h, i: (bi, 0, vo + h // group)),
    ]
    args = [q, kc, vc]
    if kl is not None:
        ll = kl.shape[1]
        in_specs += [
            pl.BlockSpec((None, ll, dqk), lambda bi, h, i: (bi, 0, ko + h // group)),
            pl.BlockSpec((None, ll, dv), lambda bi, h, i: (bi, 0, vo + h // group)),
        ]
        args += [kl, vl]
    return pl.pallas_call(
        functools.partial(_attn_kernel, has_lat=kl is not None, row_chunk=min(ROW_CHUNK, tq)),
        grid=(b, heads, lq // tq),
        in_specs=in_specs,
        out_specs=pl.BlockSpec((None, tq, dv), lambda bi, h, i: (bi, i, h)),
        out_shape=jax.ShapeDtypeStruct((b, lq, heads * dv), BF16),
        compiler_params=_params("parallel", "arbitrary", "arbitrary"),
        name="softmax_attention",
    )(*args)


def _diff_attn_kernel(*refs, lambda_init, has_lat, row_chunk):
    if has_lat:
        q_ref, kc_ref, vc_ref, kl_ref, vl_ref, lq1, lk1, lq2, lk2, gs_ref, o_ref = refs
    else:
        (q_ref, kc_ref, vc_ref, lq1, lk1, lq2, lk2, gs_ref, o_ref), kl_ref, vl_ref = refs, None, None
    lam = (jnp.exp(jnp.sum(lq1[...] * lk1[...], axis=-1, keepdims=True))
           - jnp.exp(jnp.sum(lq2[...] * lk2[...], axis=-1, keepdims=True)) + lambda_init)
    hd = DIFF_HEAD_DIM
    rows = lambda r: slice(r * row_chunk, (r + 1) * row_chunk)

    def scores(r):
        return (_raw_scores(q_ref[rows(r), :hd], kc_ref, kl_ref, slice(0, hd)),
                _raw_scores(q_ref[rows(r), hd:], kc_ref, kl_ref, slice(hd, 2 * hd)))

    def finish(r, s01):
        ec0, el0 = _softmax_numerators(s01[0])
        ec1, el1 = _softmax_numerators(s01[1])
        den0 = _row_sum(ec0, el0)
        ratio = lam * den0 / _row_sum(ec1, el1)
        y = _dot((ec0 - ec1 * ratio).astype(BF16), vc_ref[...])
        if has_lat:
            y = y + _dot((el0 - el1 * ratio).astype(BF16), vl_ref[...])
        y = y / den0
        o_ref[rows(r), :] = (_rms_norm(y, gs_ref[...]) * (1.0 - lambda_init)).astype(o_ref.dtype)

    _skewed_chunks(q_ref.shape[0] // row_chunk, scores, finish)


def _diff_attention(q, kc, vc, kl, vl, lams, g_sub, lambda_init):
    b, lq, _ = q.shape
    lc = kc.shape[1]
    tq = _divisor_tile(lq, Q_TILE, 8)
    w = 2 * DIFF_HEAD_DIM
    in_specs = [
        pl.BlockSpec((None, tq, w), lambda bi, h, i: (bi, i, h)),
        pl.BlockSpec((None, lc, w), lambda bi, h, i: (bi, 0, DIFF_HEADS + h)),
        pl.BlockSpec((None, lc, w), lambda bi, h, i: (bi, 0, h)),
    ]
    args = [q, kc, vc]
    if kl is not None:
        ll = kl.shape[1]
        in_specs += [
            pl.BlockSpec((None, ll, w), lambda bi, h, i: (bi, 0, DIFF_HEADS + h)),
            pl.BlockSpec((None, ll, w), lambda bi, h, i: (bi, 0, h)),
        ]
        args += [kl, vl]
    small = lambda n: pl.BlockSpec((1, n), lambda bi, h, i: (0, 0))
    in_specs += [small(DIFF_HEAD_DIM)] * 4 + [small(w)]
    args += [a.reshape(1, -1) for a in lams] + [g_sub.reshape(1, -1)]
    return pl.pallas_call(
        functools.partial(_diff_attn_kernel, lambda_init=lambda_init, has_lat=kl is not None,
                          row_chunk=min(ROW_CHUNK, tq)),
        grid=(b, DIFF_HEADS, lq // tq),
        in_specs=in_specs,
        out_specs=pl.BlockSpec((None, tq, w), lambda bi, h, i: (bi, i, h)),
        out_shape=jax.ShapeDtypeStruct((b, lq, DIFF_HEADS * w), BF16),
        compiler_params=_params("parallel", "arbitrary", "arbitrary"),
        name="diff_attention",
    )(*args)


def _outproj_kernel(*refs, nparts, alpha):
    y_refs = refs[:nparts]
    w_refs = refs[nparts:2 * nparts]
    x_ref, gate_ref, g_ref, b_ref, o_ref = refs[2 * nparts:]
    row_chunk = min(ROW_CHUNK, o_ref.shape[0])
    rows = lambda r: slice(r * row_chunk, (r + 1) * row_chunk)

    def project(r):
        y = _dot(y_refs[0][rows(r), :], w_refs[0][...])
        for p in range(1, nparts):
            y = y + _dot(y_refs[p][rows(r), :], w_refs[p][...])
        return y

    def finish(r, y):
        res = alpha * x_ref[rows(r), :] + gate_ref[...] * y
        o_ref[rows(r), :] = _layer_norm(res, g_ref[...], b_ref[...])

    _skewed_chunks(o_ref.shape[0] // row_chunk, project, finish)


def _outproj(ys, w_o, x, mod, sub, g, b, alpha):
    t, d = x.shape
    tm = _divisor_tile(t if mod.is_ctx else mod.seq, TOKEN_TILE, 8)
    kp = ys[0].shape[1]
    assert all(y.shape[1] == kp for y in ys) and kp * len(ys) == w_o.shape[0]
    vec = pl.BlockSpec((1, d), lambda i: (0, 0))
    in_specs = [pl.BlockSpec((tm, kp), lambda i: (i, 0)) for _ in ys]
    in_specs += [pl.BlockSpec((kp, d), functools.partial(lambda i, p: (p, 0), p=p)) for p in range(len(ys))]
    in_specs += [pl.BlockSpec((tm, d), lambda i: (i, 0)), mod.spec(d, sub, 2, tm), vec, vec]
    return pl.pallas_call(
        functools.partial(_outproj_kernel, nparts=len(ys), alpha=alpha),
        grid=(t // tm,),
        in_specs=in_specs,
        out_specs=pl.BlockSpec((tm, d), lambda i: (i, 0)),
        out_shape=jax.ShapeDtypeStruct((t, d), F32),
        compiler_params=_params("parallel"),
        name="mixer_out_proj",
    )(*ys, *([w_o] * len(ys)), x, mod.table, g.reshape(1, d), b.reshape(1, d))


def _rope_tables(rows, rot_dim):
    r, col = jnp.meshgrid(jnp.arange(rows, dtype=F32), jnp.arange(GRID_W, dtype=F32), indexing="ij")
    n_freq = rot_dim // 4
    inv_freq = ROPE_THETA ** (-jnp.arange(n_freq, dtype=F32) / n_freq)
    ang = jnp.concatenate([r.reshape(-1, 1) * inv_freq, col.reshape(-1, 1) * inv_freq], -1)
    cos, sin = jnp.cos(ang), jnp.sin(ang)
    pad = LANES // 2 - rot_dim // 2
    cos = jnp.pad(cos, ((0, 0), (0, pad)), constant_values=1.0)
    sin = jnp.pad(sin, ((0, 0), (0, pad)))
    return jnp.concatenate([cos, cos], -1), jnp.concatenate([-sin, sin], -1)


def _spread_rot(w):
    half = MLA_ROPE // 2
    z = jnp.zeros((w.shape[0], LANES // 2 - half), w.dtype)
    return jnp.concatenate([w[:, :half], z, w[:, half:], z], -1)


def kernel(x, c, ctx, c_ctx, w_ada, b_ada, ln_g, ln_b, ffn_w1, ffn_w3, ffn_w2, mg_w_in, mla_g_cq, mla_g_ckv,
           mla_w_uq, mla_w_ukv, gqa_g_q, gqa_g_k, mg_w_o, diff_w_in, diff_lq1, diff_lk1, diff_lq2, diff_lk2,
           diff_g_sub, diff_w_o):
    bsz, seq, d = x.shape
    lc = ctx.shape[1]
    depth = w_ada.shape[0]
    assert bsz + 1 <= MOD_ROWS and seq % GRID_W == 0
    alpha = (2 * depth) ** 0.25
    rows = seq // GRID_W
    rope_mla = _rope_tables(rows, MLA_ROPE)
    rope_hd = _rope_tables(rows, GQA_HEAD_DIM)

    cc = jnp.concatenate([c, c_ctx[None], jnp.zeros((MOD_ROWS - bsz - 1, d), F32)], 0)
    mod_table = _ada(cc, w_ada, b_ada).reshape(depth * MOD_ROWS * 3 * N_SUB, 1, d)

    x_lat = x.reshape(bsz * seq, d)
    x_ctx = ctx.reshape(bsz * lc, d)
    w1b, w3b, w2b = ffn_w1.astype(BF16), ffn_w3.astype(BF16), ffn_w2.astype(BF16)

    for i in range(depth):
        need_ctx = i < depth - 1
        mod_l = _Mod(mod_table, i, bsz, seq, False)
        mod_c = _Mod(mod_table, i, bsz, seq, True)
        ffn = lambda xs, m, sub, k: _ffn(xs, m, sub, w1b[i, k], w3b[i, k], w2b[i, k], ln_g[i, sub], ln_b[i, sub], alpha)

        x_lat = ffn(x_lat, mod_l, 0, 0)
        x_ctx = ffn(x_ctx, mod_c, 0, 0)

        if i % 2 == 0:
            e = i // 2
            w_in = mg_w_in[e]
            o_kr = MLA_Q_RANK + MLA_KV_RANK
            o_gq = o_kr + MLA_ROPE
            n_gqk = (GQA_HEADS + GQA_KV_HEADS) * GQA_HEAD_DIM
            w_mla = jnp.concatenate([w_in[:, :o_kr], _spread_rot(w_in[:, o_kr:o_gq])], -1).astype(BF16)
            w_gqa = w_in[:, o_gq:].astype(BF16)
            wuq = mla_w_uq[e].reshape(MLA_Q_RANK, MLA_HEADS, MLA_NOPE + MLA_ROPE)
            wuq = jnp.concatenate(
                [wuq[:, :, :MLA_NOPE],
                 _spread_rot(wuq[:, :, MLA_NOPE:].reshape(-1, MLA_ROPE)).reshape(MLA_Q_RANK, MLA_HEADS, LANES)],
                -1).reshape(MLA_Q_RANK, MLA_HEADS * 2 * LANES).astype(BF16)
            wukv = mla_w_ukv[e].reshape(MLA_KV_RANK, MLA_HEADS, MLA_NOPE + MLA_V)
            wkn = wukv[:, :, :MLA_NOPE].reshape(MLA_KV_RANK, -1).astype(BF16)
            wv = wukv[:, :, MLA_NOPE:].reshape(MLA_KV_RANK, -1).astype(BF16)
            gain = jnp.concatenate([jnp.tile(gqa_g_q[e] * (GQA_SCALE * LOG2_E), GQA_HEADS),
                                    jnp.tile(gqa_g_k[e], GQA_KV_HEADS)])[None]
            w_o = mg_w_o[e].astype(BF16)

            def project(xs, m, rp_mla, rp_hd):
                zm = _proj(xs, m, 1, w_mla, 0, w_mla.shape[1], F32)
                qk = _proj(xs, m, 1, w_gqa, 0, n_gqk, BF16, gain=gain, headnorm=True, rope=rp_hd)
                v = _proj(xs, m, 1, w_gqa, n_gqk, w_gqa.shape[1] - n_gqk, BF16)
                qa, ka, va = _mla_up(zm, seq, mla_g_cq[e], mla_g_ckv[e], wuq, wkn, wv, rp_mla,
                                     MLA_SCALE * LOG2_E)
                n = xs.shape[0] // bsz
                return [a.reshape(bsz, n, -1) for a in (qa, ka, va, qk, v)]

            qa, ka, va, qkb, vb = project(x_lat, mod_l, rope_mla, rope_hd)
            qa_c, ka_c, va_c, qkb_c, vb_c = project(x_ctx, mod_c, None, None)
            mla = dict(heads=MLA_HEADS, group=1, dqk=2 * LANES, dv=MLA_V, q_col0=0, k_col0=0, v_col0=0)
            gqa = dict(heads=GQA_HEADS, group=GQA_HEADS // GQA_KV_HEADS, dqk=GQA_HEAD_DIM, dv=GQA_HEAD_DIM,
                       q_col0=0, k_col0=GQA_HEADS * GQA_HEAD_DIM, v_col0=0)
            ya = _attention(qa, ka_c, va_c, ka, va, **mla)
            yb = _attention(qkb, qkb_c, vb_c, qkb, vb, **gqa)
            ys_lat = [ya.reshape(bsz * seq, -1), yb.reshape(bsz * seq, -1)]
            if need_ctx:
                ya_c = _attention(qa_c, ka_c, va_c, None, None, **mla)
                yb_c = _attention(qkb_c, qkb_c, vb_c, None, None, **gqa)
                ys_ctx = [ya_c.reshape(bsz * lc, -1), yb_c.reshape(bsz * lc, -1)]
        else:
            o = i // 2
            lambda_init = 0.8 - 0.6 * math.exp(-0.3 * i)
            w_in = diff_w_in[o].astype(BF16)
            n_qk = 2 * DIFF_HEADS * 2 * DIFF_HEAD_DIM
            w_o = diff_w_o[o].astype(BF16)
            lams = (diff_lq1[o], diff_lk1[o], diff_lq2[o], diff_lk2[o])
            qk_scale = jnp.concatenate([jnp.full((n_qk // 2,), DIFF_SCALE * LOG2_E, F32),
                                        jnp.ones((n_qk // 2,), F32)])[None]

            def project(xs, m, rp):
                qk = _proj(xs, m, 1, w_in, 0, n_qk, BF16, gain=qk_scale, rope=rp)
                v = _proj(xs, m, 1, w_in, n_qk, w_in.shape[1] - n_qk, BF16)
                n = xs.shape[0] // bsz
                return qk.reshape(bsz, n, -1), v.reshape(bsz, n, -1)

            qk, v = project(x_lat, mod_l, rope_hd)
            qk_c, v_c = project(x_ctx, mod_c, None)
            y = _diff_attention(qk, qk_c, v_c, qk, v, lams, diff_g_sub[o], lambda_init)
            ys_lat = [y.reshape(bsz * seq, -1)]
            if need_ctx:
                y_c = _diff_attention(qk_c, qk_c, v_c, None, None, lams, diff_g_sub[o], lambda_init)
                ys_ctx = [y_c.reshape(bsz * lc, -1)]

        x_lat = _outproj(ys_lat, w_o, x_lat, mod_l, 1, ln_g[i, 1], ln_b[i, 1], alpha)
        x_lat = ffn(x_lat, mod_l, 2, 1)
        if need_ctx:
            x_ctx = _outproj(ys_ctx, w_o, x_ctx, mod_c, 1, ln_g[i, 1], ln_b[i, 1], alpha)
            x_ctx = ffn(x_ctx, mod_c, 2, 1)
    return x_lat.reshape(bsz, seq, d)
```

```python
import functools
import math

import jax
import jax.numpy as jnp
from jax import lax
from jax.experimental import pallas as pl
from jax.experimental.pallas import tpu as pltpu

F32 = jnp.float32
BF16 = jnp.bfloat16

GRID_W = 64
ROPE_THETA = 10000.0
NORM_EPS = 1e-6
N_SUB = 3
HALF_STEP = 0.5
MLA_HEADS = 8
MLA_Q_RANK = 512
MLA_KV_RANK = 256
MLA_NOPE = 128
MLA_ROPE = 64
MLA_V = 128
MLA_SCALE = (MLA_NOPE + MLA_ROPE) ** -0.5
GQA_HEADS = 8
GQA_KV_HEADS = 2
GQA_HEAD_DIM = 128
GQA_SCALE = GQA_HEAD_DIM ** -0.5
DIFF_HEADS = 8
DIFF_HEAD_DIM = 128
DIFF_SCALE = DIFF_HEAD_DIM ** -0.5

LANES = 128
MOD_ROWS = 32
VMEM_LIMIT_BYTES = 56 * 1024 * 1024

TOKEN_TILE = 1024
OUT_TOKEN_TILE = 512
FF_TILE = 512
PROJ_TILE = 1024
ADA_TILE = 1024
Q_TILE = 2048
ROW_CHUNK = 256
COL_CHUNK = 256
LOG2_E = math.log2(math.e)


def _divisor_tile(n, pref, mult):
    if n <= pref:
        return n
    t = (pref // mult) * mult
    while t >= mult:
        if n % t == 0:
            return t
        t -= mult
    raise ValueError(f"no tile for {n} (pref {pref}, mult {mult})")


def _params(*sem):
    return pltpu.CompilerParams(dimension_semantics=sem, vmem_limit_bytes=VMEM_LIMIT_BYTES)


def _layer_norm(r, g, b):
    mu = jnp.mean(r, axis=-1, keepdims=True)
    d = r - mu
    var = jnp.mean(d * d, axis=-1, keepdims=True)
    return d * lax.rsqrt(var + NORM_EPS) * g + b


def _rms_norm(x, g):
    return x * lax.rsqrt(jnp.mean(x * x, axis=-1, keepdims=True) + NORM_EPS) * g


def _rope_block(blk, cos, sin):
    return blk * cos + pltpu.roll(blk, LANES // 2, 1) * sin


def _skewed_chunks(n, matmul_fn, finish_fn):
    nxt = matmul_fn(0)
    for c in range(n):
        cur, nxt = nxt, (matmul_fn(c + 1) if c + 1 < n else None)
        finish_fn(c, cur)


def _dot(a, b):
    return jnp.dot(a, b, preferred_element_type=F32)


def _dot_nt(a, b):
    return lax.dot_general(a, b, (((1,), (1,)), ((), ())), preferred_element_type=F32)


def _ada_kernel(c_ref, w_ref, b_ref, o_ref):
    c = c_ref[...]
    s = (c * jax.nn.sigmoid(c)).astype(BF16)
    o_ref[...] = _dot(s, w_ref[...].astype(BF16)) + b_ref[...]


def _ada(cc, w_ada, b_ada):
    depth, d, n = w_ada.shape
    tn = _divisor_tile(n, ADA_TILE, LANES)
    return pl.pallas_call(
        _ada_kernel,
        grid=(depth, n // tn),
        in_specs=[
            pl.BlockSpec((MOD_ROWS, d), lambda l, j: (0, 0)),
            pl.BlockSpec((None, d, tn), lambda l, j: (l, 0, j)),
            pl.BlockSpec((None, 1, tn), lambda l, j: (l, 0, j)),
        ],
        out_specs=pl.BlockSpec((None, MOD_ROWS, tn), lambda l, j: (l, 0, j)),
        out_shape=jax.ShapeDtypeStruct((depth, MOD_ROWS, n), F32),
        compiler_params=_params("arbitrary", "arbitrary"),
        name="ada_mod",
    )(cc, w_ada, b_ada.reshape(depth, 1, n))


class _Mod:
    def __init__(self, table, layer, batch, seq, is_ctx):
        self.table, self.layer, self.batch, self.seq, self.is_ctx = table, layer, batch, seq, is_ctx

    def spec(self, d, sub, kind, tm):
        base = self.layer * MOD_ROWS
        off = 3 * sub + kind
        if self.is_ctx:
            row = (base + self.batch) * 3 * N_SUB + off
            return pl.BlockSpec((None, 1, d), lambda i, *_: (row, 0, 0))
        tpb = self.seq // tm
        return pl.BlockSpec((None, 1, d), lambda i, *_: ((base + i // tpb) * 3 * N_SUB + off, 0, 0))


def _ffn_kernel(x_ref, shift_ref, scale_ref, gate_ref, w1_ref, w3_ref, w2_ref, g_ref, b_ref, o_ref,
                hb_ref, *, alpha, nf):
    f = pl.program_id(1)
    tm = x_ref.shape[0]

    def modulate(rows):
        hb_ref[rows, :] = (x_ref[rows, :] * (1.0 + scale_ref[...]) + shift_ref[...]).astype(BF16)

    def swiglu_chunk(rows):
        hb = hb_ref[rows, :]
        a = _dot(hb, w1_ref[...])
        u = _dot(hb, w3_ref[...])
        act = (a * jax.nn.sigmoid(a) * u).astype(BF16)
        return _dot(act, w2_ref[...])

    def post_norm(rows, y):
        r = alpha * x_ref[rows, :] + gate_ref[...] * (HALF_STEP * y)
        o_ref[rows, :] = _layer_norm(r, g_ref[...], b_ref[...])

    def step(first, last):
        rc = min(ROW_CHUNK, tm) if (first or last) else tm
        rows = lambda r: slice(r * rc, (r + 1) * rc)
        n = tm // rc
        if first:
            modulate(rows(0))

        def matmuls(r):
            if first and r + 1 < n:
                modulate(rows(r + 1))
            y = swiglu_chunk(rows(r))
            return y if first else o_ref[rows(r), :] + y

        def finish(r, y):
            if last:
                post_norm(rows(r), y)
            else:
                o_ref[rows(r), :] = y

        _skewed_chunks(n, matmuls, finish)

    if nf == 1:
        step(True, True)
        return
    pl.when(f == 0)(lambda: step(True, False))
    pl.when(f == nf - 1)(lambda: step(False, True))
    if nf > 2:
        pl.when(jnp.logical_and(f > 0, f < nf - 1))(lambda: step(False, False))


def _ffn(x, mod, sub, w1, w3, w2, half, g, b, alpha):
    t, d = x.shape
    f = w1.shape[-1]
    layer = mod.layer
    tm = _divisor_tile(t if mod.is_ctx else mod.seq, TOKEN_TILE, 8)
    tf = _divisor_tile(f, FF_TILE, LANES)
    nf = f // tf
    row = pl.BlockSpec((tm, d), lambda i, k: (i, 0))
    vec = pl.BlockSpec((1, d), lambda i, k: (0, 0))
    return pl.pallas_call(
        functools.partial(_ffn_kernel, alpha=alpha, nf=nf),
        grid=(t // tm, nf),
        in_specs=[
            row,
            mod.spec(d, sub, 0, tm), mod.spec(d, sub, 1, tm), mod.spec(d, sub, 2, tm),
            pl.BlockSpec((None, None, d, tf), lambda i, k: (layer, half, 0, k)),
            pl.BlockSpec((None, None, d, tf), lambda i, k: (layer, half, 0, k)),
            pl.BlockSpec((None, None, tf, d), lambda i, k: (layer, half, k, 0)),
            vec, vec,
        ],
        out_specs=row,
        out_shape=jax.ShapeDtypeStruct((t, d), F32),
        scratch_shapes=[pltpu.VMEM((tm, d), BF16)],
        compiler_params=_params("parallel", "arbitrary"),
        name="ffn_half_step",
    )(x, mod.table, mod.table, mod.table, w1, w3, w2, g.reshape(1, d), b.reshape(1, d))


def _proj_kernel(*refs, headnorm, colscale, rope):
    x_ref, shift_ref, scale_ref, w_ref = refs[:4]
    rest = list(refs[4:])
    gain_ref = rest.pop(0) if (headnorm or colscale) else None
    cos_ref, sin_ref = (rest.pop(0), rest.pop(0)) if rope else (None, None)
    o_ref, hb_ref = rest

    @pl.when(pl.program_id(1) == 0)
    def _():
        hb_ref[...] = (x_ref[...] * (1.0 + scale_ref[...]) + shift_ref[...]).astype(BF16)

    if not (headnorm or colscale or rope):
        o_ref[...] = _dot(hb_ref[...], w_ref[...]).astype(o_ref.dtype)
        return
    tn = o_ref.shape[1]
    cw = COL_CHUNK if tn % COL_CHUNK == 0 else LANES

    def finish(c, z):
        for kb in range(cw // LANES):
            cols = slice(c * cw + kb * LANES, c * cw + (kb + 1) * LANES)
            blk = z[:, kb * LANES:(kb + 1) * LANES]
            if headnorm:
                blk = _rms_norm(blk, gain_ref[:, cols])
            elif colscale:
                blk = blk * gain_ref[:, cols]
            if rope:
                blk = _rope_block(blk, cos_ref[...], sin_ref[...])
            o_ref[:, cols] = blk.astype(o_ref.dtype)

    _skewed_chunks(tn // cw, lambda c: _dot(hb_ref[...], w_ref[:, c * cw:(c + 1) * cw]), finish)


def _proj(x, mod, sub, w, col0, ncols, out_dtype, gain=None, headnorm=False, rope=None):
    t, d = x.shape
    tm = _divisor_tile(t if mod.is_ctx else mod.seq, TOKEN_TILE, 8)
    tn = _divisor_tile(ncols, PROJ_TILE, LANES)
    assert col0 % tn == 0
    joff = col0 // tn
    in_specs = [
        pl.BlockSpec((tm, d), lambda i, j: (i, 0)),
        mod.spec(d, sub, 0, tm), mod.spec(d, sub, 1, tm),
        pl.BlockSpec((d, tn), lambda i, j: (0, j + joff)),
    ]
    args = [x, mod.table, mod.table, w]
    if gain is not None:
        in_specs.append(pl.BlockSpec((1, tn), lambda i, j: (0, j)))
        args.append(gain)
    if rope is not None:
        tpb = mod.seq // tm
        tab = pl.BlockSpec((tm, LANES), lambda i, j: (i % tpb, 0))
        in_specs += [tab, tab]
        args += list(rope)
    return pl.pallas_call(
        functools.partial(_proj_kernel, headnorm=headnorm, colscale=gain is not None and not headnorm,
                          rope=rope is not None),
        grid=(t // tm, ncols // tn),
        in_specs=in_specs,
        out_specs=pl.BlockSpec((tm, tn), lambda i, j: (i, j)),
        out_shape=jax.ShapeDtypeStruct((t, ncols), out_dtype),
        scratch_shapes=[pltpu.VMEM((tm, d), BF16)],
        compiler_params=_params("parallel", "arbitrary"),
        name="mixer_in_proj",
    )(*args)


def _mla_up_kernel(*refs, rope, qscale):
    zm_ref, gcq_ref, gckv_ref, wuq_ref, wkn_ref, wv_ref = refs[:6]
    rest = list(refs[6:])
    cos_ref, sin_ref = (rest.pop(0), rest.pop(0)) if rope else (None, None)
    qa_ref, ka_ref, va_ref = rest
    hd = 2 * LANES

    def rot(blk):
        return _rope_block(blk, cos_ref[...], sin_ref[...]) if rope else blk

    nq = _rms_norm(zm_ref[:, :MLA_Q_RANK], gcq_ref[...]).astype(BF16)
    q = _dot(nq, wuq_ref[...]) * qscale
    for h in range(MLA_HEADS):
        qa_ref[:, h * hd:h * hd + LANES] = q[:, h * hd:h * hd + LANES].astype(BF16)
        qa_ref[:, h * hd + LANES:(h + 1) * hd] = rot(q[:, h * hd + LANES:(h + 1) * hd]).astype(BF16)

    nkv = _rms_norm(zm_ref[:, MLA_Q_RANK:MLA_Q_RANK + MLA_KV_RANK], gckv_ref[...]).astype(BF16)
    kn = _dot(nkv, wkn_ref[...])
    va_ref[...] = _dot(nkv, wv_ref[...]).astype(BF16)
    kr = rot(zm_ref[:, MLA_Q_RANK + MLA_KV_RANK:MLA_Q_RANK + MLA_KV_RANK + LANES]).astype(BF16)
    for h in range(MLA_HEADS):
        ka_ref[:, h * hd:h * hd + LANES] = kn[:, h * LANES:(h + 1) * LANES].astype(BF16)
        ka_ref[:, h * hd + LANES:(h + 1) * hd] = kr


def _mla_up(zm, seq, g_cq, g_ckv, wuq, wkn, wv, rope, qscale):
    t, nz = zm.shape
    tm = _divisor_tile(seq if rope is not None else t, TOKEN_TILE, 8)
    hq = MLA_HEADS * 2 * LANES
    hv = MLA_HEADS * MLA_V
    full = lambda a: pl.BlockSpec(a.shape, lambda i: (0, 0))
    g_cq = g_cq.reshape(1, -1)
    g_ckv = g_ckv.reshape(1, -1)
    in_specs = [pl.BlockSpec((tm, nz), lambda i: (i, 0)), full(g_cq), full(g_ckv), full(wuq), full(wkn), full(wv)]
    args = [zm, g_cq, g_ckv, wuq, wkn, wv]
    if rope is not None:
        tpb = seq // tm
        tab = pl.BlockSpec((tm, LANES), lambda i: (i % tpb, 0))
        in_specs += [tab, tab]
        args += list(rope)
    return pl.pallas_call(
        functools.partial(_mla_up_kernel, rope=rope is not None, qscale=qscale),
        grid=(t // tm,),
        in_specs=in_specs,
        out_specs=[pl.BlockSpec((tm, hq), lambda i: (i, 0)), pl.BlockSpec((tm, hq), lambda i: (i, 0)),
                   pl.BlockSpec((tm, hv), lambda i: (i, 0))],
        out_shape=[jax.ShapeDtypeStruct((t, hq), BF16), jax.ShapeDtypeStruct((t, hq), BF16),
                   jax.ShapeDtypeStruct((t, hv), BF16)],
        compiler_params=_params("parallel"),
        name="mla_up_proj",
    )(*args)


def _raw_scores(q, kc_ref, kl_ref, kcols):
    sc = _dot_nt(q, kc_ref[:, kcols])
    sl = None if kl_ref is None else _dot_nt(q, kl_ref[:, kcols])
    return sc, sl


def _softmax_numerators(scores):
    sc, sl = scores
    m = jnp.max(sc, axis=-1, keepdims=True)
    if sl is None:
        return jnp.exp2(sc - m), None
    m = jnp.maximum(m, jnp.max(sl, axis=-1, keepdims=True))
    return jnp.exp2(sc - m), jnp.exp2(sl - m)


def _row_sum(ec, el):
    den = jnp.sum(ec, axis=-1, keepdims=True)
    return den if el is None else den + jnp.sum(el, axis=-1, keepdims=True)


def _attn_kernel(*refs, has_lat, row_chunk):
    if has_lat:
        q_ref, kc_ref, vc_ref, kl_ref, vl_ref, o_ref = refs
    else:
        (q_ref, kc_ref, vc_ref, o_ref), kl_ref, vl_ref = refs, None, None
    rows = lambda r: slice(r * row_chunk, (r + 1) * row_chunk)
    dv = vc_ref.shape[1]
    with_ones = lambda v: jnp.concatenate([v, jnp.ones_like(v)], axis=1)
    vc1 = with_ones(vc_ref[...])
    vl1 = with_ones(vl_ref[...]) if has_lat else None

    def finish(r, scores):
        ec, el = _softmax_numerators(scores)
        o = _dot(ec.astype(BF16), vc1)
        if has_lat:
            o = o + _dot(el.astype(BF16), vl1)
        o_ref[rows(r), :] = (o[:, :dv] / o[:, dv:]).astype(o_ref.dtype)

    _skewed_chunks(q_ref.shape[0] // row_chunk,
                   lambda r: _raw_scores(q_ref[rows(r), :], kc_ref, kl_ref, slice(None)), finish)


def _attention(q, kc, vc, kl, vl, *, heads, group, dqk, dv, q_col0, k_col0, v_col0):
    assert dv == LANES
    b, lq, _ = q.shape
    lc = kc.shape[1]
    tq = _divisor_tile(lq, Q_TILE, 8)
    qo, ko, vo = q_col0 // dqk, k_col0 // dqk, v_col0 // dv
    in_specs = [
        pl.BlockSpec((None, tq, dqk), lambda bi, h, i: (bi, i, qo + h)),
        pl.BlockSpec((None, lc, dqk), lambda bi, h, i: (bi, 0, ko + h // group)),
        pl.BlockSpec((None, lc, dv), lambda bi, h, i: (bi, 0, vo + h // group)),
    ]
    args = [q, kc, vc]
    if kl is not None:
        ll = kl.shape[1]
        in_specs += [
            pl.BlockSpec((None, ll, dqk), lambda bi, h, i: (bi, 0, ko + h // group)),
            pl.BlockSpec((None, ll, dv), lambda bi, h, i: (bi, 0, vo + h // group)),
        ]
        args += [kl, vl]
    return pl.pallas_call(
        functools.partial(_attn_kernel, has_lat=kl is not None, row_chunk=min(ROW_CHUNK, tq)),
        grid=(b, heads, lq // tq),
        in_specs=in_specs,
        out_specs=pl.BlockSpec((None, tq, dv), lambda bi, h, i: (bi, i, h)),
        out_shape=jax.ShapeDtypeStruct((b, lq, heads * dv), BF16),
        compiler_params=_params("parallel", "arbitrary", "arbitrary"),
        name="softmax_attention",
    )(*args)


def _diff_attn_kernel(*refs, lambda_init, has_lat, row_chunk):
    if has_lat:
        q_ref, kc_ref, vc_ref, kl_ref, vl_ref, lq1, lk1, lq2, lk2, gs_ref, o_ref = refs
    else:
        (q_ref, kc_ref, vc_ref, lq1, lk1, lq2, lk2, gs_ref, o_ref), kl_ref, vl_ref = refs, None, None
    lam = (jnp.exp(jnp.sum(lq1[...] * lk1[...], axis=-1, keepdims=True))
           - jnp.exp(jnp.sum(lq2[...] * lk2[...], axis=-1, keepdims=True)) + lambda_init)
    hd = DIFF_HEAD_DIM
    rows = lambda r: slice(r * row_chunk, (r + 1) * row_chunk)

    def scores(r):
        return (_raw_scores(q_ref[rows(r), :hd], kc_ref, kl_ref, slice(0, hd)),
                _raw_scores(q_ref[rows(r), hd:], kc_ref, kl_ref, slice(hd, 2 * hd)))

    def finish(r, s01):
        ec0, el0 = _softmax_numerators(s01[0])
        ec1, el1 = _softmax_numerators(s01[1])
        den0 = _row_sum(ec0, el0)
        ratio = lam * den0 / _row_sum(ec1, el1)
        y = _dot((ec0 - ec1 * ratio).astype(BF16), vc_ref[...])
        if has_lat:
            y = y + _dot((el0 - el1 * ratio).astype(BF16), vl_ref[...])
        y = y / den0
        o_ref[rows(r), :] = (_rms_norm(y, gs_ref[...]) * (1.0 - lambda_init)).astype(o_ref.dtype)

    _skewed_chunks(q_ref.shape[0] // row_chunk, scores, finish)


def _diff_attention(q, kc, vc, kl, vl, lams, g_sub, lambda_init):
    b, lq, _ = q.shape
    lc = kc.shape[1]
    tq = _divisor_tile(lq, Q_TILE, 8)
    w = 2 * DIFF_HEAD_DIM
    in_specs = [
        pl.BlockSpec((None, tq, w), lambda bi, h, i: (bi, i, h)),
        pl.BlockSpec((None, lc, w), lambda bi, h, i: (bi, 0, DIFF_HEADS + h)),
        pl.BlockSpec((None, lc, w), lambda bi, h, i: (bi, 0, h)),
    ]
    args = [q, kc, vc]
    if kl is not None:
        ll = kl.shape[1]
        in_specs += [
            pl.BlockSpec((None, ll, w), lambda bi, h, i: (bi, 0, DIFF_HEADS + h)),
            pl.BlockSpec((None, ll, w), lambda bi, h, i: (bi, 0, h)),
        ]
        args += [kl, vl]
    small = lambda n: pl.BlockSpec((1, n), lambda bi, h, i: (0, 0))
    in_specs += [small(DIFF_HEAD_DIM)] * 4 + [small(w)]
    args += [a.reshape(1, -1) for a in lams] + [g_sub.reshape(1, -1)]
    return pl.pallas_call(
        functools.partial(_diff_attn_kernel, lambda_init=lambda_init, has_lat=kl is not None,
                          row_chunk=min(ROW_CHUNK, tq)),
        grid=(b, DIFF_HEADS, lq // tq),
        in_specs=in_specs,
        out_specs=pl.BlockSpec((None, tq, w), lambda bi, h, i: (bi, i, h)),
        out_shape=jax.ShapeDtypeStruct((b, lq, DIFF_HEADS * w), BF16),
        compiler_params=_params("parallel", "arbitrary", "arbitrary"),
        name="diff_attention",
    )(*args)


def _outproj_kernel(*refs, nparts, alpha):
    y_refs = refs[:nparts]
    w_refs = refs[nparts:2 * nparts]
    x_ref, gate_ref, g_ref, b_ref, o_ref = refs[2 * nparts:]
    row_chunk = min(ROW_CHUNK, o_ref.shape[0])
    rows = lambda r: slice(r * row_chunk, (r + 1) * row_chunk)

    def project(r):
        y = _dot(y_refs[0][rows(r), :], w_refs[0][...])
        for p in range(1, nparts):
            y = y + _dot(y_refs[p][rows(r), :], w_refs[p][...])
        return y

    def finish(r, y):
        res = alpha * x_ref[rows(r), :] + gate_ref[...] * y
        o_ref[rows(r), :] = _layer_norm(res, g_ref[...], b_ref[...])

    _skewed_chunks(o_ref.shape[0] // row_chunk, project, finish)


def _outproj(ys, w_o, x, mod, sub, g, b, alpha):
    t, d = x.shape
    tm = _divisor_tile(t if mod.is_ctx else mod.seq, OUT_TOKEN_TILE, 8)
    kp = ys[0].shape[1]
    assert all(y.shape[1] == kp for y in ys) and kp * len(ys) == w_o.shape[0]
    vec = pl.BlockSpec((1, d), lambda i: (0, 0))
    in_specs = [pl.BlockSpec((tm, kp), lambda i: (i, 0)) for _ in ys]
    in_specs += [pl.BlockSpec((kp, d), functools.partial(lambda i, p: (p, 0), p=p), pipeline_mode=pl.Buffered(1))
                 for p in range(len(ys))]
    in_specs += [pl.BlockSpec((tm, d), lambda i: (i, 0)), mod.spec(d, sub, 2, tm), vec, vec]
    return pl.pallas_call(
        functools.partial(_outproj_kernel, nparts=len(ys), alpha=alpha),
        grid=(t // tm,),
        in_specs=in_specs,
        out_specs=pl.BlockSpec((tm, d), lambda i: (i, 0)),
        out_shape=jax.ShapeDtypeStruct((t, d), F32),
        compiler_params=_params("parallel"),
        name="mixer_out_proj",
    )(*ys, *([w_o] * len(ys)), x, mod.table, g.reshape(1, d), b.reshape(1, d))


def _rope_tables(rows, rot_dim):
    r, col = jnp.meshgrid(jnp.arange(rows, dtype=F32), jnp.arange(GRID_W, dtype=F32), indexing="ij")
    n_freq = rot_dim // 4
    inv_freq = ROPE_THETA ** (-jnp.arange(n_freq, dtype=F32) / n_freq)
    ang = jnp.concatenate([r.reshape(-1, 1) * inv_freq, col.reshape(-1, 1) * inv_freq], -1)
    cos, sin = jnp.cos(ang), jnp.sin(ang)
    pad = LANES // 2 - rot_dim // 2
    cos = jnp.pad(cos, ((0, 0), (0, pad)), constant_values=1.0)
    sin = jnp.pad(sin, ((0, 0), (0, pad)))
    return jnp.concatenate([cos, cos], -1), jnp.concatenate([-sin, sin], -1)


def _spread_rot(w):
    half = MLA_ROPE // 2
    z = jnp.zeros((w.shape[0], LANES // 2 - half), w.dtype)
    return jnp.concatenate([w[:, :half], z, w[:, half:], z], -1)


def kernel(x, c, ctx, c_ctx, w_ada, b_ada, ln_g, ln_b, ffn_w1, ffn_w3, ffn_w2, mg_w_in, mla_g_cq, mla_g_ckv,
           mla_w_uq, mla_w_ukv, gqa_g_q, gqa_g_k, mg_w_o, diff_w_in, diff_lq1, diff_lk1, diff_lq2, diff_lk2,
           diff_g_sub, diff_w_o):
    bsz, seq, d = x.shape
    lc = ctx.shape[1]
    depth = w_ada.shape[0]
    assert bsz + 1 <= MOD_ROWS and seq % GRID_W == 0
    alpha = (2 * depth) ** 0.25
    rows = seq // GRID_W
    rope_mla = _rope_tables(rows, MLA_ROPE)
    rope_hd = _rope_tables(rows, GQA_HEAD_DIM)

    cc = jnp.concatenate([c, c_ctx[None], jnp.zeros((MOD_ROWS - bsz - 1, d), F32)], 0)
    mod_table = _ada(cc, w_ada, b_ada).reshape(depth * MOD_ROWS * 3 * N_SUB, 1, d)

    x_lat = x.reshape(bsz * seq, d)
    x_ctx = ctx.reshape(bsz * lc, d)
    w1b, w3b, w2b = ffn_w1.astype(BF16), ffn_w3.astype(BF16), ffn_w2.astype(BF16)

    for i in range(depth):
        need_ctx = i < depth - 1
        mod_l = _Mod(mod_table, i, bsz, seq, False)
        mod_c = _Mod(mod_table, i, bsz, seq, True)
        ffn = lambda xs, m, sub, k: _ffn(xs, m, sub, w1b, w3b, w2b, k, ln_g[i, sub], ln_b[i, sub], alpha)

        x_lat = ffn(x_lat, mod_l, 0, 0)
        x_ctx = ffn(x_ctx, mod_c, 0, 0)

        if i % 2 == 0:
            e = i // 2
            w_in = mg_w_in[e]
            o_kr = MLA_Q_RANK + MLA_KV_RANK
            o_gq = o_kr + MLA_ROPE
            n_gqk = (GQA_HEADS + GQA_KV_HEADS) * GQA_HEAD_DIM
            w_mla = jnp.concatenate([w_in[:, :o_kr], _spread_rot(w_in[:, o_kr:o_gq])], -1).astype(BF16)
            w_gqa = w_in[:, o_gq:].astype(BF16)
            wuq = mla_w_uq[e].reshape(MLA_Q_RANK, MLA_HEADS, MLA_NOPE + MLA_ROPE)
            wuq = jnp.concatenate(
                [wuq[:, :, :MLA_NOPE],
                 _spread_rot(wuq[:, :, MLA_NOPE:].reshape(-1, MLA_ROPE)).reshape(MLA_Q_RANK, MLA_HEADS, LANES)],
                -1).reshape(MLA_Q_RANK, MLA_HEADS * 2 * LANES).astype(BF16)
            wukv = mla_w_ukv[e].reshape(MLA_KV_RANK, MLA_HEADS, MLA_NOPE + MLA_V)
            wkn = wukv[:, :, :MLA_NOPE].reshape(MLA_KV_RANK, -1).astype(BF16)
            wv = wukv[:, :, MLA_NOPE:].reshape(MLA_KV_RANK, -1).astype(BF16)
            gain = jnp.concatenate([jnp.tile(gqa_g_q[e] * (GQA_SCALE * LOG2_E), GQA_HEADS),
                                    jnp.tile(gqa_g_k[e], GQA_KV_HEADS)])[None]
            w_o = mg_w_o[e].astype(BF16)

            def project(xs, m, rp_mla, rp_hd):
                zm = _proj(xs, m, 1, w_mla, 0, w_mla.shape[1], F32)
                qk = _proj(xs, m, 1, w_gqa, 0, n_gqk, BF16, gain=gain, headnorm=True, rope=rp_hd)
                v = _proj(xs, m, 1, w_gqa, n_gqk, w_gqa.shape[1] - n_gqk, BF16)
                qa, ka, va = _mla_up(zm, seq, mla_g_cq[e], mla_g_ckv[e], wuq, wkn, wv, rp_mla,
                                     MLA_SCALE * LOG2_E)
                n = xs.shape[0] // bsz
                return [a.reshape(bsz, n, -1) for a in (qa, ka, va, qk, v)]

            qa, ka, va, qkb, vb = project(x_lat, mod_l, rope_mla, rope_hd)
            qa_c, ka_c, va_c, qkb_c, vb_c = project(x_ctx, mod_c, None, None)
            mla = dict(heads=MLA_HEADS, group=1, dqk=2 * LANES, dv=MLA_V, q_col0=0, k_col0=0, v_col0=0)
            gqa = dict(heads=GQA_HEADS, group=GQA_HEADS // GQA_KV_HEADS, dqk=GQA_HEAD_DIM, dv=GQA_HEAD_DIM,
                       q_col0=0, k_col0=GQA_HEADS * GQA_HEAD_DIM, v_col0=0)
            ya = _attention(qa, ka_c, va_c, ka, va, **mla)
            yb = _attention(qkb, qkb_c, vb_c, qkb, vb, **gqa)
            ys_lat = [ya.reshape(bsz * seq, -1), yb.reshape(bsz * seq, -1)]
            if need_ctx:
                ya_c = _attention(qa_c, ka_c, va_c, None, None, **mla)
                yb_c = _attention(qkb_c, qkb_c, vb_c, None, None, **gqa)
                ys_ctx = [ya_c.reshape(bsz * lc, -1), yb_c.reshape(bsz * lc, -1)]
        else:
            o = i // 2
            lambda_init = 0.8 - 0.6 * math.exp(-0.3 * i)
            w_in = diff_w_in[o].astype(BF16)
            n_qk = 2 * DIFF_HEADS * 2 * DIFF_HEAD_DIM
            w_o = diff_w_o[o].astype(BF16)
            lams = (diff_lq1[o], diff_lk1[o], diff_lq2[o], diff_lk2[o])
            qk_scale = jnp.concatenate([jnp.full((n_qk // 2,), DIFF_SCALE * LOG2_E, F32),
                                        jnp.ones((n_qk // 2,), F32)])[None]

            def project(xs, m, rp):
                qk = _proj(xs, m, 1, w_in, 0, n_qk, BF16, gain=qk_scale, rope=rp)
                v = _proj(xs, m, 1, w_in, n_qk, w_in.shape[1] - n_qk, BF16)
                n = xs.shape[0] // bsz
                return qk.reshape(bsz, n, -1), v.reshape(bsz, n, -1)

            qk, v = project(x_lat, mod_l, rope_hd)
            qk_c, v_c = project(x_ctx, mod_c, None)
            y = _diff_attention(qk, qk_c, v_c, qk, v, lams, diff_g_sub[o], lambda_init)
            ys_lat = [y.reshape(bsz * seq, -1)]
            if need_ctx:
                y_c = _diff_attention(qk_c, qk_c, v_c, None, None, lams, diff_g_sub[o], lambda_init)
                ys_ctx = [y_c.reshape(bsz * lc, -1)]

        x_lat = _outproj(ys_lat, w_o, x_lat, mod_l, 1, ln_g[i, 1], ln_b[i, 1], alpha)
        x_lat = ffn(x_lat, mod_l, 2, 1)
        if need_ctx:
            x_ctx = _outproj(ys_ctx, w_o, x_ctx, mod_c, 1, ln_g[i, 1], ln_b[i, 1], alpha)
            x_ctx = ffn(x_ctx, mod_c, 2, 1)
    return x_lat.reshape(bsz, seq, d)
```

```python
import functools
import math

import jax
import jax.numpy as jnp
from jax import lax
from jax.experimental import pallas as pl
from jax.experimental.pallas import tpu as pltpu

F32 = jnp.float32
BF16 = jnp.bfloat16

GRID_W = 64
ROPE_THETA = 10000.0
NORM_EPS = 1e-6
N_SUB = 3
HALF_STEP = 0.5
MLA_HEADS = 8
MLA_Q_RANK = 512
MLA_KV_RANK = 256
MLA_NOPE = 128
MLA_ROPE = 64
MLA_V = 128
MLA_SCALE = (MLA_NOPE + MLA_ROPE) ** -0.5
GQA_HEADS = 8
GQA_KV_HEADS = 2
GQA_HEAD_DIM = 128
GQA_SCALE = GQA_HEAD_DIM ** -0.5
DIFF_HEADS = 8
DIFF_HEAD_DIM = 128
DIFF_SCALE = DIFF_HEAD_DIM ** -0.5

LANES = 128
MOD_ROWS = 32
VMEM_LIMIT_BYTES = 56 * 1024 * 1024

TOKEN_TILE = 1024
OUT_TOKEN_TILE = 512
FF_TILE = 512
PROJ_TILE = 1536
ADA_TILE = 1024
Q_TILE = 2048
ROW_CHUNK = 256
COL_CHUNK = 512
LOG2_E = math.log2(math.e)


def _divisor_tile(n, pref, mult):
    if n <= pref:
        return n
    t = (pref // mult) * mult
    while t >= mult:
        if n % t == 0:
            return t
        t -= mult
    raise ValueError(f"no tile for {n} (pref {pref}, mult {mult})")


def _params(*sem):
    return pltpu.CompilerParams(dimension_semantics=sem, vmem_limit_bytes=VMEM_LIMIT_BYTES)


def _layer_norm(r, g, b):
    mu = jnp.mean(r, axis=-1, keepdims=True)
    d = r - mu
    var = jnp.mean(d * d, axis=-1, keepdims=True)
    return d * lax.rsqrt(var + NORM_EPS) * g + b


def _rms_norm(x, g):
    return x * lax.rsqrt(jnp.mean(x * x, axis=-1, keepdims=True) + NORM_EPS) * g


def _rope_block(blk, cos, sin):
    return blk * cos + pltpu.roll(blk, LANES // 2, 1) * sin


def _skewed_chunks(n, matmul_fn, finish_fn):
    nxt = matmul_fn(0)
    for c in range(n):
        cur, nxt = nxt, (matmul_fn(c + 1) if c + 1 < n else None)
        finish_fn(c, cur)


def _dot(a, b):
    return jnp.dot(a, b, preferred_element_type=F32)


def _dot_nt(a, b):
    return lax.dot_general(a, b, (((1,), (1,)), ((), ())), preferred_element_type=F32)


def _ada_kernel(c_ref, w_ref, b_ref, o_ref):
    c = c_ref[...]
    s = (c * jax.nn.sigmoid(c)).astype(BF16)
    o_ref[...] = _dot(s, w_ref[...].astype(BF16)) + b_ref[...]


def _ada(cc, w_ada, b_ada):
    depth, d, n = w_ada.shape
    tn = _divisor_tile(n, ADA_TILE, LANES)
    return pl.pallas_call(
        _ada_kernel,
        grid=(depth, n // tn),
        in_specs=[
            pl.BlockSpec((MOD_ROWS, d), lambda l, j: (0, 0)),
            pl.BlockSpec((None, d, tn), lambda l, j: (l, 0, j)),
            pl.BlockSpec((None, 1, tn), lambda l, j: (l, 0, j)),
        ],
        out_specs=pl.BlockSpec((None, MOD_ROWS, tn), lambda l, j: (l, 0, j)),
        out_shape=jax.ShapeDtypeStruct((depth, MOD_ROWS, n), F32),
        compiler_params=_params("arbitrary", "arbitrary"),
        name="ada_mod",
    )(cc, w_ada, b_ada.reshape(depth, 1, n))


class _Mod:
    def __init__(self, table, layer, batch, seq, is_ctx):
        self.table, self.layer, self.batch, self.seq, self.is_ctx = table, layer, batch, seq, is_ctx

    def spec(self, d, sub, kind, tm):
        base = self.layer * MOD_ROWS
        off = 3 * sub + kind
        if self.is_ctx:
            row = (base + self.batch) * 3 * N_SUB + off
            return pl.BlockSpec((None, 1, d), lambda i, *_: (row, 0, 0))
        tpb = self.seq // tm
        return pl.BlockSpec((None, 1, d), lambda i, *_: ((base + i // tpb) * 3 * N_SUB + off, 0, 0))


def _ffn_kernel(x_ref, shift_ref, scale_ref, gate_ref, w1_ref, w3_ref, w2_ref, g_ref, b_ref, o_ref,
                hb_ref, *, alpha, nf):
    f = pl.program_id(1)
    tm = x_ref.shape[0]

    def modulate(rows):
        hb_ref[rows, :] = (x_ref[rows, :] * (1.0 + scale_ref[...]) + shift_ref[...]).astype(BF16)

    def swiglu_chunk(rows):
        hb = hb_ref[rows, :]
        a = _dot(hb, w1_ref[...])
        u = _dot(hb, w3_ref[...])
        act = (a * jax.nn.sigmoid(a) * u).astype(BF16)
        return _dot(act, w2_ref[...])

    def post_norm(rows, y):
        r = alpha * x_ref[rows, :] + gate_ref[...] * (HALF_STEP * y)
        o_ref[rows, :] = _layer_norm(r, g_ref[...], b_ref[...])

    def step(first, last):
        rc = min(ROW_CHUNK, tm) if (first or last) else tm
        rows = lambda r: slice(r * rc, (r + 1) * rc)
        n = tm // rc
        if first:
            modulate(rows(0))

        def matmuls(r):
            if first and r + 1 < n:
                modulate(rows(r + 1))
            y = swiglu_chunk(rows(r))
            return y if first else o_ref[rows(r), :] + y

        def finish(r, y):
            if last:
                post_norm(rows(r), y)
            else:
                o_ref[rows(r), :] = y

        _skewed_chunks(n, matmuls, finish)

    if nf == 1:
        step(True, True)
        return
    pl.when(f == 0)(lambda: step(True, False))
    pl.when(f == nf - 1)(lambda: step(False, True))
    if nf > 2:
        pl.when(jnp.logical_and(f > 0, f < nf - 1))(lambda: step(False, False))


def _ffn(x, mod, sub, w1, w3, w2, half, g, b, alpha):
    t, d = x.shape
    f = w1.shape[-1]
    layer = mod.layer
    tm = _divisor_tile(t if mod.is_ctx else mod.seq, TOKEN_TILE, 8)
    tf = _divisor_tile(f, FF_TILE, LANES)
    nf = f // tf
    row = pl.BlockSpec((tm, d), lambda i, k: (i, 0))
    vec = pl.BlockSpec((1, d), lambda i, k: (0, 0))
    return pl.pallas_call(
        functools.partial(_ffn_kernel, alpha=alpha, nf=nf),
        grid=(t // tm, nf),
        in_specs=[
            row,
            mod.spec(d, sub, 0, tm), mod.spec(d, sub, 1, tm), mod.spec(d, sub, 2, tm),
            pl.BlockSpec((None, None, d, tf), lambda i, k: (layer, half, 0, k)),
            pl.BlockSpec((None, None, d, tf), lambda i, k: (layer, half, 0, k)),
            pl.BlockSpec((None, None, tf, d), lambda i, k: (layer, half, k, 0)),
            vec, vec,
        ],
        out_specs=row,
        out_shape=jax.ShapeDtypeStruct((t, d), F32),
        scratch_shapes=[pltpu.VMEM((tm, d), BF16)],
        compiler_params=_params("parallel", "arbitrary"),
        name="ffn_half_step",
    )(x, mod.table, mod.table, mod.table, w1, w3, w2, g.reshape(1, d), b.reshape(1, d))


def _proj_kernel(*refs, headnorm, colscale, rope, plain_tail):
    x_ref, shift_ref, scale_ref, w_ref = refs[:4]
    rest = list(refs[4:])
    gain_ref = rest.pop(0) if (headnorm or colscale) else None
    cos_ref, sin_ref = (rest.pop(0), rest.pop(0)) if rope else (None, None)
    o_ref, hb_ref = rest

    @pl.when(pl.program_id(1) == 0)
    def _():
        hb_ref[...] = (x_ref[...] * (1.0 + scale_ref[...]) + shift_ref[...]).astype(BF16)

    if not (headnorm or colscale or rope):
        o_ref[...] = _dot(hb_ref[...], w_ref[...]).astype(o_ref.dtype)
        return
    tn = o_ref.shape[1]
    cw = next(w for w in (COL_CHUNK, COL_CHUNK // 2, LANES) if tn % w == 0)

    def finish(c, z):
        col0 = [c * cw + kb * LANES for kb in range(cw // LANES)]
        hw = z[0].shape[1]
        blks = [z[(kb * LANES) // hw][:, (kb * LANES) % hw:(kb * LANES) % hw + LANES] for kb in range(cw // LANES)]
        live = [s < tn - plain_tail for s in col0]
        if headnorm:
            ms = [jnp.mean(b * b, axis=-1, keepdims=True) if on else None for b, on in zip(blks, live)]
            rs = [lax.rsqrt(m + NORM_EPS) if on else None for m, on in zip(ms, live)]
            blks = [b * r * gain_ref[:, s:s + LANES] if on else b for b, r, s, on in zip(blks, rs, col0, live)]
        elif colscale:
            blks = [b * gain_ref[:, s:s + LANES] if on else b for b, s, on in zip(blks, col0, live)]
        if rope:
            rolled = [pltpu.roll(b, LANES // 2, 1) if on else None for b, on in zip(blks, live)]
            blks = [b * cos_ref[...] + r * sin_ref[...] if on else b for b, r, on in zip(blks, rolled, live)]
        for b, s in zip(blks, col0):
            o_ref[:, s:s + LANES] = b.astype(o_ref.dtype)

    nh = 2 if cw >= 2 * LANES else 1
    hw = cw // nh
    _skewed_chunks(
        tn // cw,
        lambda c: [_dot(hb_ref[...], w_ref[:, c * cw + h * hw:c * cw + (h + 1) * hw]) for h in range(nh)],
        finish)


def _proj(x, mod, sub, w, col0, ncols, out_dtype, gain=None, headnorm=False, rope=None, plain_tail=0):
    t, d = x.shape
    tm = _divisor_tile(t if mod.is_ctx else mod.seq, TOKEN_TILE, 8)
    tn = _divisor_tile(ncols, PROJ_TILE, LANES)
    assert col0 % tn == 0 and (plain_tail == 0 or tn == ncols)
    joff = col0 // tn
    in_specs = [
        pl.BlockSpec((tm, d), lambda i, j: (i, 0)),
        mod.spec(d, sub, 0, tm), mod.spec(d, sub, 1, tm),
        pl.BlockSpec((d, tn), lambda i, j: (0, j + joff)),
    ]
    args = [x, mod.table, mod.table, w]
    if gain is not None:
        in_specs.append(pl.BlockSpec((1, tn), lambda i, j: (0, j)))
        args.append(gain)
    if rope is not None:
        tpb = mod.seq // tm
        tab = pl.BlockSpec((tm, LANES), lambda i, j: (i % tpb, 0))
        in_specs += [tab, tab]
        args += list(rope)
    return pl.pallas_call(
        functools.partial(_proj_kernel, headnorm=headnorm, colscale=gain is not None and not headnorm,
                          rope=rope is not None, plain_tail=plain_tail),
        grid=(t // tm, ncols // tn),
        in_specs=in_specs,
        out_specs=pl.BlockSpec((tm, tn), lambda i, j: (i, j)),
        out_shape=jax.ShapeDtypeStruct((t, ncols), out_dtype),
        scratch_shapes=[pltpu.VMEM((tm, d), BF16)],
        compiler_params=_params("parallel", "arbitrary"),
        name="mixer_in_proj",
    )(*args)


def _mla_up_kernel(*refs, rope, qscale):
    zm_ref, gcq_ref, gckv_ref, wuq_ref, wkn_ref, wv_ref = refs[:6]
    rest = list(refs[6:])
    cos_ref, sin_ref = (rest.pop(0), rest.pop(0)) if rope else (None, None)
    qa_ref, ka_ref, va_ref = rest
    hd = 2 * LANES

    def rot(blk):
        return _rope_block(blk, cos_ref[...], sin_ref[...]) if rope else blk

    nq = _rms_norm(zm_ref[:, :MLA_Q_RANK], gcq_ref[...]).astype(BF16)
    q = _dot(nq, wuq_ref[...]) * qscale
    for h in range(MLA_HEADS):
        qa_ref[:, h * hd:h * hd + LANES] = q[:, h * hd:h * hd + LANES].astype(BF16)
        qa_ref[:, h * hd + LANES:(h + 1) * hd] = rot(q[:, h * hd + LANES:(h + 1) * hd]).astype(BF16)

    nkv = _rms_norm(zm_ref[:, MLA_Q_RANK:MLA_Q_RANK + MLA_KV_RANK], gckv_ref[...]).astype(BF16)
    kn = _dot(nkv, wkn_ref[...])
    va_ref[...] = _dot(nkv, wv_ref[...]).astype(BF16)
    kr = rot(zm_ref[:, MLA_Q_RANK + MLA_KV_RANK:MLA_Q_RANK + MLA_KV_RANK + LANES]).astype(BF16)
    for h in range(MLA_HEADS):
        ka_ref[:, h * hd:h * hd + LANES] = kn[:, h * LANES:(h + 1) * LANES].astype(BF16)
        ka_ref[:, h * hd + LANES:(h + 1) * hd] = kr


def _mla_up(zm, seq, g_cq, g_ckv, wuq, wkn, wv, rope, qscale):
    t, nz = zm.shape
    tm = _divisor_tile(seq if rope is not None else t, TOKEN_TILE, 8)
    hq = MLA_HEADS * 2 * LANES
    hv = MLA_HEADS * MLA_V
    full = lambda a: pl.BlockSpec(a.shape, lambda i: (0, 0))
    g_cq = g_cq.reshape(1, -1)
    g_ckv = g_ckv.reshape(1, -1)
    in_specs = [pl.BlockSpec((tm, nz), lambda i: (i, 0)), full(g_cq), full(g_ckv), full(wuq), full(wkn), full(wv)]
    args = [zm, g_cq, g_ckv, wuq, wkn, wv]
    if rope is not None:
        tpb = seq // tm
        tab = pl.BlockSpec((tm, LANES), lambda i: (i % tpb, 0))
        in_specs += [tab, tab]
        args += list(rope)
    return pl.pallas_call(
        functools.partial(_mla_up_kernel, rope=rope is not None, qscale=qscale),
        grid=(t // tm,),
        in_specs=in_specs,
        out_specs=[pl.BlockSpec((tm, hq), lambda i: (i, 0)), pl.BlockSpec((tm, hq), lambda i: (i, 0)),
                   pl.BlockSpec((tm, hv), lambda i: (i, 0))],
        out_shape=[jax.ShapeDtypeStruct((t, hq), BF16), jax.ShapeDtypeStruct((t, hq), BF16),
                   jax.ShapeDtypeStruct((t, hv), BF16)],
        compiler_params=_params("parallel"),
        name="mla_up_proj",
    )(*args)


def _raw_scores(q, kc_ref, kl_ref, kcols):
    sc = _dot_nt(q, kc_ref[:, kcols])
    sl = None if kl_ref is None else _dot_nt(q, kl_ref[:, kcols])
    return sc, sl


def _softmax_numerators(scores):
    sc, sl = scores
    m = jnp.max(sc, axis=-1, keepdims=True)
    if sl is None:
        return jnp.exp2(sc - m), None
    m = jnp.maximum(m, jnp.max(sl, axis=-1, keepdims=True))
    return jnp.exp2(sc - m), jnp.exp2(sl - m)


def _row_sum(ec, el):
    den = jnp.sum(ec, axis=-1, keepdims=True)
    return den if el is None else den + jnp.sum(el, axis=-1, keepdims=True)


def _attn_kernel(*refs, has_lat, row_chunk):
    if has_lat:
        q_ref, kc_ref, vc_ref, kl_ref, vl_ref, o_ref = refs
    else:
        (q_ref, kc_ref, vc_ref, o_ref), kl_ref, vl_ref = refs, None, None
    rows = lambda r: slice(r * row_chunk, (r + 1) * row_chunk)
    dv = vc_ref.shape[1]
    with_ones = lambda v: jnp.concatenate([v, jnp.ones_like(v)], axis=1)
    vc1 = with_ones(vc_ref[...])
    vl1 = with_ones(vl_ref[...]) if has_lat else None

    def finish(r, scores):
        ec, el = _softmax_numerators(scores)
        o = _dot(ec.astype(BF16), vc1)
        if has_lat:
            o = o + _dot(el.astype(BF16), vl1)
        o_ref[rows(r), :] = (o[:, :dv] / o[:, dv:]).astype(o_ref.dtype)

    _skewed_chunks(q_ref.shape[0] // row_chunk,
                   lambda r: _raw_scores(q_ref[rows(r), :], kc_ref, kl_ref, slice(None)), finish)


def _attention(q, kc, vc, kl, vl, *, heads, group, dqk, dv, q_col0, k_col0, v_col0):
    assert dv == LANES
    b, lq, _ = q.shape
    lc = kc.shape[1]
    tq = _divisor_tile(lq, Q_TILE, 8)
    qo, ko, vo = q_col0 // dqk, k_col0 // dqk, v_col0 // dv
    in_specs = [
        pl.BlockSpec((None, tq, dqk), lambda bi, h, i: (bi, i, qo + h)),
        pl.BlockSpec((None, lc, dqk), lambda bi, h, i: (bi, 0, ko + h // group)),
        pl.BlockSpec((None, lc, dv), lambda bi, h, i: (bi, 0, vo + h // group)),
    ]
    args = [q, kc, vc]
    if kl is not None:
        ll = kl.shape[1]
        in_specs += [
            pl.BlockSpec((None, ll, dqk), lambda bi, h, i: (bi, 0, ko + h // group)),
            pl.BlockSpec((None, ll, dv), lambda bi, h, i: (bi, 0, vo + h // group)),
        ]
        args += [kl, vl]
    return pl.pallas_call(
        functools.partial(_attn_kernel, has_lat=kl is not None, row_chunk=min(ROW_CHUNK, tq)),
        grid=(b, heads, lq // tq),
        in_specs=in_specs,
        out_specs=pl.BlockSpec((None, tq, dv), lambda bi, h, i: (bi, i, h)),
        out_shape=jax.ShapeDtypeStruct((b, lq, heads * dv), BF16),
        compiler_params=_params("parallel", "arbitrary", "arbitrary"),
        name="softmax_attention",
    )(*args)


def _diff_attn_kernel(*refs, lambda_init, has_lat, row_chunk):
    if has_lat:
        q_ref, kc_ref, vc_ref, kl_ref, vl_ref, lq1, lk1, lq2, lk2, gs_ref, o_ref = refs
    else:
        (q_ref, kc_ref, vc_ref, lq1, lk1, lq2, lk2, gs_ref, o_ref), kl_ref, vl_ref = refs, None, None
    lam = (jnp.exp(jnp.sum(lq1[...] * lk1[...], axis=-1, keepdims=True))
           - jnp.exp(jnp.sum(lq2[...] * lk2[...], axis=-1, keepdims=True)) + lambda_init)
    hd = DIFF_HEAD_DIM
    rows = lambda r: slice(r * row_chunk, (r + 1) * row_chunk)

    def scores(r):
        return (_raw_scores(q_ref[rows(r), :hd], kc_ref, kl_ref, slice(0, hd)),
                _raw_scores(q_ref[rows(r), hd:], kc_ref, kl_ref, slice(hd, 2 * hd)))

    def finish(r, s01):
        ec0, el0 = _softmax_numerators(s01[0])
        ec1, el1 = _softmax_numerators(s01[1])
        den0 = _row_sum(ec0, el0)
        ratio = lam * den0 / _row_sum(ec1, el1)
        y = _dot((ec0 - ec1 * ratio).astype(BF16), vc_ref[...])
        if has_lat:
            y = y + _dot((el0 - el1 * ratio).astype(BF16), vl_ref[...])
        y = y / den0
        o_ref[rows(r), :] = (_rms_norm(y, gs_ref[...]) * (1.0 - lambda_init)).astype(o_ref.dtype)

    _skewed_chunks(q_ref.shape[0] // row_chunk, scores, finish)


def _diff_attention(q, kc, vc, kl, vl, lams, g_sub, lambda_init):
    b, lq, _ = q.shape
    lc = kc.shape[1]
    tq = _divisor_tile(lq, Q_TILE, 8)
    w = 2 * DIFF_HEAD_DIM
    in_specs = [
        pl.BlockSpec((None, tq, w), lambda bi, h, i: (bi, i, h)),
        pl.BlockSpec((None, lc, w), lambda bi, h, i: (bi, 0, DIFF_HEADS + h)),
        pl.BlockSpec((None, lc, w), lambda bi, h, i: (bi, 0, h)),
    ]
    args = [q, kc, vc]
    if kl is not None:
        ll = kl.shape[1]
        in_specs += [
            pl.BlockSpec((None, ll, w), lambda bi, h, i: (bi, 0, DIFF_HEADS + h)),
            pl.BlockSpec((None, ll, w), lambda bi, h, i: (bi, 0, h)),
        ]
        args += [kl, vl]
    small = lambda n: pl.BlockSpec((1, n), lambda bi, h, i: (0, 0))
    in_specs += [small(DIFF_HEAD_DIM)] * 4 + [small(w)]
    args += [a.reshape(1, -1) for a in lams] + [g_sub.reshape(1, -1)]
    return pl.pallas_call(
        functools.partial(_diff_attn_kernel, lambda_init=lambda_init, has_lat=kl is not None,
                          row_chunk=min(ROW_CHUNK, tq)),
        grid=(b, DIFF_HEADS, lq // tq),
        in_specs=in_specs,
        out_specs=pl.BlockSpec((None, tq, w), lambda bi, h, i: (bi, i, h)),
        out_shape=jax.ShapeDtypeStruct((b, lq, DIFF_HEADS * w), BF16),
        compiler_params=_params("parallel", "arbitrary", "arbitrary"),
        name="diff_attention",
    )(*args)


def _outproj_kernel(*refs, nparts, alpha):
    y_refs = refs[:nparts]
    w_refs = refs[nparts:2 * nparts]
    x_ref, gate_ref, g_ref, b_ref, o_ref = refs[2 * nparts:]
    row_chunk = min(ROW_CHUNK, o_ref.shape[0])
    rows = lambda r: slice(r * row_chunk, (r + 1) * row_chunk)

    def project(r):
        y = _dot(y_refs[0][rows(r), :], w_refs[0][...])
        for p in range(1, nparts):
            y = y + _dot(y_refs[p][rows(r), :], w_refs[p][...])
        return y

    def finish(r, y):
        res = alpha * x_ref[rows(r), :] + gate_ref[...] * y
        o_ref[rows(r), :] = _layer_norm(res, g_ref[...], b_ref[...])

    _skewed_chunks(o_ref.shape[0] // row_chunk, project, finish)


def _outproj(ys, w_o, x, mod, sub, g, b, alpha):
    t, d = x.shape
    tm = _divisor_tile(t if mod.is_ctx else mod.seq, OUT_TOKEN_TILE, 8)
    kp = ys[0].shape[1]
    assert all(y.shape[1] == kp for y in ys) and kp * len(ys) == w_o.shape[0]
    vec = pl.BlockSpec((1, d), lambda i: (0, 0))
    in_specs = [pl.BlockSpec((tm, kp), lambda i: (i, 0)) for _ in ys]
    in_specs += [pl.BlockSpec((kp, d), functools.partial(lambda i, p: (p, 0), p=p), pipeline_mode=pl.Buffered(1))
                 for p in range(len(ys))]
    in_specs += [pl.BlockSpec((tm, d), lambda i: (i, 0)), mod.spec(d, sub, 2, tm), vec, vec]
    return pl.pallas_call(
        functools.partial(_outproj_kernel, nparts=len(ys), alpha=alpha),
        grid=(t // tm,),
        in_specs=in_specs,
        out_specs=pl.BlockSpec((tm, d), lambda i: (i, 0)),
        out_shape=jax.ShapeDtypeStruct((t, d), F32),
        compiler_params=_params("parallel"),
        name="mixer_out_proj",
    )(*ys, *([w_o] * len(ys)), x, mod.table, g.reshape(1, d), b.reshape(1, d))


def _rope_tables(rows, rot_dim):
    r, col = jnp.meshgrid(jnp.arange(rows, dtype=F32), jnp.arange(GRID_W, dtype=F32), indexing="ij")
    n_freq = rot_dim // 4
    inv_freq = ROPE_THETA ** (-jnp.arange(n_freq, dtype=F32) / n_freq)
    ang = jnp.concatenate([r.reshape(-1, 1) * inv_freq, col.reshape(-1, 1) * inv_freq], -1)
    cos, sin = jnp.cos(ang), jnp.sin(ang)
    pad = LANES // 2 - rot_dim // 2
    cos = jnp.pad(cos, ((0, 0), (0, pad)), constant_values=1.0)
    sin = jnp.pad(sin, ((0, 0), (0, pad)))
    return jnp.concatenate([cos, cos], -1), jnp.concatenate([-sin, sin], -1)


def _spread_rot(w):
    half = MLA_ROPE // 2
    z = jnp.zeros((w.shape[0], LANES // 2 - half), w.dtype)
    return jnp.concatenate([w[:, :half], z, w[:, half:], z], -1)


def kernel(x, c, ctx, c_ctx, w_ada, b_ada, ln_g, ln_b, ffn_w1, ffn_w3, ffn_w2, mg_w_in, mla_g_cq, mla_g_ckv,
           mla_w_uq, mla_w_ukv, gqa_g_q, gqa_g_k, mg_w_o, diff_w_in, diff_lq1, diff_lk1, diff_lq2, diff_lk2,
           diff_g_sub, diff_w_o):
    bsz, seq, d = x.shape
    lc = ctx.shape[1]
    depth = w_ada.shape[0]
    assert bsz + 1 <= MOD_ROWS and seq % GRID_W == 0
    alpha = (2 * depth) ** 0.25
    rows = seq // GRID_W
    rope_mla = _rope_tables(rows, MLA_ROPE)
    rope_hd = _rope_tables(rows, GQA_HEAD_DIM)

    cc = jnp.concatenate([c, c_ctx[None], jnp.zeros((MOD_ROWS - bsz - 1, d), F32)], 0)
    mod_table = _ada(cc, w_ada, b_ada).reshape(depth * MOD_ROWS * 3 * N_SUB, 1, d)

    x_lat = x.reshape(bsz * seq, d)
    x_ctx = ctx.reshape(bsz * lc, d)
    w1b, w3b, w2b = ffn_w1.astype(BF16), ffn_w3.astype(BF16), ffn_w2.astype(BF16)

    for i in range(depth):
        need_ctx = i < depth - 1
        mod_l = _Mod(mod_table, i, bsz, seq, False)
        mod_c = _Mod(mod_table, i, bsz, seq, True)
        ffn = lambda xs, m, sub, k: _ffn(xs, m, sub, w1b, w3b, w2b, k, ln_g[i, sub], ln_b[i, sub], alpha)

        x_lat = ffn(x_lat, mod_l, 0, 0)
        x_ctx = ffn(x_ctx, mod_c, 0, 0)

        if i % 2 == 0:
            e = i // 2
            w_in = mg_w_in[e]
            o_kr = MLA_Q_RANK + MLA_KV_RANK
            o_gq = o_kr + MLA_ROPE
            n_gqk = (GQA_HEADS + GQA_KV_HEADS) * GQA_HEAD_DIM
            w_mla = jnp.concatenate([w_in[:, :o_kr], _spread_rot(w_in[:, o_kr:o_gq])], -1).astype(BF16)
            w_gqa = w_in[:, o_gq:].astype(BF16)
            wuq = mla_w_uq[e].reshape(MLA_Q_RANK, MLA_HEADS, MLA_NOPE + MLA_ROPE)
            wuq = jnp.concatenate(
                [wuq[:, :, :MLA_NOPE],
                 _spread_rot(wuq[:, :, MLA_NOPE:].reshape(-1, MLA_ROPE)).reshape(MLA_Q_RANK, MLA_HEADS, LANES)],
                -1).reshape(MLA_Q_RANK, MLA_HEADS * 2 * LANES).astype(BF16)
            wukv = mla_w_ukv[e].reshape(MLA_KV_RANK, MLA_HEADS, MLA_NOPE + MLA_V)
            wkn = wukv[:, :, :MLA_NOPE].reshape(MLA_KV_RANK, -1).astype(BF16)
            wv = wukv[:, :, MLA_NOPE:].reshape(MLA_KV_RANK, -1).astype(BF16)
            n_gv = GQA_KV_HEADS * GQA_HEAD_DIM
            gain = jnp.concatenate([jnp.tile(gqa_g_q[e] * (GQA_SCALE * LOG2_E), GQA_HEADS),
                                    jnp.tile(gqa_g_k[e], GQA_KV_HEADS), jnp.ones((n_gv,), F32)])[None]
            w_o = mg_w_o[e].astype(BF16)

            def project(xs, m, rp_mla, rp_hd):
                zm = _proj(xs, m, 1, w_mla, 0, w_mla.shape[1], F32)
                qkv = _proj(xs, m, 1, w_gqa, 0, n_gqk + n_gv, BF16, gain=gain, headnorm=True, rope=rp_hd,
                            plain_tail=n_gv)
                qa, ka, va = _mla_up(zm, seq, mla_g_cq[e], mla_g_ckv[e], wuq, wkn, wv, rp_mla,
                                     MLA_SCALE * LOG2_E)
                n = xs.shape[0] // bsz
                return [a.reshape(bsz, n, -1) for a in (qa, ka, va, qkv)]

            qa, ka, va, qkv = project(x_lat, mod_l, rope_mla, rope_hd)
            qa_c, ka_c, va_c, qkv_c = project(x_ctx, mod_c, None, None)
            mla = dict(heads=MLA_HEADS, group=1, dqk=2 * LANES, dv=MLA_V, q_col0=0, k_col0=0, v_col0=0)
            gqa = dict(heads=GQA_HEADS, group=GQA_HEADS // GQA_KV_HEADS, dqk=GQA_HEAD_DIM, dv=GQA_HEAD_DIM,
                       q_col0=0, k_col0=GQA_HEADS * GQA_HEAD_DIM, v_col0=n_gqk)
            ya = _attention(qa, ka_c, va_c, ka, va, **mla)
            yb = _attention(qkv, qkv_c, qkv_c, qkv, qkv, **gqa)
            ys_lat = [ya.reshape(bsz * seq, -1), yb.reshape(bsz * seq, -1)]
            if need_ctx:
                ya_c = _attention(qa_c, ka_c, va_c, None, None, **mla)
                yb_c = _attention(qkv_c, qkv_c, qkv_c, None, None, **gqa)
                ys_ctx = [ya_c.reshape(bsz * lc, -1), yb_c.reshape(bsz * lc, -1)]
        else:
            o = i // 2
            lambda_init = 0.8 - 0.6 * math.exp(-0.3 * i)
            w_in = diff_w_in[o].astype(BF16)
            n_qk = 2 * DIFF_HEADS * 2 * DIFF_HEAD_DIM
            w_o = diff_w_o[o].astype(BF16)
            lams = (diff_lq1[o], diff_lk1[o], diff_lq2[o], diff_lk2[o])
            qk_scale = jnp.concatenate([jnp.full((n_qk // 2,), DIFF_SCALE * LOG2_E, F32),
                                        jnp.ones((n_qk // 2,), F32)])[None]

            def project(xs, m, rp):
                qk = _proj(xs, m, 1, w_in, 0, n_qk, BF16, gain=qk_scale, rope=rp)
                v = _proj(xs, m, 1, w_in, n_qk, w_in.shape[1] - n_qk, BF16)
                n = xs.shape[0] // bsz
                return qk.reshape(bsz, n, -1), v.reshape(bsz, n, -1)

            qk, v = project(x_lat, mod_l, rope_hd)
            qk_c, v_c = project(x_ctx, mod_c, None)
            y = _diff_attention(qk, qk_c, v_c, qk, v, lams, diff_g_sub[o], lambda_init)
            ys_lat = [y.reshape(bsz * seq, -1)]
            if need_ctx:
                y_c = _diff_attention(qk_c, qk_c, v_c, None, None, lams, diff_g_sub[o], lambda_init)
                ys_ctx = [y_c.reshape(bsz * lc, -1)]

        x_lat = _outproj(ys_lat, w_o, x_lat, mod_l, 1, ln_g[i, 1], ln_b[i, 1], alpha)
        x_lat = ffn(x_lat, mod_l, 2, 1)
        if need_ctx:
            x_ctx = _outproj(ys_ctx, w_o, x_ctx, mod_c, 1, ln_g[i, 1], ln_b[i, 1], alpha)
            x_ctx = ffn(x_ctx, mod_c, 2, 1)
    return x_lat.reshape(bsz, seq, d)
```

```python
import functools
import math

import jax
import jax.numpy as jnp
from jax import lax
from jax.experimental import pallas as pl
from jax.experimental.pallas import tpu as pltpu

F32 = jnp.float32
BF16 = jnp.bfloat16

GRID_W = 64
ROPE_THETA = 10000.0
NORM_EPS = 1e-6
N_SUB = 3
HALF_STEP = 0.5
MLA_HEADS = 8
MLA_Q_RANK = 512
MLA_KV_RANK = 256
MLA_NOPE = 128
MLA_ROPE = 64
MLA_V = 128
MLA_SCALE = (MLA_NOPE + MLA_ROPE) ** -0.5
GQA_HEADS = 8
GQA_KV_HEADS = 2
GQA_HEAD_DIM = 128
GQA_SCALE = GQA_HEAD_DIM ** -0.5
DIFF_HEADS = 8
DIFF_HEAD_DIM = 128
DIFF_SCALE = DIFF_HEAD_DIM ** -0.5

LANES = 128
MOD_ROWS = 32
VMEM_LIMIT_BYTES = 56 * 1024 * 1024

TOKEN_TILE = 1024
OUT_TOKEN_TILE = 512
FF_TILE = 512
PROJ_TILE = 1536
ADA_TILE = 1024
Q_TILE = 2048
ROW_CHUNK = 256
COL_CHUNK = 512
LOG2_E = math.log2(math.e)


def _divisor_tile(n, pref, mult):
    if n <= pref:
        return n
    t = (pref // mult) * mult
    while t >= mult:
        if n % t == 0:
            return t
        t -= mult
    raise ValueError(f"no tile for {n} (pref {pref}, mult {mult})")


def _params(*sem):
    return pltpu.CompilerParams(dimension_semantics=sem, vmem_limit_bytes=VMEM_LIMIT_BYTES)


def _layer_norm(r, g, b):
    mu = jnp.mean(r, axis=-1, keepdims=True)
    d = r - mu
    var = jnp.mean(d * d, axis=-1, keepdims=True)
    return d * lax.rsqrt(var + NORM_EPS) * g + b


def _rms_norm(x, g):
    return x * lax.rsqrt(jnp.mean(x * x, axis=-1, keepdims=True) + NORM_EPS) * g


def _rope_block(blk, cos, sin):
    return blk * cos + pltpu.roll(blk, LANES // 2, 1) * sin


def _skewed_chunks(n, matmul_fn, finish_fn):
    nxt = matmul_fn(0)
    for c in range(n):
        cur, nxt = nxt, (matmul_fn(c + 1) if c + 1 < n else None)
        finish_fn(c, cur)


def _dot(a, b):
    return jnp.dot(a, b, preferred_element_type=F32)


def _dot_nt(a, b):
    return lax.dot_general(a, b, (((1,), (1,)), ((), ())), preferred_element_type=F32)


def _ada_kernel(c_ref, w_ref, b_ref, o_ref):
    c = c_ref[...]
    s = (c * jax.nn.sigmoid(c)).astype(BF16)
    o_ref[...] = _dot(s, w_ref[...].astype(BF16)) + b_ref[...]


def _ada(cc, w_ada, b_ada):
    depth, d, n = w_ada.shape
    tn = _divisor_tile(n, ADA_TILE, LANES)
    return pl.pallas_call(
        _ada_kernel,
        grid=(depth, n // tn),
        in_specs=[
            pl.BlockSpec((MOD_ROWS, d), lambda l, j: (0, 0)),
            pl.BlockSpec((None, d, tn), lambda l, j: (l, 0, j)),
            pl.BlockSpec((None, 1, tn), lambda l, j: (l, 0, j)),
        ],
        out_specs=pl.BlockSpec((None, MOD_ROWS, tn), lambda l, j: (l, 0, j)),
        out_shape=jax.ShapeDtypeStruct((depth, MOD_ROWS, n), F32),
        compiler_params=_params("arbitrary", "arbitrary"),
        name="ada_mod",
    )(cc, w_ada, b_ada.reshape(depth, 1, n))


class _Mod:
    def __init__(self, table, layer, batch, seq, is_ctx):
        self.table, self.layer, self.batch, self.seq, self.is_ctx = table, layer, batch, seq, is_ctx

    def spec(self, d, sub, kind, tm):
        base = self.layer * MOD_ROWS
        off = 3 * sub + kind
        if self.is_ctx:
            row = (base + self.batch) * 3 * N_SUB + off
            return pl.BlockSpec((None, 1, d), lambda i, *_: (row, 0, 0))
        tpb = self.seq // tm
        return pl.BlockSpec((None, 1, d), lambda i, *_: ((base + i // tpb) * 3 * N_SUB + off, 0, 0))


def _ffn_kernel(x_ref, shift_ref, scale_ref, gate_ref, w1_ref, w3_ref, w2_ref, g_ref, b_ref, o_ref,
                hb_ref, *, alpha, nf):
    f = pl.program_id(1)
    tm = x_ref.shape[0]

    def modulate(rows):
        hb_ref[rows, :] = (x_ref[rows, :] * (1.0 + scale_ref[...]) + shift_ref[...]).astype(BF16)

    def swiglu_chunk(rows):
        hb = hb_ref[rows, :]
        a = _dot(hb, w1_ref[...])
        u = _dot(hb, w3_ref[...])
        act = (a * jax.nn.sigmoid(a) * u).astype(BF16)
        return _dot(act, w2_ref[...])

    def post_norm(rows, y):
        r = alpha * x_ref[rows, :] + gate_ref[...] * (HALF_STEP * y)
        o_ref[rows, :] = _layer_norm(r, g_ref[...], b_ref[...])

    def step(first, last):
        rc = min(ROW_CHUNK, tm) if (first or last) else tm
        rows = lambda r: slice(r * rc, (r + 1) * rc)
        n = tm // rc
        if first:
            modulate(rows(0))

        def matmuls(r):
            if first and r + 1 < n:
                modulate(rows(r + 1))
            y = swiglu_chunk(rows(r))
            return y if first else o_ref[rows(r), :] + y

        def finish(r, y):
            if last:
                post_norm(rows(r), y)
            else:
                o_ref[rows(r), :] = y

        _skewed_chunks(n, matmuls, finish)

    if nf == 1:
        step(True, True)
        return
    pl.when(f == 0)(lambda: step(True, False))
    pl.when(f == nf - 1)(lambda: step(False, True))
    if nf > 2:
        pl.when(jnp.logical_and(f > 0, f < nf - 1))(lambda: step(False, False))


def _ffn(x, mod, sub, w1, w3, w2, half, g, b, alpha):
    t, d = x.shape
    f = w1.shape[-1]
    layer = mod.layer
    tm = _divisor_tile(t if mod.is_ctx else mod.seq, TOKEN_TILE, 8)
    tf = _divisor_tile(f, FF_TILE, LANES)
    nf = f // tf
    row = pl.BlockSpec((tm, d), lambda i, k: (i, 0))
    vec = pl.BlockSpec((1, d), lambda i, k: (0, 0))
    return pl.pallas_call(
        functools.partial(_ffn_kernel, alpha=alpha, nf=nf),
        grid=(t // tm, nf),
        in_specs=[
            row,
            mod.spec(d, sub, 0, tm), mod.spec(d, sub, 1, tm), mod.spec(d, sub, 2, tm),
            pl.BlockSpec((None, None, d, tf), lambda i, k: (layer, half, 0, k)),
            pl.BlockSpec((None, None, d, tf), lambda i, k: (layer, half, 0, k)),
            pl.BlockSpec((None, None, tf, d), lambda i, k: (layer, half, k, 0)),
            vec, vec,
        ],
        out_specs=row,
        out_shape=jax.ShapeDtypeStruct((t, d), F32),
        scratch_shapes=[pltpu.VMEM((tm, d), BF16)],
        compiler_params=_params("parallel", "arbitrary"),
        name="ffn_half_step",
    )(x, mod.table, mod.table, mod.table, w1, w3, w2, g.reshape(1, d), b.reshape(1, d))


def _proj_kernel(*refs, headnorm, colscale, rope, plain_tail):
    x_ref, shift_ref, scale_ref, w_ref = refs[:4]
    rest = list(refs[4:])
    gain_ref = rest.pop(0) if (headnorm or colscale) else None
    cos_ref, sin_ref = (rest.pop(0), rest.pop(0)) if rope else (None, None)
    o_ref, hb_ref = rest

    @pl.when(pl.program_id(1) == 0)
    def _():
        hb_ref[...] = (x_ref[...] * (1.0 + scale_ref[...]) + shift_ref[...]).astype(BF16)

    if not (headnorm or colscale or rope):
        o_ref[...] = _dot(hb_ref[...], w_ref[...]).astype(o_ref.dtype)
        return
    tn = o_ref.shape[1]
    cw = next(w for w in (COL_CHUNK, COL_CHUNK // 2, LANES) if tn % w == 0)

    def finish(c, z):
        col0 = [c * cw + kb * LANES for kb in range(cw // LANES)]
        hw = z[0].shape[1]
        blks = [z[(kb * LANES) // hw][:, (kb * LANES) % hw:(kb * LANES) % hw + LANES] for kb in range(cw // LANES)]
        live = [s < tn - plain_tail for s in col0]
        if headnorm:
            ms = [jnp.mean(b * b, axis=-1, keepdims=True) if on else None for b, on in zip(blks, live)]
            rs = [lax.rsqrt(m + NORM_EPS) if on else None for m, on in zip(ms, live)]
            blks = [b * r * gain_ref[:, s:s + LANES] if on else b for b, r, s, on in zip(blks, rs, col0, live)]
        elif colscale:
            blks = [b * gain_ref[:, s:s + LANES] if on else b for b, s, on in zip(blks, col0, live)]
        if rope:
            rolled = [pltpu.roll(b, LANES // 2, 1) if on else None for b, on in zip(blks, live)]
            blks = [b * cos_ref[...] + r * sin_ref[...] if on else b for b, r, on in zip(blks, rolled, live)]
        for b, s in zip(blks, col0):
            o_ref[:, s:s + LANES] = b.astype(o_ref.dtype)

    nh = 2 if cw >= 2 * LANES else 1
    hw = cw // nh
    _skewed_chunks(
        tn // cw,
        lambda c: [_dot(hb_ref[...], w_ref[:, c * cw + h * hw:c * cw + (h + 1) * hw]) for h in range(nh)],
        finish)


def _proj(x, mod, sub, w, col0, ncols, out_dtype, gain=None, headnorm=False, rope=None, plain_tail=0):
    t, d = x.shape
    tm = _divisor_tile(t if mod.is_ctx else mod.seq, TOKEN_TILE, 8)
    tn = _divisor_tile(ncols, PROJ_TILE, LANES)
    assert col0 % tn == 0 and (plain_tail == 0 or tn == ncols)
    joff = col0 // tn
    in_specs = [
        pl.BlockSpec((tm, d), lambda i, j: (i, 0)),
        mod.spec(d, sub, 0, tm), mod.spec(d, sub, 1, tm),
        pl.BlockSpec((d, tn), lambda i, j: (0, j + joff)),
    ]
    args = [x, mod.table, mod.table, w]
    if gain is not None:
        in_specs.append(pl.BlockSpec((1, tn), lambda i, j: (0, j)))
        args.append(gain)
    if rope is not None:
        tpb = mod.seq // tm
        tab = pl.BlockSpec((tm, LANES), lambda i, j: (i % tpb, 0))
        in_specs += [tab, tab]
        args += list(rope)
    return pl.pallas_call(
        functools.partial(_proj_kernel, headnorm=headnorm, colscale=gain is not None and not headnorm,
                          rope=rope is not None, plain_tail=plain_tail),
        grid=(t // tm, ncols // tn),
        in_specs=in_specs,
        out_specs=pl.BlockSpec((tm, tn), lambda i, j: (i, j)),
        out_shape=jax.ShapeDtypeStruct((t, ncols), out_dtype),
        scratch_shapes=[pltpu.VMEM((tm, d), BF16)],
        compiler_params=_params("parallel", "arbitrary"),
        name="mixer_in_proj",
    )(*args)


def _mla_up_kernel(*refs, rope, qscale):
    zm_ref, gcq_ref, gckv_ref, wuq_ref, wkn_ref, wv_ref = refs[:6]
    rest = list(refs[6:])
    cos_ref, sin_ref = (rest.pop(0), rest.pop(0)) if rope else (None, None)
    qa_ref, ka_ref, va_ref = rest
    hd = 2 * LANES

    def rot(blk):
        return _rope_block(blk, cos_ref[...], sin_ref[...]) if rope else blk

    nq = _rms_norm(zm_ref[:, :MLA_Q_RANK], gcq_ref[...]).astype(BF16)
    q = _dot(nq, wuq_ref[...]) * qscale
    for h in range(MLA_HEADS):
        qa_ref[:, h * hd:h * hd + LANES] = q[:, h * hd:h * hd + LANES].astype(BF16)
        qa_ref[:, h * hd + LANES:(h + 1) * hd] = rot(q[:, h * hd + LANES:(h + 1) * hd]).astype(BF16)

    nkv = _rms_norm(zm_ref[:, MLA_Q_RANK:MLA_Q_RANK + MLA_KV_RANK], gckv_ref[...]).astype(BF16)
    kn = _dot(nkv, wkn_ref[...])
    va_ref[...] = _dot(nkv, wv_ref[...]).astype(BF16)
    kr = rot(zm_ref[:, MLA_Q_RANK + MLA_KV_RANK:MLA_Q_RANK + MLA_KV_RANK + LANES]).astype(BF16)
    for h in range(MLA_HEADS):
        ka_ref[:, h * hd:h * hd + LANES] = kn[:, h * LANES:(h + 1) * LANES].astype(BF16)
        ka_ref[:, h * hd + LANES:(h + 1) * hd] = kr


def _mla_up(zm, seq, g_cq, g_ckv, wuq, wkn, wv, rope, qscale):
    t, nz = zm.shape
    tm = _divisor_tile(seq if rope is not None else t, TOKEN_TILE, 8)
    hq = MLA_HEADS * 2 * LANES
    hv = MLA_HEADS * MLA_V
    full = lambda a: pl.BlockSpec(a.shape, lambda i: (0, 0))
    g_cq = g_cq.reshape(1, -1)
    g_ckv = g_ckv.reshape(1, -1)
    in_specs = [pl.BlockSpec((tm, nz), lambda i: (i, 0)), full(g_cq), full(g_ckv), full(wuq), full(wkn), full(wv)]
    args = [zm, g_cq, g_ckv, wuq, wkn, wv]
    if rope is not None:
        tpb = seq // tm
        tab = pl.BlockSpec((tm, LANES), lambda i: (i % tpb, 0))
        in_specs += [tab, tab]
        args += list(rope)
    return pl.pallas_call(
        functools.partial(_mla_up_kernel, rope=rope is not None, qscale=qscale),
        grid=(t // tm,),
        in_specs=in_specs,
        out_specs=[pl.BlockSpec((tm, hq), lambda i: (i, 0)), pl.BlockSpec((tm, hq), lambda i: (i, 0)),
                   pl.BlockSpec((tm, hv), lambda i: (i, 0))],
        out_shape=[jax.ShapeDtypeStruct((t, hq), BF16), jax.ShapeDtypeStruct((t, hq), BF16),
                   jax.ShapeDtypeStruct((t, hv), BF16)],
        compiler_params=_params("parallel"),
        name="mla_up_proj",
    )(*args)


def _raw_scores(q, kc_ref, kl_ref, kcols):
    sc = _dot_nt(q, kc_ref[:, kcols])
    sl = None if kl_ref is None else _dot_nt(q, kl_ref[:, kcols])
    return sc, sl


def _softmax_numerators(scores):
    sc, sl = scores
    m = jnp.max(sc, axis=-1, keepdims=True)
    if sl is None:
        return jnp.exp2(sc - m), None
    m = jnp.maximum(m, jnp.max(sl, axis=-1, keepdims=True))
    return jnp.exp2(sc - m), jnp.exp2(sl - m)


def _row_sum(ec, el):
    den = jnp.sum(ec, axis=-1, keepdims=True)
    return den if el is None else den + jnp.sum(el, axis=-1, keepdims=True)


def _attn_kernel(*refs, has_lat, row_chunk, hps, group, dqk, dv):
    if has_lat:
        q_ref, kc_ref, vc_ref, kl_ref, vl_ref, o_ref = refs
    else:
        (q_ref, kc_ref, vc_ref, o_ref), kl_ref, vl_ref = refs, None, None
    nr = q_ref.shape[0] // row_chunk
    rows = lambda r: slice(r * row_chunk, (r + 1) * row_chunk)
    with_ones = lambda v: jnp.concatenate([v, jnp.ones_like(v)], axis=1)
    n_kv = vc_ref.shape[1] // dv
    vc1 = [with_ones(vc_ref[:, g * dv:(g + 1) * dv]) for g in range(n_kv)]
    vl1 = [with_ones(vl_ref[:, g * dv:(g + 1) * dv]) for g in range(n_kv)] if has_lat else None

    def scores(c):
        h, r = divmod(c, nr)
        g = h // group
        return _raw_scores(q_ref[rows(r), h * dqk:(h + 1) * dqk], kc_ref, kl_ref, slice(g * dqk, (g + 1) * dqk))

    def finish(c, s):
        h, r = divmod(c, nr)
        g = h // group
        ec, el = _softmax_numerators(s)
        o = _dot(ec.astype(BF16), vc1[g])
        if has_lat:
            o = o + _dot(el.astype(BF16), vl1[g])
        o_ref[rows(r), h * dv:(h + 1) * dv] = (o[:, :dv] / o[:, dv:]).astype(o_ref.dtype)

    _skewed_chunks(hps * nr, scores, finish)


def _attention(q, kc, vc, kl, vl, *, heads, group, dqk, dv, q_col0, k_col0, v_col0, hps=1):
    assert dv == LANES and heads % hps == 0 and (hps == 1 or hps % group == 0)
    b, lq, _ = q.shape
    lc = kc.shape[1]
    tq = _divisor_tile(lq, Q_TILE, 8)
    n_kv = max(hps // group, 1)
    kv_of = (lambda h: h // group) if hps == 1 else (lambda h: h)
    qw, kw, vw = hps * dqk, n_kv * dqk, n_kv * dv
    assert q_col0 % qw == 0 and k_col0 % kw == 0 and v_col0 % vw == 0
    qo, ko, vo = q_col0 // qw, k_col0 // kw, v_col0 // vw
    in_specs = [
        pl.BlockSpec((None, tq, qw), lambda bi, h, i: (bi, i, qo + h)),
        pl.BlockSpec((None, lc, kw), lambda bi, h, i: (bi, 0, ko + kv_of(h))),
        pl.BlockSpec((None, lc, vw), lambda bi, h, i: (bi, 0, vo + kv_of(h))),
    ]
    args = [q, kc, vc]
    if kl is not None:
        ll = kl.shape[1]
        in_specs += [
            pl.BlockSpec((None, ll, kw), lambda bi, h, i: (bi, 0, ko + kv_of(h))),
            pl.BlockSpec((None, ll, vw), lambda bi, h, i: (bi, 0, vo + kv_of(h))),
        ]
        args += [kl, vl]
    return pl.pallas_call(
        functools.partial(_attn_kernel, has_lat=kl is not None, row_chunk=min(ROW_CHUNK, tq), hps=hps,
                          group=group if hps > 1 else hps, dqk=dqk, dv=dv),
        grid=(b, heads // hps, lq // tq),
        in_specs=in_specs,
        out_specs=pl.BlockSpec((None, tq, hps * dv), lambda bi, h, i: (bi, i, h)),
        out_shape=jax.ShapeDtypeStruct((b, lq, heads * dv), BF16),
        compiler_params=_params("parallel", "arbitrary", "arbitrary"),
        name="softmax_attention",
    )(*args)


def _diff_attn_kernel(*refs, lambda_init, has_lat, row_chunk):
    if has_lat:
        q_ref, kc_ref, vc_ref, kl_ref, vl_ref, lq1, lk1, lq2, lk2, gs_ref, o_ref = refs
    else:
        (q_ref, kc_ref, vc_ref, lq1, lk1, lq2, lk2, gs_ref, o_ref), kl_ref, vl_ref = refs, None, None
    lam = (jnp.exp(jnp.sum(lq1[...] * lk1[...], axis=-1, keepdims=True))
           - jnp.exp(jnp.sum(lq2[...] * lk2[...], axis=-1, keepdims=True)) + lambda_init)
    hd = DIFF_HEAD_DIM
    rows = lambda r: slice(r * row_chunk, (r + 1) * row_chunk)

    def scores(r):
        return (_raw_scores(q_ref[rows(r), :hd], kc_ref, kl_ref, slice(0, hd)),
                _raw_scores(q_ref[rows(r), hd:], kc_ref, kl_ref, slice(hd, 2 * hd)))

    def finish(r, s01):
        ec0, el0 = _softmax_numerators(s01[0])
        ec1, el1 = _softmax_numerators(s01[1])
        den0 = _row_sum(ec0, el0)
        ratio = lam * den0 / _row_sum(ec1, el1)
        y = _dot((ec0 - ec1 * ratio).astype(BF16), vc_ref[...])
        if has_lat:
            y = y + _dot((el0 - el1 * ratio).astype(BF16), vl_ref[...])
        y = y / den0
        o_ref[rows(r), :] = (_rms_norm(y, gs_ref[...]) * (1.0 - lambda_init)).astype(o_ref.dtype)

    _skewed_chunks(q_ref.shape[0] // row_chunk, scores, finish)


def _diff_attention(q, kc, vc, kl, vl, lams, g_sub, lambda_init):
    b, lq, _ = q.shape
    lc = kc.shape[1]
    tq = _divisor_tile(lq, Q_TILE, 8)
    w = 2 * DIFF_HEAD_DIM
    in_specs = [
        pl.BlockSpec((None, tq, w), lambda bi, h, i: (bi, i, h)),
        pl.BlockSpec((None, lc, w), lambda bi, h, i: (bi, 0, DIFF_HEADS + h)),
        pl.BlockSpec((None, lc, w), lambda bi, h, i: (bi, 0, h)),
    ]
    args = [q, kc, vc]
    if kl is not None:
        ll = kl.shape[1]
        in_specs += [
            pl.BlockSpec((None, ll, w), lambda bi, h, i: (bi, 0, DIFF_HEADS + h)),
            pl.BlockSpec((None, ll, w), lambda bi, h, i: (bi, 0, h)),
        ]
        args += [kl, vl]
    small = lambda n: pl.BlockSpec((1, n), lambda bi, h, i: (0, 0))
    in_specs += [small(DIFF_HEAD_DIM)] * 4 + [small(w)]
    args += [a.reshape(1, -1) for a in lams] + [g_sub.reshape(1, -1)]
    return pl.pallas_call(
        functools.partial(_diff_attn_kernel, lambda_init=lambda_init, has_lat=kl is not None,
                          row_chunk=min(ROW_CHUNK, tq)),
        grid=(b, DIFF_HEADS, lq // tq),
        in_specs=in_specs,
        out_specs=pl.BlockSpec((None, tq, w), lambda bi, h, i: (bi, i, h)),
        out_shape=jax.ShapeDtypeStruct((b, lq, DIFF_HEADS * w), BF16),
        compiler_params=_params("parallel", "arbitrary", "arbitrary"),
        name="diff_attention",
    )(*args)


def _outproj_kernel(*refs, nparts, alpha):
    y_refs = refs[:nparts]
    w_refs = refs[nparts:2 * nparts]
    x_ref, gate_ref, g_ref, b_ref, o_ref = refs[2 * nparts:]
    row_chunk = min(ROW_CHUNK, o_ref.shape[0])
    rows = lambda r: slice(r * row_chunk, (r + 1) * row_chunk)

    def project(r):
        y = _dot(y_refs[0][rows(r), :], w_refs[0][...])
        for p in range(1, nparts):
            y = y + _dot(y_refs[p][rows(r), :], w_refs[p][...])
        return y

    def finish(r, y):
        res = alpha * x_ref[rows(r), :] + gate_ref[...] * y
        o_ref[rows(r), :] = _layer_norm(res, g_ref[...], b_ref[...])

    _skewed_chunks(o_ref.shape[0] // row_chunk, project, finish)


def _outproj(ys, w_o, x, mod, sub, g, b, alpha):
    t, d = x.shape
    tm = _divisor_tile(t if mod.is_ctx else mod.seq, OUT_TOKEN_TILE, 8)
    kp = ys[0].shape[1]
    assert all(y.shape[1] == kp for y in ys) and kp * len(ys) == w_o.shape[0]
    vec = pl.BlockSpec((1, d), lambda i: (0, 0))
    in_specs = [pl.BlockSpec((tm, kp), lambda i: (i, 0)) for _ in ys]
    in_specs += [pl.BlockSpec((kp, d), functools.partial(lambda i, p: (p, 0), p=p), pipeline_mode=pl.Buffered(1))
                 for p in range(len(ys))]
    in_specs += [pl.BlockSpec((tm, d), lambda i: (i, 0)), mod.spec(d, sub, 2, tm), vec, vec]
    return pl.pallas_call(
        functools.partial(_outproj_kernel, nparts=len(ys), alpha=alpha),
        grid=(t // tm,),
        in_specs=in_specs,
        out_specs=pl.BlockSpec((tm, d), lambda i: (i, 0)),
        out_shape=jax.ShapeDtypeStruct((t, d), F32),
        compiler_params=_params("parallel"),
        name="mixer_out_proj",
    )(*ys, *([w_o] * len(ys)), x, mod.table, g.reshape(1, d), b.reshape(1, d))


def _rope_tables(rows, rot_dim):
    r, col = jnp.meshgrid(jnp.arange(rows, dtype=F32), jnp.arange(GRID_W, dtype=F32), indexing="ij")
    n_freq = rot_dim // 4
    inv_freq = ROPE_THETA ** (-jnp.arange(n_freq, dtype=F32) / n_freq)
    ang = jnp.concatenate([r.reshape(-1, 1) * inv_freq, col.reshape(-1, 1) * inv_freq], -1)
    cos, sin = jnp.cos(ang), jnp.sin(ang)
    pad = LANES // 2 - rot_dim // 2
    cos = jnp.pad(cos, ((0, 0), (0, pad)), constant_values=1.0)
    sin = jnp.pad(sin, ((0, 0), (0, pad)))
    return jnp.concatenate([cos, cos], -1), jnp.concatenate([-sin, sin], -1)


def _spread_rot(w):
    half = MLA_ROPE // 2
    z = jnp.zeros((w.shape[0], LANES // 2 - half), w.dtype)
    return jnp.concatenate([w[:, :half], z, w[:, half:], z], -1)


def kernel(x, c, ctx, c_ctx, w_ada, b_ada, ln_g, ln_b, ffn_w1, ffn_w3, ffn_w2, mg_w_in, mla_g_cq, mla_g_ckv,
           mla_w_uq, mla_w_ukv, gqa_g_q, gqa_g_k, mg_w_o, diff_w_in, diff_lq1, diff_lk1, diff_lq2, diff_lk2,
           diff_g_sub, diff_w_o):
    bsz, seq, d = x.shape
    lc = ctx.shape[1]
    depth = w_ada.shape[0]
    assert bsz + 1 <= MOD_ROWS and seq % GRID_W == 0
    alpha = (2 * depth) ** 0.25
    rows = seq // GRID_W
    rope_mla = _rope_tables(rows, MLA_ROPE)
    rope_hd = _rope_tables(rows, GQA_HEAD_DIM)

    cc = jnp.concatenate([c, c_ctx[None], jnp.zeros((MOD_ROWS - bsz - 1, d), F32)], 0)
    mod_table = _ada(cc, w_ada, b_ada).reshape(depth * MOD_ROWS * 3 * N_SUB, 1, d)

    x_lat = x.reshape(bsz * seq, d)
    x_ctx = ctx.reshape(bsz * lc, d)
    w1b, w3b, w2b = ffn_w1.astype(BF16), ffn_w3.astype(BF16), ffn_w2.astype(BF16)

    for i in range(depth):
        need_ctx = i < depth - 1
        mod_l = _Mod(mod_table, i, bsz, seq, False)
        mod_c = _Mod(mod_table, i, bsz, seq, True)
        ffn = lambda xs, m, sub, k: _ffn(xs, m, sub, w1b, w3b, w2b, k, ln_g[i, sub], ln_b[i, sub], alpha)

        x_lat = ffn(x_lat, mod_l, 0, 0)
        x_ctx = ffn(x_ctx, mod_c, 0, 0)

        if i % 2 == 0:
            e = i // 2
            w_in = mg_w_in[e]
            o_kr = MLA_Q_RANK + MLA_KV_RANK
            o_gq = o_kr + MLA_ROPE
            n_gqk = (GQA_HEADS + GQA_KV_HEADS) * GQA_HEAD_DIM
            w_mla = jnp.concatenate([w_in[:, :o_kr], _spread_rot(w_in[:, o_kr:o_gq])], -1).astype(BF16)
            w_gqa = w_in[:, o_gq:].astype(BF16)
            wuq = mla_w_uq[e].reshape(MLA_Q_RANK, MLA_HEADS, MLA_NOPE + MLA_ROPE)
            wuq = jnp.concatenate(
                [wuq[:, :, :MLA_NOPE],
                 _spread_rot(wuq[:, :, MLA_NOPE:].reshape(-1, MLA_ROPE)).reshape(MLA_Q_RANK, MLA_HEADS, LANES)],
                -1).reshape(MLA_Q_RANK, MLA_HEADS * 2 * LANES).astype(BF16)
            wukv = mla_w_ukv[e].reshape(MLA_KV_RANK, MLA_HEADS, MLA_NOPE + MLA_V)
            wkn = wukv[:, :, :MLA_NOPE].reshape(MLA_KV_RANK, -1).astype(BF16)
            wv = wukv[:, :, MLA_NOPE:].reshape(MLA_KV_RANK, -1).astype(BF16)
            n_gv = GQA_KV_HEADS * GQA_HEAD_DIM
            gain = jnp.concatenate([jnp.tile(gqa_g_q[e] * (GQA_SCALE * LOG2_E), GQA_HEADS),
                                    jnp.tile(gqa_g_k[e], GQA_KV_HEADS), jnp.ones((n_gv,), F32)])[None]
            w_o = mg_w_o[e].astype(BF16)

            def project(xs, m, rp_mla, rp_hd):
                zm = _proj(xs, m, 1, w_mla, 0, w_mla.shape[1], F32)
                qkv = _proj(xs, m, 1, w_gqa, 0, n_gqk + n_gv, BF16, gain=gain, headnorm=True, rope=rp_hd,
                            plain_tail=n_gv)
                qa, ka, va = _mla_up(zm, seq, mla_g_cq[e], mla_g_ckv[e], wuq, wkn, wv, rp_mla,
                                     MLA_SCALE * LOG2_E)
                n = xs.shape[0] // bsz
                return [a.reshape(bsz, n, -1) for a in (qa, ka, va, qkv)]

            qa, ka, va, qkv = project(x_lat, mod_l, rope_mla, rope_hd)
            qa_c, ka_c, va_c, qkv_c = project(x_ctx, mod_c, None, None)
            mla = dict(heads=MLA_HEADS, group=1, dqk=2 * LANES, dv=MLA_V, q_col0=0, k_col0=0, v_col0=0)
            gqa = dict(heads=GQA_HEADS, group=GQA_HEADS // GQA_KV_HEADS, dqk=GQA_HEAD_DIM, dv=GQA_HEAD_DIM,
                       q_col0=0, k_col0=GQA_HEADS * GQA_HEAD_DIM, v_col0=n_gqk)
            ya = _attention(qa, ka_c, va_c, ka, va, **mla)
            yb = _attention(qkv, qkv_c, qkv_c, qkv, qkv, **gqa)
            ys_lat = [ya.reshape(bsz * seq, -1), yb.reshape(bsz * seq, -1)]
            if need_ctx:
                ya_c = _attention(qa_c, ka_c, va_c, None, None, **mla, hps=MLA_HEADS)
                yb_c = _attention(qkv_c, qkv_c, qkv_c, None, None, **gqa, hps=GQA_HEADS)
                ys_ctx = [ya_c.reshape(bsz * lc, -1), yb_c.reshape(bsz * lc, -1)]
        else:
            o = i // 2
            lambda_init = 0.8 - 0.6 * math.exp(-0.3 * i)
            w_in = diff_w_in[o].astype(BF16)
            n_qk = 2 * DIFF_HEADS * 2 * DIFF_HEAD_DIM
            w_o = diff_w_o[o].astype(BF16)
            lams = (diff_lq1[o], diff_lk1[o], diff_lq2[o], diff_lk2[o])
            qk_scale = jnp.concatenate([jnp.full((n_qk // 2,), DIFF_SCALE * LOG2_E, F32),
                                        jnp.ones((n_qk // 2,), F32)])[None]

            def project(xs, m, rp):
                qk = _proj(xs, m, 1, w_in, 0, n_qk, BF16, gain=qk_scale, rope=rp)
                v = _proj(xs, m, 1, w_in, n_qk, w_in.shape[1] - n_qk, BF16)
                n = xs.shape[0] // bsz
                return qk.reshape(bsz, n, -1), v.reshape(bsz, n, -1)

            qk, v = project(x_lat, mod_l, rope_hd)
            qk_c, v_c = project(x_ctx, mod_c, None)
            y = _diff_attention(qk, qk_c, v_c, qk, v, lams, diff_g_sub[o], lambda_init)
            ys_lat = [y.reshape(bsz * seq, -1)]
            if need_ctx:
                y_c = _diff_attention(qk_c, qk_c, v_c, None, None, lams, diff_g_sub[o], lambda_init)
                ys_ctx = [y_c.reshape(bsz * lc, -1)]

        x_lat = _outproj(ys_lat, w_o, x_lat, mod_l, 1, ln_g[i, 1], ln_b[i, 1], alpha)
        x_lat = ffn(x_lat, mod_l, 2, 1)
        if need_ctx:
            x_ctx = _outproj(ys_ctx, w_o, x_ctx, mod_c, 1, ln_g[i, 1], ln_b[i, 1], alpha)
            x_ctx = ffn(x_ctx, mod_c, 2, 1)
    return x_lat.reshape(bsz, seq, d)
```

```python
import functools
import math

import jax
import jax.numpy as jnp
from jax import lax
from jax.experimental import pallas as pl
from jax.experimental.pallas import tpu as pltpu

F32 = jnp.float32
BF16 = jnp.bfloat16

GRID_W = 64
ROPE_THETA = 10000.0
NORM_EPS = 1e-6
N_SUB = 3
HALF_STEP = 0.5
MLA_HEADS = 8
MLA_Q_RANK = 512
MLA_KV_RANK = 256
MLA_NOPE = 128
MLA_ROPE = 64
MLA_V = 128
MLA_SCALE = (MLA_NOPE + MLA_ROPE) ** -0.5
GQA_HEADS = 8
GQA_KV_HEADS = 2
GQA_HEAD_DIM = 128
GQA_SCALE = GQA_HEAD_DIM ** -0.5
DIFF_HEADS = 8
DIFF_HEAD_DIM = 128
DIFF_SCALE = DIFF_HEAD_DIM ** -0.5

LANES = 128
MOD_ROWS = 32
VMEM_LIMIT_BYTES = 56 * 1024 * 1024

TOKEN_TILE = 1024
OUT_TOKEN_TILE = 512
FF_TILE = 512
PROJ_TILE = 1536
ADA_TILE = 1024
Q_TILE = 2048
ROW_CHUNK = 256
ATTN_ROW_CHUNK = 512
COL_CHUNK = 512
LOG2_E = math.log2(math.e)


def _divisor_tile(n, pref, mult):
    if n <= pref:
        return n
    t = (pref // mult) * mult
    while t >= mult:
        if n % t == 0:
            return t
        t -= mult
    raise ValueError(f"no tile for {n} (pref {pref}, mult {mult})")


def _params(*sem):
    return pltpu.CompilerParams(dimension_semantics=sem, vmem_limit_bytes=VMEM_LIMIT_BYTES)


def _layer_norm(r, g, b):
    mu = jnp.mean(r, axis=-1, keepdims=True)
    d = r - mu
    var = jnp.mean(d * d, axis=-1, keepdims=True)
    return d * lax.rsqrt(var + NORM_EPS) * g + b


def _rms_norm(x, g):
    return x * lax.rsqrt(jnp.mean(x * x, axis=-1, keepdims=True) + NORM_EPS) * g


def _rope_block(blk, cos, sin):
    return blk * cos + pltpu.roll(blk, LANES // 2, 1) * sin


def _skewed_chunks(n, matmul_fn, finish_fn):
    nxt = matmul_fn(0)
    for c in range(n):
        cur, nxt = nxt, (matmul_fn(c + 1) if c + 1 < n else None)
        finish_fn(c, cur)


def _dot(a, b):
    return jnp.dot(a, b, preferred_element_type=F32)


def _dot_nt(a, b):
    return lax.dot_general(a, b, (((1,), (1,)), ((), ())), preferred_element_type=F32)


def _ada_kernel(c_ref, w_ref, b_ref, o_ref):
    c = c_ref[...]
    s = (c * jax.nn.sigmoid(c)).astype(BF16)
    o_ref[...] = _dot(s, w_ref[...].astype(BF16)) + b_ref[...]


def _ada(cc, w_ada, b_ada):
    depth, d, n = w_ada.shape
    tn = _divisor_tile(n, ADA_TILE, LANES)
    return pl.pallas_call(
        _ada_kernel,
        grid=(depth, n // tn),
        in_specs=[
            pl.BlockSpec((MOD_ROWS, d), lambda l, j: (0, 0)),
            pl.BlockSpec((None, d, tn), lambda l, j: (l, 0, j)),
            pl.BlockSpec((None, 1, tn), lambda l, j: (l, 0, j)),
        ],
        out_specs=pl.BlockSpec((None, MOD_ROWS, tn), lambda l, j: (l, 0, j)),
        out_shape=jax.ShapeDtypeStruct((depth, MOD_ROWS, n), F32),
        compiler_params=_params("arbitrary", "arbitrary"),
        name="ada_mod",
    )(cc, w_ada, b_ada.reshape(depth, 1, n))


class _Mod:
    def __init__(self, table, layer, batch, seq, is_ctx):
        self.table, self.layer, self.batch, self.seq, self.is_ctx = table, layer, batch, seq, is_ctx

    def spec(self, d, sub, kind, tm):
        base = self.layer * MOD_ROWS
        off = 3 * sub + kind
        if self.is_ctx:
            row = (base + self.batch) * 3 * N_SUB + off
            return pl.BlockSpec((None, 1, d), lambda i, *_: (row, 0, 0))
        tpb = self.seq // tm
        return pl.BlockSpec((None, 1, d), lambda i, *_: ((base + i // tpb) * 3 * N_SUB + off, 0, 0))


def _ffn_kernel(x_ref, shift_ref, scale_ref, gate_ref, w1_ref, w3_ref, w2_ref, g_ref, b_ref, o_ref,
                hb_ref, *, alpha, nf):
    f = pl.program_id(1)
    tm = x_ref.shape[0]

    def modulate(rows):
        hb_ref[rows, :] = (x_ref[rows, :] * (1.0 + scale_ref[...]) + shift_ref[...]).astype(BF16)

    def swiglu_chunk(rows):
        hb = hb_ref[rows, :]
        a = _dot(hb, w1_ref[...])
        u = _dot(hb, w3_ref[...])
        act = (a * jax.nn.sigmoid(a) * u).astype(BF16)
        return _dot(act, w2_ref[...])

    def post_norm(rows, y):
        r = alpha * x_ref[rows, :] + gate_ref[...] * (HALF_STEP * y)
        o_ref[rows, :] = _layer_norm(r, g_ref[...], b_ref[...])

    def step(first, last):
        rc = min(ROW_CHUNK, tm) if (first or last) else tm
        rows = lambda r: slice(r * rc, (r + 1) * rc)
        n = tm // rc
        if first:
            modulate(rows(0))

        def matmuls(r):
            if first and r + 1 < n:
                modulate(rows(r + 1))
            y = swiglu_chunk(rows(r))
            return y if first else o_ref[rows(r), :] + y

        def finish(r, y):
            if last:
                post_norm(rows(r), y)
            else:
                o_ref[rows(r), :] = y

        _skewed_chunks(n, matmuls, finish)

    if nf == 1:
        step(True, True)
        return
    pl.when(f == 0)(lambda: step(True, False))
    pl.when(f == nf - 1)(lambda: step(False, True))
    if nf > 2:
        pl.when(jnp.logical_and(f > 0, f < nf - 1))(lambda: step(False, False))


def _ffn(x, mod, sub, w1, w3, w2, half, g, b, alpha):
    t, d = x.shape
    f = w1.shape[-1]
    layer = mod.layer
    tm = _divisor_tile(t if mod.is_ctx else mod.seq, TOKEN_TILE, 8)
    tf = _divisor_tile(f, FF_TILE, LANES)
    nf = f // tf
    row = pl.BlockSpec((tm, d), lambda i, k: (i, 0))
    vec = pl.BlockSpec((1, d), lambda i, k: (0, 0))
    return pl.pallas_call(
        functools.partial(_ffn_kernel, alpha=alpha, nf=nf),
        grid=(t // tm, nf),
        in_specs=[
            row,
            mod.spec(d, sub, 0, tm), mod.spec(d, sub, 1, tm), mod.spec(d, sub, 2, tm),
            pl.BlockSpec((None, None, d, tf), lambda i, k: (layer, half, 0, k)),
            pl.BlockSpec((None, None, d, tf), lambda i, k: (layer, half, 0, k)),
            pl.BlockSpec((None, None, tf, d), lambda i, k: (layer, half, k, 0)),
            vec, vec,
        ],
        out_specs=row,
        out_shape=jax.ShapeDtypeStruct((t, d), F32),
        scratch_shapes=[pltpu.VMEM((tm, d), BF16)],
        compiler_params=_params("parallel", "arbitrary"),
        name="ffn_half_step",
    )(x, mod.table, mod.table, mod.table, w1, w3, w2, g.reshape(1, d), b.reshape(1, d))


def _proj_kernel(*refs, headnorm, colscale, rope, plain_tail):
    x_ref, shift_ref, scale_ref, w_ref = refs[:4]
    rest = list(refs[4:])
    gain_ref = rest.pop(0) if (headnorm or colscale) else None
    cos_ref, sin_ref = (rest.pop(0), rest.pop(0)) if rope else (None, None)
    o_ref, hb_ref = rest

    @pl.when(pl.program_id(1) == 0)
    def _():
        hb_ref[...] = (x_ref[...] * (1.0 + scale_ref[...]) + shift_ref[...]).astype(BF16)

    if not (headnorm or colscale or rope):
        o_ref[...] = _dot(hb_ref[...], w_ref[...]).astype(o_ref.dtype)
        return
    tn = o_ref.shape[1]
    cw = next(w for w in (COL_CHUNK, COL_CHUNK // 2, LANES) if tn % w == 0)

    def finish(c, z):
        col0 = [c * cw + kb * LANES for kb in range(cw // LANES)]
        hw = z[0].shape[1]
        blks = [z[(kb * LANES) // hw][:, (kb * LANES) % hw:(kb * LANES) % hw + LANES] for kb in range(cw // LANES)]
        live = [s < tn - plain_tail for s in col0]
        if headnorm:
            ms = [jnp.mean(b * b, axis=-1, keepdims=True) if on else None for b, on in zip(blks, live)]
            rs = [lax.rsqrt(m + NORM_EPS) if on else None for m, on in zip(ms, live)]
            blks = [b * r * gain_ref[:, s:s + LANES] if on else b for b, r, s, on in zip(blks, rs, col0, live)]
        elif colscale:
            blks = [b * gain_ref[:, s:s + LANES] if on else b for b, s, on in zip(blks, col0, live)]
        if rope:
            rolled = [pltpu.roll(b, LANES // 2, 1) if on else None for b, on in zip(blks, live)]
            blks = [b * cos_ref[...] + r * sin_ref[...] if on else b for b, r, on in zip(blks, rolled, live)]
        for b, s in zip(blks, col0):
            o_ref[:, s:s + LANES] = b.astype(o_ref.dtype)

    nh = 2 if cw >= 2 * LANES else 1
    hw = cw // nh
    _skewed_chunks(
        tn // cw,
        lambda c: [_dot(hb_ref[...], w_ref[:, c * cw + h * hw:c * cw + (h + 1) * hw]) for h in range(nh)],
        finish)


def _proj(x, mod, sub, w, col0, ncols, out_dtype, gain=None, headnorm=False, rope=None, plain_tail=0):
    t, d = x.shape
    tm = _divisor_tile(t if mod.is_ctx else mod.seq, TOKEN_TILE, 8)
    tn = _divisor_tile(ncols, PROJ_TILE, LANES)
    assert col0 % tn == 0 and (plain_tail == 0 or tn == ncols)
    joff = col0 // tn
    in_specs = [
        pl.BlockSpec((tm, d), lambda i, j: (i, 0)),
        mod.spec(d, sub, 0, tm), mod.spec(d, sub, 1, tm),
        pl.BlockSpec((d, tn), lambda i, j: (0, j + joff)),
    ]
    args = [x, mod.table, mod.table, w]
    if gain is not None:
        in_specs.append(pl.BlockSpec((1, tn), lambda i, j: (0, j)))
        args.append(gain)
    if rope is not None:
        tpb = mod.seq // tm
        tab = pl.BlockSpec((tm, LANES), lambda i, j: (i % tpb, 0))
        in_specs += [tab, tab]
        args += list(rope)
    return pl.pallas_call(
        functools.partial(_proj_kernel, headnorm=headnorm, colscale=gain is not None and not headnorm,
                          rope=rope is not None, plain_tail=plain_tail),
        grid=(t // tm, ncols // tn),
        in_specs=in_specs,
        out_specs=pl.BlockSpec((tm, tn), lambda i, j: (i, j)),
        out_shape=jax.ShapeDtypeStruct((t, ncols), out_dtype),
        scratch_shapes=[pltpu.VMEM((tm, d), BF16)],
        compiler_params=_params("parallel", "arbitrary"),
        name="mixer_in_proj",
    )(*args)


def _mla_up_kernel(*refs, rope, qscale):
    zm_ref, gcq_ref, gckv_ref, wuq_ref, wkn_ref, wv_ref = refs[:6]
    rest = list(refs[6:])
    cos_ref, sin_ref = (rest.pop(0), rest.pop(0)) if rope else (None, None)
    qa_ref, ka_ref, va_ref = rest
    hd = 2 * LANES

    def rot(blk):
        return _rope_block(blk, cos_ref[...], sin_ref[...]) if rope else blk

    nq = _rms_norm(zm_ref[:, :MLA_Q_RANK], gcq_ref[...]).astype(BF16)
    q = _dot(nq, wuq_ref[...]) * qscale
    for h in range(MLA_HEADS):
        qa_ref[:, h * hd:h * hd + LANES] = q[:, h * hd:h * hd + LANES].astype(BF16)
        qa_ref[:, h * hd + LANES:(h + 1) * hd] = rot(q[:, h * hd + LANES:(h + 1) * hd]).astype(BF16)

    nkv = _rms_norm(zm_ref[:, MLA_Q_RANK:MLA_Q_RANK + MLA_KV_RANK], gckv_ref[...]).astype(BF16)
    kn = _dot(nkv, wkn_ref[...])
    va_ref[...] = _dot(nkv, wv_ref[...]).astype(BF16)
    kr = rot(zm_ref[:, MLA_Q_RANK + MLA_KV_RANK:MLA_Q_RANK + MLA_KV_RANK + LANES]).astype(BF16)
    for h in range(MLA_HEADS):
        ka_ref[:, h * hd:h * hd + LANES] = kn[:, h * LANES:(h + 1) * LANES].astype(BF16)
        ka_ref[:, h * hd + LANES:(h + 1) * hd] = kr


def _mla_up(zm, seq, g_cq, g_ckv, wuq, wkn, wv, rope, qscale):
    t, nz = zm.shape
    tm = _divisor_tile(seq if rope is not None else t, TOKEN_TILE, 8)
    hq = MLA_HEADS * 2 * LANES
    hv = MLA_HEADS * MLA_V
    full = lambda a: pl.BlockSpec(a.shape, lambda i: (0, 0))
    g_cq = g_cq.reshape(1, -1)
    g_ckv = g_ckv.reshape(1, -1)
    in_specs = [pl.BlockSpec((tm, nz), lambda i: (i, 0)), full(g_cq), full(g_ckv), full(wuq), full(wkn), full(wv)]
    args = [zm, g_cq, g_ckv, wuq, wkn, wv]
    if rope is not None:
        tpb = seq // tm
        tab = pl.BlockSpec((tm, LANES), lambda i: (i % tpb, 0))
        in_specs += [tab, tab]
        args += list(rope)
    return pl.pallas_call(
        functools.partial(_mla_up_kernel, rope=rope is not None, qscale=qscale),
        grid=(t // tm,),
        in_specs=in_specs,
        out_specs=[pl.BlockSpec((tm, hq), lambda i: (i, 0)), pl.BlockSpec((tm, hq), lambda i: (i, 0)),
                   pl.BlockSpec((tm, hv), lambda i: (i, 0))],
        out_shape=[jax.ShapeDtypeStruct((t, hq), BF16), jax.ShapeDtypeStruct((t, hq), BF16),
                   jax.ShapeDtypeStruct((t, hv), BF16)],
        compiler_params=_params("parallel"),
        name="mla_up_proj",
    )(*args)


def _raw_scores(q, kc_ref, kl_ref, kcols):
    sc = _dot_nt(q, kc_ref[:, kcols])
    sl = None if kl_ref is None else _dot_nt(q, kl_ref[:, kcols])
    return sc, sl


def _softmax_numerators(scores):
    sc, sl = scores
    m = jnp.max(sc, axis=-1, keepdims=True)
    if sl is None:
        return jnp.exp2(sc - m), None
    m = jnp.maximum(m, jnp.max(sl, axis=-1, keepdims=True))
    return jnp.exp2(sc - m), jnp.exp2(sl - m)


def _row_sum(ec, el):
    den = jnp.sum(ec, axis=-1, keepdims=True)
    return den if el is None else den + jnp.sum(el, axis=-1, keepdims=True)


def _attn_kernel(*refs, has_lat, row_chunk, hps, group, dqk, dv):
    if has_lat:
        q_ref, kc_ref, vc_ref, kl_ref, vl_ref, o_ref = refs
    else:
        (q_ref, kc_ref, vc_ref, o_ref), kl_ref, vl_ref = refs, None, None
    nr = q_ref.shape[0] // row_chunk
    rows = lambda r: slice(r * row_chunk, (r + 1) * row_chunk)
    with_ones = lambda v: jnp.concatenate([v, jnp.ones_like(v)], axis=1)
    n_kv = vc_ref.shape[1] // dv
    vc1 = [with_ones(vc_ref[:, g * dv:(g + 1) * dv]) for g in range(n_kv)]
    vl1 = [with_ones(vl_ref[:, g * dv:(g + 1) * dv]) for g in range(n_kv)] if has_lat else None

    def scores(c):
        h, r = divmod(c, nr)
        g = h // group
        return _raw_scores(q_ref[rows(r), h * dqk:(h + 1) * dqk], kc_ref, kl_ref, slice(g * dqk, (g + 1) * dqk))

    def finish(c, s):
        h, r = divmod(c, nr)
        g = h // group
        ec, el = _softmax_numerators(s)
        o = _dot(ec.astype(BF16), vc1[g])
        if has_lat:
            o = o + _dot(el.astype(BF16), vl1[g])
        o_ref[rows(r), h * dv:(h + 1) * dv] = (o[:, :dv] / o[:, dv:]).astype(o_ref.dtype)

    _skewed_chunks(hps * nr, scores, finish)


def _attention(q, kc, vc, kl, vl, *, heads, group, dqk, dv, q_col0, k_col0, v_col0, hps=1):
    assert dv == LANES and heads % hps == 0 and (hps == 1 or hps % group == 0)
    b, lq, _ = q.shape
    lc = kc.shape[1]
    tq = _divisor_tile(lq, Q_TILE, 8)
    n_kv = max(hps // group, 1)
    kv_of = (lambda h: h // group) if hps == 1 else (lambda h: h)
    qw, kw, vw = hps * dqk, n_kv * dqk, n_kv * dv
    assert q_col0 % qw == 0 and k_col0 % kw == 0 and v_col0 % vw == 0
    qo, ko, vo = q_col0 // qw, k_col0 // kw, v_col0 // vw
    in_specs = [
        pl.BlockSpec((None, tq, qw), lambda bi, h, i: (bi, i, qo + h)),
        pl.BlockSpec((None, lc, kw), lambda bi, h, i: (bi, 0, ko + kv_of(h))),
        pl.BlockSpec((None, lc, vw), lambda bi, h, i: (bi, 0, vo + kv_of(h))),
    ]
    args = [q, kc, vc]
    if kl is not None:
        ll = kl.shape[1]
        in_specs += [
            pl.BlockSpec((None, ll, kw), lambda bi, h, i: (bi, 0, ko + kv_of(h))),
            pl.BlockSpec((None, ll, vw), lambda bi, h, i: (bi, 0, vo + kv_of(h))),
        ]
        args += [kl, vl]
    return pl.pallas_call(
        functools.partial(_attn_kernel, has_lat=kl is not None, row_chunk=min(ATTN_ROW_CHUNK, tq), hps=hps,
                          group=group if hps > 1 else hps, dqk=dqk, dv=dv),
        grid=(b, heads // hps, lq // tq),
        in_specs=in_specs,
        out_specs=pl.BlockSpec((None, tq, hps * dv), lambda bi, h, i: (bi, i, h)),
        out_shape=jax.ShapeDtypeStruct((b, lq, heads * dv), BF16),
        compiler_params=_params("parallel", "arbitrary", "arbitrary"),
        name="softmax_attention",
    )(*args)


def _diff_attn_kernel(*refs, lambda_init, has_lat, row_chunk):
    if has_lat:
        q_ref, kc_ref, vc_ref, kl_ref, vl_ref, lq1, lk1, lq2, lk2, gs_ref, o_ref = refs
    else:
        (q_ref, kc_ref, vc_ref, lq1, lk1, lq2, lk2, gs_ref, o_ref), kl_ref, vl_ref = refs, None, None
    lam = (jnp.exp(jnp.sum(lq1[...] * lk1[...], axis=-1, keepdims=True))
           - jnp.exp(jnp.sum(lq2[...] * lk2[...], axis=-1, keepdims=True)) + lambda_init)
    hd = DIFF_HEAD_DIM
    rows = lambda r: slice(r * row_chunk, (r + 1) * row_chunk)

    def scores(r):
        return (_raw_scores(q_ref[rows(r), :hd], kc_ref, kl_ref, slice(0, hd)),
                _raw_scores(q_ref[rows(r), hd:], kc_ref, kl_ref, slice(hd, 2 * hd)))

    def finish(r, s01):
        ec0, el0 = _softmax_numerators(s01[0])
        ec1, el1 = _softmax_numerators(s01[1])
        den0 = _row_sum(ec0, el0)
        ratio = lam * den0 / _row_sum(ec1, el1)
        y = _dot((ec0 - ec1 * ratio).astype(BF16), vc_ref[...])
        if has_lat:
            y = y + _dot((el0 - el1 * ratio).astype(BF16), vl_ref[...])
        y = y / den0
        o_ref[rows(r), :] = (_rms_norm(y, gs_ref[...]) * (1.0 - lambda_init)).astype(o_ref.dtype)

    _skewed_chunks(q_ref.shape[0] // row_chunk, scores, finish)


def _diff_attention(q, kc, vc, kl, vl, lams, g_sub, lambda_init):
    b, lq, _ = q.shape
    lc = kc.shape[1]
    tq = _divisor_tile(lq, Q_TILE, 8)
    w = 2 * DIFF_HEAD_DIM
    in_specs = [
        pl.BlockSpec((None, tq, w), lambda bi, h, i: (bi, i, h)),
        pl.BlockSpec((None, lc, w), lambda bi, h, i: (bi, 0, DIFF_HEADS + h)),
        pl.BlockSpec((None, lc, w), lambda bi, h, i: (bi, 0, h)),
    ]
    args = [q, kc, vc]
    if kl is not None:
        ll = kl.shape[1]
        in_specs += [
            pl.BlockSpec((None, ll, w), lambda bi, h, i: (bi, 0, DIFF_HEADS + h)),
            pl.BlockSpec((None, ll, w), lambda bi, h, i: (bi, 0, h)),
        ]
        args += [kl, vl]
    small = lambda n: pl.BlockSpec((1, n), lambda bi, h, i: (0, 0))
    in_specs += [small(DIFF_HEAD_DIM)] * 4 + [small(w)]
    args += [a.reshape(1, -1) for a in lams] + [g_sub.reshape(1, -1)]
    return pl.pallas_call(
        functools.partial(_diff_attn_kernel, lambda_init=lambda_init, has_lat=kl is not None,
                          row_chunk=min(ATTN_ROW_CHUNK, tq)),
        grid=(b, DIFF_HEADS, lq // tq),
        in_specs=in_specs,
        out_specs=pl.BlockSpec((None, tq, w), lambda bi, h, i: (bi, i, h)),
        out_shape=jax.ShapeDtypeStruct((b, lq, DIFF_HEADS * w), BF16),
        compiler_params=_params("parallel", "arbitrary", "arbitrary"),
        name="diff_attention",
    )(*args)


def _outproj_kernel(*refs, nparts, alpha):
    y_refs = refs[:nparts]
    w_refs = refs[nparts:2 * nparts]
    x_ref, gate_ref, g_ref, b_ref, o_ref = refs[2 * nparts:]
    row_chunk = min(ROW_CHUNK, o_ref.shape[0])
    rows = lambda r: slice(r * row_chunk, (r + 1) * row_chunk)

    def project(r):
        y = _dot(y_refs[0][rows(r), :], w_refs[0][...])
        for p in range(1, nparts):
            y = y + _dot(y_refs[p][rows(r), :], w_refs[p][...])
        return y

    def finish(r, y):
        res = alpha * x_ref[rows(r), :] + gate_ref[...] * y
        o_ref[rows(r), :] = _layer_norm(res, g_ref[...], b_ref[...])

    _skewed_chunks(o_ref.shape[0] // row_chunk, project, finish)


def _outproj(ys, w_o, x, mod, sub, g, b, alpha):
    t, d = x.shape
    tm = _divisor_tile(t if mod.is_ctx else mod.seq, OUT_TOKEN_TILE, 8)
    kp = ys[0].shape[1]
    assert all(y.shape[1] == kp for y in ys) and kp * len(ys) == w_o.shape[0]
    vec = pl.BlockSpec((1, d), lambda i: (0, 0))
    in_specs = [pl.BlockSpec((tm, kp), lambda i: (i, 0)) for _ in ys]
    in_specs += [pl.BlockSpec((kp, d), functools.partial(lambda i, p: (p, 0), p=p), pipeline_mode=pl.Buffered(1))
                 for p in range(len(ys))]
    in_specs += [pl.BlockSpec((tm, d), lambda i: (i, 0)), mod.spec(d, sub, 2, tm), vec, vec]
    return pl.pallas_call(
        functools.partial(_outproj_kernel, nparts=len(ys), alpha=alpha),
        grid=(t // tm,),
        in_specs=in_specs,
        out_specs=pl.BlockSpec((tm, d), lambda i: (i, 0)),
        out_shape=jax.ShapeDtypeStruct((t, d), F32),
        compiler_params=_params("parallel"),
        name="mixer_out_proj",
    )(*ys, *([w_o] * len(ys)), x, mod.table, g.reshape(1, d), b.reshape(1, d))


def _rope_tables(rows, rot_dim):
    r, col = jnp.meshgrid(jnp.arange(rows, dtype=F32), jnp.arange(GRID_W, dtype=F32), indexing="ij")
    n_freq = rot_dim // 4
    inv_freq = ROPE_THETA ** (-jnp.arange(n_freq, dtype=F32) / n_freq)
    ang = jnp.concatenate([r.reshape(-1, 1) * inv_freq, col.reshape(-1, 1) * inv_freq], -1)
    cos, sin = jnp.cos(ang), jnp.sin(ang)
    pad = LANES // 2 - rot_dim // 2
    cos = jnp.pad(cos, ((0, 0), (0, pad)), constant_values=1.0)
    sin = jnp.pad(sin, ((0, 0), (0, pad)))
    return jnp.concatenate([cos, cos], -1), jnp.concatenate([-sin, sin], -1)


def _spread_rot(w):
    half = MLA_ROPE // 2
    z = jnp.zeros((w.shape[0], LANES // 2 - half), w.dtype)
    return jnp.concatenate([w[:, :half], z, w[:, half:], z], -1)


def kernel(x, c, ctx, c_ctx, w_ada, b_ada, ln_g, ln_b, ffn_w1, ffn_w3, ffn_w2, mg_w_in, mla_g_cq, mla_g_ckv,
           mla_w_uq, mla_w_ukv, gqa_g_q, gqa_g_k, mg_w_o, diff_w_in, diff_lq1, diff_lk1, diff_lq2, diff_lk2,
           diff_g_sub, diff_w_o):
    bsz, seq, d = x.shape
    lc = ctx.shape[1]
    depth = w_ada.shape[0]
    assert bsz + 1 <= MOD_ROWS and seq % GRID_W == 0
    alpha = (2 * depth) ** 0.25
    rows = seq // GRID_W
    rope_mla = _rope_tables(rows, MLA_ROPE)
    rope_hd = _rope_tables(rows, GQA_HEAD_DIM)

    cc = jnp.concatenate([c, c_ctx[None], jnp.zeros((MOD_ROWS - bsz - 1, d), F32)], 0)
    mod_table = _ada(cc, w_ada, b_ada).reshape(depth * MOD_ROWS * 3 * N_SUB, 1, d)

    x_lat = x.reshape(bsz * seq, d)
    x_ctx = ctx.reshape(bsz * lc, d)
    w1b, w3b, w2b = ffn_w1.astype(BF16), ffn_w3.astype(BF16), ffn_w2.astype(BF16)

    for i in range(depth):
        need_ctx = i < depth - 1
        mod_l = _Mod(mod_table, i, bsz, seq, False)
        mod_c = _Mod(mod_table, i, bsz, seq, True)
        ffn = lambda xs, m, sub, k: _ffn(xs, m, sub, w1b, w3b, w2b, k, ln_g[i, sub], ln_b[i, sub], alpha)

        x_lat = ffn(x_lat, mod_l, 0, 0)
        x_ctx = ffn(x_ctx, mod_c, 0, 0)

        if i % 2 == 0:
            e = i // 2
            w_in = mg_w_in[e]
            o_kr = MLA_Q_RANK + MLA_KV_RANK
            o_gq = o_kr + MLA_ROPE
            n_gqk = (GQA_HEADS + GQA_KV_HEADS) * GQA_HEAD_DIM
            w_mla = jnp.concatenate([w_in[:, :o_kr], _spread_rot(w_in[:, o_kr:o_gq])], -1).astype(BF16)
            w_gqa = w_in[:, o_gq:].astype(BF16)
            wuq = mla_w_uq[e].reshape(MLA_Q_RANK, MLA_HEADS, MLA_NOPE + MLA_ROPE)
            wuq = jnp.concatenate(
                [wuq[:, :, :MLA_NOPE],
                 _spread_rot(wuq[:, :, MLA_NOPE:].reshape(-1, MLA_ROPE)).reshape(MLA_Q_RANK, MLA_HEADS, LANES)],
                -1).reshape(MLA_Q_RANK, MLA_HEADS * 2 * LANES).astype(BF16)
            wukv = mla_w_ukv[e].reshape(MLA_KV_RANK, MLA_HEADS, MLA_NOPE + MLA_V)
            wkn = wukv[:, :, :MLA_NOPE].reshape(MLA_KV_RANK, -1).astype(BF16)
            wv = wukv[:, :, MLA_NOPE:].reshape(MLA_KV_RANK, -1).astype(BF16)
            n_gv = GQA_KV_HEADS * GQA_HEAD_DIM
            gain = jnp.concatenate([jnp.tile(gqa_g_q[e] * (GQA_SCALE * LOG2_E), GQA_HEADS),
                                    jnp.tile(gqa_g_k[e], GQA_KV_HEADS), jnp.ones((n_gv,), F32)])[None]
            w_o = mg_w_o[e].astype(BF16)

            def project(xs, m, rp_mla, rp_hd):
                zm = _proj(xs, m, 1, w_mla, 0, w_mla.shape[1], F32)
                qkv = _proj(xs, m, 1, w_gqa, 0, n_gqk + n_gv, BF16, gain=gain, headnorm=True, rope=rp_hd,
                            plain_tail=n_gv)
                qa, ka, va = _mla_up(zm, seq, mla_g_cq[e], mla_g_ckv[e], wuq, wkn, wv, rp_mla,
                                     MLA_SCALE * LOG2_E)
                n = xs.shape[0] // bsz
                return [a.reshape(bsz, n, -1) for a in (qa, ka, va, qkv)]

            qa, ka, va, qkv = project(x_lat, mod_l, rope_mla, rope_hd)
            qa_c, ka_c, va_c, qkv_c = project(x_ctx, mod_c, None, None)
            mla = dict(heads=MLA_HEADS, group=1, dqk=2 * LANES, dv=MLA_V, q_col0=0, k_col0=0, v_col0=0)
            gqa = dict(heads=GQA_HEADS, group=GQA_HEADS // GQA_KV_HEADS, dqk=GQA_HEAD_DIM, dv=GQA_HEAD_DIM,
                       q_col0=0, k_col0=GQA_HEADS * GQA_HEAD_DIM, v_col0=n_gqk)
            ya = _attention(qa, ka_c, va_c, ka, va, **mla)
            yb = _attention(qkv, qkv_c, qkv_c, qkv, qkv, **gqa)
            ys_lat = [ya.reshape(bsz * seq, -1), yb.reshape(bsz * seq, -1)]
            if need_ctx:
                ya_c = _attention(qa_c, ka_c, va_c, None, None, **mla, hps=MLA_HEADS)
                yb_c = _attention(qkv_c, qkv_c, qkv_c, None, None, **gqa, hps=GQA_HEADS)
                ys_ctx = [ya_c.reshape(bsz * lc, -1), yb_c.reshape(bsz * lc, -1)]
        else:
            o = i // 2
            lambda_init = 0.8 - 0.6 * math.exp(-0.3 * i)
            w_in = diff_w_in[o].astype(BF16)
            n_qk = 2 * DIFF_HEADS * 2 * DIFF_HEAD_DIM
            w_o = diff_w_o[o].astype(BF16)
            lams = (diff_lq1[o], diff_lk1[o], diff_lq2[o], diff_lk2[o])
            qk_scale = jnp.concatenate([jnp.full((n_qk // 2,), DIFF_SCALE * LOG2_E, F32),
                                        jnp.ones((n_qk // 2,), F32)])[None]

            def project(xs, m, rp):
                qk = _proj(xs, m, 1, w_in, 0, n_qk, BF16, gain=qk_scale, rope=rp)
                v = _proj(xs, m, 1, w_in, n_qk, w_in.shape[1] - n_qk, BF16)
                n = xs.shape[0] // bsz
                return qk.reshape(bsz, n, -1), v.reshape(bsz, n, -1)

            qk, v = project(x_lat, mod_l, rope_hd)
            qk_c, v_c = project(x_ctx, mod_c, None)
            y = _diff_attention(qk, qk_c, v_c, qk, v, lams, diff_g_sub[o], lambda_init)
            ys_lat = [y.reshape(bsz * seq, -1)]
            if need_ctx:
                y_c = _diff_attention(qk_c, qk_c, v_c, None, None, lams, diff_g_sub[o], lambda_init)
                ys_ctx = [y_c.reshape(bsz * lc, -1)]

        x_lat = _outproj(ys_lat, w_o, x_lat, mod_l, 1, ln_g[i, 1], ln_b[i, 1], alpha)
        x_lat = ffn(x_lat, mod_l, 2, 1)
        if need_ctx:
            x_ctx = _outproj(ys_ctx, w_o, x_ctx, mod_c, 1, ln_g[i, 1], ln_b[i, 1], alpha)
            x_ctx = ffn(x_ctx, mod_c, 2, 1)
    return x_lat.reshape(bsz, seq, d)
```

```python
import functools
import math

import jax
import jax.numpy as jnp
from jax import lax
from jax.experimental import pallas as pl
from jax.experimental.pallas import tpu as pltpu

F32 = jnp.float32
BF16 = jnp.bfloat16

GRID_W = 64
ROPE_THETA = 10000.0
NORM_EPS = 1e-6
N_SUB = 3
HALF_STEP = 0.5
MLA_HEADS = 8
MLA_Q_RANK = 512
MLA_KV_RANK = 256
MLA_NOPE = 128
MLA_ROPE = 64
MLA_V = 128
MLA_SCALE = (MLA_NOPE + MLA_ROPE) ** -0.5
GQA_HEADS = 8
GQA_KV_HEADS = 2
GQA_HEAD_DIM = 128
GQA_SCALE = GQA_HEAD_DIM ** -0.5
DIFF_HEADS = 8
DIFF_HEAD_DIM = 128
DIFF_SCALE = DIFF_HEAD_DIM ** -0.5

LANES = 128
MOD_ROWS = 32
VMEM_LIMIT_BYTES = 56 * 1024 * 1024

TOKEN_TILE = 1024
OUT_TOKEN_TILE = 512
FF_TILE = 512
PROJ_TILE = 1536
ADA_TILE = 1024
Q_TILE = 2048
ROW_CHUNK = 256
ATTN_ROW_CHUNK = 512
ATTN_HEADS_PER_STEP = 4
DIFF_HEADS_PER_STEP = 2
COL_CHUNK = 512
LOG2_E = math.log2(math.e)


def _divisor_tile(n, pref, mult):
    if n <= pref:
        return n
    t = (pref // mult) * mult
    while t >= mult:
        if n % t == 0:
            return t
        t -= mult
    raise ValueError(f"no tile for {n} (pref {pref}, mult {mult})")


def _params(*sem):
    return pltpu.CompilerParams(dimension_semantics=sem, vmem_limit_bytes=VMEM_LIMIT_BYTES)


def _layer_norm(r, g, b):
    mu = jnp.mean(r, axis=-1, keepdims=True)
    d = r - mu
    var = jnp.mean(d * d, axis=-1, keepdims=True)
    return d * lax.rsqrt(var + NORM_EPS) * g + b


def _rms_norm(x, g):
    return x * lax.rsqrt(jnp.mean(x * x, axis=-1, keepdims=True) + NORM_EPS) * g


def _rope_block(blk, cos, sin):
    return blk * cos + pltpu.roll(blk, LANES // 2, 1) * sin


def _skewed_chunks(n, matmul_fn, finish_fn):
    nxt = matmul_fn(0)
    for c in range(n):
        cur, nxt = nxt, (matmul_fn(c + 1) if c + 1 < n else None)
        finish_fn(c, cur)


def _dot(a, b):
    return jnp.dot(a, b, preferred_element_type=F32)


def _dot_nt(a, b):
    return lax.dot_general(a, b, (((1,), (1,)), ((), ())), preferred_element_type=F32)


def _ada_kernel(c_ref, w_ref, b_ref, o_ref):
    c = c_ref[...]
    s = (c * jax.nn.sigmoid(c)).astype(BF16)
    o_ref[...] = _dot(s, w_ref[...].astype(BF16)) + b_ref[...]


def _ada(cc, w_ada, b_ada):
    depth, d, n = w_ada.shape
    tn = _divisor_tile(n, ADA_TILE, LANES)
    return pl.pallas_call(
        _ada_kernel,
        grid=(depth, n // tn),
        in_specs=[
            pl.BlockSpec((MOD_ROWS, d), lambda l, j: (0, 0)),
            pl.BlockSpec((None, d, tn), lambda l, j: (l, 0, j)),
            pl.BlockSpec((None, 1, tn), lambda l, j: (l, 0, j)),
        ],
        out_specs=pl.BlockSpec((None, MOD_ROWS, tn), lambda l, j: (l, 0, j)),
        out_shape=jax.ShapeDtypeStruct((depth, MOD_ROWS, n), F32),
        compiler_params=_params("arbitrary", "arbitrary"),
        name="ada_mod",
    )(cc, w_ada, b_ada.reshape(depth, 1, n))


class _Mod:
    def __init__(self, table, layer, batch, seq, is_ctx):
        self.table, self.layer, self.batch, self.seq, self.is_ctx = table, layer, batch, seq, is_ctx

    def spec(self, d, sub, kind, tm):
        base = self.layer * MOD_ROWS
        off = 3 * sub + kind
        if self.is_ctx:
            row = (base + self.batch) * 3 * N_SUB + off
            return pl.BlockSpec((None, 1, d), lambda i, *_: (row, 0, 0))
        tpb = self.seq // tm
        return pl.BlockSpec((None, 1, d), lambda i, *_: ((base + i // tpb) * 3 * N_SUB + off, 0, 0))


def _ffn_kernel(x_ref, shift_ref, scale_ref, gate_ref, w1_ref, w3_ref, w2_ref, g_ref, b_ref, o_ref,
                hb_ref, *, alpha, nf):
    f = pl.program_id(1)
    tm = x_ref.shape[0]

    def modulate(rows):
        hb_ref[rows, :] = (x_ref[rows, :] * (1.0 + scale_ref[...]) + shift_ref[...]).astype(BF16)

    def swiglu_chunk(rows):
        hb = hb_ref[rows, :]
        a = _dot(hb, w1_ref[...])
        u = _dot(hb, w3_ref[...])
        act = (a * jax.nn.sigmoid(a) * u).astype(BF16)
        return _dot(act, w2_ref[...])

    def post_norm(rows, y):
        r = alpha * x_ref[rows, :] + gate_ref[...] * (HALF_STEP * y)
        o_ref[rows, :] = _layer_norm(r, g_ref[...], b_ref[...])

    def step(first, last):
        rc = min(ROW_CHUNK, tm) if (first or last) else tm
        rows = lambda r: slice(r * rc, (r + 1) * rc)
        n = tm // rc
        if first:
            modulate(rows(0))

        def matmuls(r):
            if first and r + 1 < n:
                modulate(rows(r + 1))
            y = swiglu_chunk(rows(r))
            return y if first else o_ref[rows(r), :] + y

        def finish(r, y):
            if last:
                post_norm(rows(r), y)
            else:
                o_ref[rows(r), :] = y

        _skewed_chunks(n, matmuls, finish)

    if nf == 1:
        step(True, True)
        return
    pl.when(f == 0)(lambda: step(True, False))
    pl.when(f == nf - 1)(lambda: step(False, True))
    if nf > 2:
        pl.when(jnp.logical_and(f > 0, f < nf - 1))(lambda: step(False, False))


def _ffn(x, mod, sub, w1, w3, w2, half, g, b, alpha):
    t, d = x.shape
    f = w1.shape[-1]
    layer = mod.layer
    tm = _divisor_tile(t if mod.is_ctx else mod.seq, TOKEN_TILE, 8)
    tf = _divisor_tile(f, FF_TILE, LANES)
    nf = f // tf
    row = pl.BlockSpec((tm, d), lambda i, k: (i, 0))
    vec = pl.BlockSpec((1, d), lambda i, k: (0, 0))
    return pl.pallas_call(
        functools.partial(_ffn_kernel, alpha=alpha, nf=nf),
        grid=(t // tm, nf),
        in_specs=[
            row,
            mod.spec(d, sub, 0, tm), mod.spec(d, sub, 1, tm), mod.spec(d, sub, 2, tm),
            pl.BlockSpec((None, None, d, tf), lambda i, k: (layer, half, 0, k)),
            pl.BlockSpec((None, None, d, tf), lambda i, k: (layer, half, 0, k)),
            pl.BlockSpec((None, None, tf, d), lambda i, k: (layer, half, k, 0)),
            vec, vec,
        ],
        out_specs=row,
        out_shape=jax.ShapeDtypeStruct((t, d), F32),
        scratch_shapes=[pltpu.VMEM((tm, d), BF16)],
        compiler_params=_params("parallel", "arbitrary"),
        name="ffn_half_step",
    )(x, mod.table, mod.table, mod.table, w1, w3, w2, g.reshape(1, d), b.reshape(1, d))


def _proj_kernel(*refs, headnorm, colscale, rope, plain_tail):
    x_ref, shift_ref, scale_ref, w_ref = refs[:4]
    rest = list(refs[4:])
    gain_ref = rest.pop(0) if (headnorm or colscale) else None
    cos_ref, sin_ref = (rest.pop(0), rest.pop(0)) if rope else (None, None)
    o_ref, hb_ref = rest

    @pl.when(pl.program_id(1) == 0)
    def _():
        hb_ref[...] = (x_ref[...] * (1.0 + scale_ref[...]) + shift_ref[...]).astype(BF16)

    if not (headnorm or colscale or rope):
        o_ref[...] = _dot(hb_ref[...], w_ref[...]).astype(o_ref.dtype)
        return
    tn = o_ref.shape[1]
    cw = next(w for w in (COL_CHUNK, COL_CHUNK // 2, LANES) if tn % w == 0)

    def finish(c, z):
        col0 = [c * cw + kb * LANES for kb in range(cw // LANES)]
        hw = z[0].shape[1]
        blks = [z[(kb * LANES) // hw][:, (kb * LANES) % hw:(kb * LANES) % hw + LANES] for kb in range(cw // LANES)]
        live = [s < tn - plain_tail for s in col0]
        if headnorm:
            ms = [jnp.mean(b * b, axis=-1, keepdims=True) if on else None for b, on in zip(blks, live)]
            rs = [lax.rsqrt(m + NORM_EPS) if on else None for m, on in zip(ms, live)]
            blks = [b * r * gain_ref[:, s:s + LANES] if on else b for b, r, s, on in zip(blks, rs, col0, live)]
        elif colscale:
            blks = [b * gain_ref[:, s:s + LANES] if on else b for b, s, on in zip(blks, col0, live)]
        if rope:
            rolled = [pltpu.roll(b, LANES // 2, 1) if on else None for b, on in zip(blks, live)]
            blks = [b * cos_ref[...] + r * sin_ref[...] if on else b for b, r, on in zip(blks, rolled, live)]
        for b, s in zip(blks, col0):
            o_ref[:, s:s + LANES] = b.astype(o_ref.dtype)

    nh = 2 if cw >= 2 * LANES else 1
    hw = cw // nh
    _skewed_chunks(
        tn // cw,
        lambda c: [_dot(hb_ref[...], w_ref[:, c * cw + h * hw:c * cw + (h + 1) * hw]) for h in range(nh)],
        finish)


def _proj(x, mod, sub, w, col0, ncols, out_dtype, gain=None, headnorm=False, rope=None, plain_tail=0):
    t, d = x.shape
    tm = _divisor_tile(t if mod.is_ctx else mod.seq, TOKEN_TILE, 8)
    tn = _divisor_tile(ncols, PROJ_TILE, LANES)
    assert col0 % tn == 0 and (plain_tail == 0 or tn == ncols)
    joff = col0 // tn
    in_specs = [
        pl.BlockSpec((tm, d), lambda i, j: (i, 0)),
        mod.spec(d, sub, 0, tm), mod.spec(d, sub, 1, tm),
        pl.BlockSpec((d, tn), lambda i, j: (0, j + joff)),
    ]
    args = [x, mod.table, mod.table, w]
    if gain is not None:
        in_specs.append(pl.BlockSpec((1, tn), lambda i, j: (0, j)))
        args.append(gain)
    if rope is not None:
        tpb = mod.seq // tm
        tab = pl.BlockSpec((tm, LANES), lambda i, j: (i % tpb, 0))
        in_specs += [tab, tab]
        args += list(rope)
    return pl.pallas_call(
        functools.partial(_proj_kernel, headnorm=headnorm, colscale=gain is not None and not headnorm,
                          rope=rope is not None, plain_tail=plain_tail),
        grid=(t // tm, ncols // tn),
        in_specs=in_specs,
        out_specs=pl.BlockSpec((tm, tn), lambda i, j: (i, j)),
        out_shape=jax.ShapeDtypeStruct((t, ncols), out_dtype),
        scratch_shapes=[pltpu.VMEM((tm, d), BF16)],
        compiler_params=_params("parallel", "arbitrary"),
        name="mixer_in_proj",
    )(*args)


def _mla_up_kernel(*refs, rope, qscale):
    zm_ref, gcq_ref, gckv_ref, wuq_ref, wkn_ref, wv_ref = refs[:6]
    rest = list(refs[6:])
    cos_ref, sin_ref = (rest.pop(0), rest.pop(0)) if rope else (None, None)
    qa_ref, ka_ref, va_ref = rest
    hd = 2 * LANES

    def rot(blk):
        return _rope_block(blk, cos_ref[...], sin_ref[...]) if rope else blk

    nq = _rms_norm(zm_ref[:, :MLA_Q_RANK], gcq_ref[...]).astype(BF16)
    q = _dot(nq, wuq_ref[...]) * qscale
    for h in range(MLA_HEADS):
        qa_ref[:, h * hd:h * hd + LANES] = q[:, h * hd:h * hd + LANES].astype(BF16)
        qa_ref[:, h * hd + LANES:(h + 1) * hd] = rot(q[:, h * hd + LANES:(h + 1) * hd]).astype(BF16)

    nkv = _rms_norm(zm_ref[:, MLA_Q_RANK:MLA_Q_RANK + MLA_KV_RANK], gckv_ref[...]).astype(BF16)
    kn = _dot(nkv, wkn_ref[...])
    va_ref[...] = _dot(nkv, wv_ref[...]).astype(BF16)
    kr = rot(zm_ref[:, MLA_Q_RANK + MLA_KV_RANK:MLA_Q_RANK + MLA_KV_RANK + LANES]).astype(BF16)
    for h in range(MLA_HEADS):
        ka_ref[:, h * hd:h * hd + LANES] = kn[:, h * LANES:(h + 1) * LANES].astype(BF16)
        ka_ref[:, h * hd + LANES:(h + 1) * hd] = kr


def _mla_up(zm, seq, g_cq, g_ckv, wuq, wkn, wv, rope, qscale):
    t, nz = zm.shape
    tm = _divisor_tile(seq if rope is not None else t, TOKEN_TILE, 8)
    hq = MLA_HEADS * 2 * LANES
    hv = MLA_HEADS * MLA_V
    full = lambda a: pl.BlockSpec(a.shape, lambda i: (0, 0))
    g_cq = g_cq.reshape(1, -1)
    g_ckv = g_ckv.reshape(1, -1)
    in_specs = [pl.BlockSpec((tm, nz), lambda i: (i, 0)), full(g_cq), full(g_ckv), full(wuq), full(wkn), full(wv)]
    args = [zm, g_cq, g_ckv, wuq, wkn, wv]
    if rope is not None:
        tpb = seq // tm
        tab = pl.BlockSpec((tm, LANES), lambda i: (i % tpb, 0))
        in_specs += [tab, tab]
        args += list(rope)
    return pl.pallas_call(
        functools.partial(_mla_up_kernel, rope=rope is not None, qscale=qscale),
        grid=(t // tm,),
        in_specs=in_specs,
        out_specs=[pl.BlockSpec((tm, hq), lambda i: (i, 0)), pl.BlockSpec((tm, hq), lambda i: (i, 0)),
                   pl.BlockSpec((tm, hv), lambda i: (i, 0))],
        out_shape=[jax.ShapeDtypeStruct((t, hq), BF16), jax.ShapeDtypeStruct((t, hq), BF16),
                   jax.ShapeDtypeStruct((t, hv), BF16)],
        compiler_params=_params("parallel"),
        name="mla_up_proj",
    )(*args)


def _raw_scores(q, kc_ref, kl_ref, kcols):
    sc = _dot_nt(q, kc_ref[:, kcols])
    sl = None if kl_ref is None else _dot_nt(q, kl_ref[:, kcols])
    return sc, sl


def _softmax_numerators(scores):
    sc, sl = scores
    m = jnp.max(sc, axis=-1, keepdims=True)
    if sl is None:
        return jnp.exp2(sc - m), None
    m = jnp.maximum(m, jnp.max(sl, axis=-1, keepdims=True))
    return jnp.exp2(sc - m), jnp.exp2(sl - m)


def _row_sum(ec, el):
    den = jnp.sum(ec, axis=-1, keepdims=True)
    return den if el is None else den + jnp.sum(el, axis=-1, keepdims=True)


def _attn_kernel(*refs, has_lat, row_chunk, hps, group, dqk, dv):
    if has_lat:
        q_ref, kc_ref, vc_ref, kl_ref, vl_ref, o_ref = refs
    else:
        (q_ref, kc_ref, vc_ref, o_ref), kl_ref, vl_ref = refs, None, None
    nr = q_ref.shape[0] // row_chunk
    rows = lambda r: slice(r * row_chunk, (r + 1) * row_chunk)
    with_ones = lambda v: jnp.concatenate([v, jnp.ones_like(v)], axis=1)
    n_kv = vc_ref.shape[1] // dv
    vc1 = [with_ones(vc_ref[:, g * dv:(g + 1) * dv]) for g in range(n_kv)]
    vl1 = [with_ones(vl_ref[:, g * dv:(g + 1) * dv]) for g in range(n_kv)] if has_lat else None

    def scores(c):
        h, r = divmod(c, nr)
        g = h // group
        return _raw_scores(q_ref[rows(r), h * dqk:(h + 1) * dqk], kc_ref, kl_ref, slice(g * dqk, (g + 1) * dqk))

    def finish(c, s):
        h, r = divmod(c, nr)
        g = h // group
        ec, el = _softmax_numerators(s)
        o = _dot(ec.astype(BF16), vc1[g])
        if has_lat:
            o = o + _dot(el.astype(BF16), vl1[g])
        o_ref[rows(r), h * dv:(h + 1) * dv] = (o[:, :dv] / o[:, dv:]).astype(o_ref.dtype)

    _skewed_chunks(hps * nr, scores, finish)


def _attention(q, kc, vc, kl, vl, *, heads, group, dqk, dv, q_col0, k_col0, v_col0, hps=1):
    assert dv == LANES and heads % hps == 0 and (hps == 1 or hps % group == 0)
    b, lq, _ = q.shape
    lc = kc.shape[1]
    tq = _divisor_tile(lq, Q_TILE, 8)
    n_kv = max(hps // group, 1)
    kv_of = (lambda h: h // group) if hps == 1 else (lambda h: h)
    qw, kw, vw = hps * dqk, n_kv * dqk, n_kv * dv
    assert q_col0 % qw == 0 and k_col0 % kw == 0 and v_col0 % vw == 0
    qo, ko, vo = q_col0 // qw, k_col0 // kw, v_col0 // vw
    in_specs = [
        pl.BlockSpec((None, tq, qw), lambda bi, h, i: (bi, i, qo + h)),
        pl.BlockSpec((None, lc, kw), lambda bi, h, i: (bi, 0, ko + kv_of(h))),
        pl.BlockSpec((None, lc, vw), lambda bi, h, i: (bi, 0, vo + kv_of(h))),
    ]
    args = [q, kc, vc]
    if kl is not None:
        ll = kl.shape[1]
        in_specs += [
            pl.BlockSpec((None, ll, kw), lambda bi, h, i: (bi, 0, ko + kv_of(h))),
            pl.BlockSpec((None, ll, vw), lambda bi, h, i: (bi, 0, vo + kv_of(h))),
        ]
        args += [kl, vl]
    return pl.pallas_call(
        functools.partial(_attn_kernel, has_lat=kl is not None, row_chunk=min(ATTN_ROW_CHUNK, tq), hps=hps,
                          group=group if hps > 1 else hps, dqk=dqk, dv=dv),
        grid=(b, heads // hps, lq // tq),
        in_specs=in_specs,
        out_specs=pl.BlockSpec((None, tq, hps * dv), lambda bi, h, i: (bi, i, h)),
        out_shape=jax.ShapeDtypeStruct((b, lq, heads * dv), BF16),
        compiler_params=_params("parallel", "arbitrary", "arbitrary"),
        name="softmax_attention",
    )(*args)


def _diff_attn_kernel(*refs, lambda_init, has_lat, row_chunk, hps):
    if has_lat:
        q_ref, kc_ref, vc_ref, kl_ref, vl_ref, lq1, lk1, lq2, lk2, gs_ref, o_ref = refs
    else:
        (q_ref, kc_ref, vc_ref, lq1, lk1, lq2, lk2, gs_ref, o_ref), kl_ref, vl_ref = refs, None, None
    lam = (jnp.exp(jnp.sum(lq1[...] * lk1[...], axis=-1, keepdims=True))
           - jnp.exp(jnp.sum(lq2[...] * lk2[...], axis=-1, keepdims=True)) + lambda_init)
    hd = DIFF_HEAD_DIM
    nr = q_ref.shape[0] // row_chunk
    rows = lambda r: slice(r * row_chunk, (r + 1) * row_chunk)

    def scores(c):
        h, r = divmod(c, nr)
        c0, c1 = slice(2 * h * hd, (2 * h + 1) * hd), slice((2 * h + 1) * hd, (2 * h + 2) * hd)
        return (_raw_scores(q_ref[rows(r), c0], kc_ref, kl_ref, c0),
                _raw_scores(q_ref[rows(r), c1], kc_ref, kl_ref, c1))

    def finish(c, s01):
        h, r = divmod(c, nr)
        vcols = slice(2 * h * hd, (2 * h + 2) * hd)
        ec0, el0 = _softmax_numerators(s01[0])
        ec1, el1 = _softmax_numerators(s01[1])
        den0 = _row_sum(ec0, el0)
        ratio = lam * den0 / _row_sum(ec1, el1)
        y = _dot((ec0 - ec1 * ratio).astype(BF16), vc_ref[:, vcols])
        if has_lat:
            y = y + _dot((el0 - el1 * ratio).astype(BF16), vl_ref[:, vcols])
        y = y / den0
        o_ref[rows(r), vcols] = (_rms_norm(y, gs_ref[...]) * (1.0 - lambda_init)).astype(o_ref.dtype)

    _skewed_chunks(hps * nr, scores, finish)


def _diff_attention(q, kc, vc, kl, vl, lams, g_sub, lambda_init, hps=1):
    assert DIFF_HEADS % hps == 0
    b, lq, _ = q.shape
    lc = kc.shape[1]
    tq = _divisor_tile(lq, Q_TILE, 8)
    w = hps * 2 * DIFF_HEAD_DIM
    steps = DIFF_HEADS // hps
    in_specs = [
        pl.BlockSpec((None, tq, w), lambda bi, h, i: (bi, i, h)),
        pl.BlockSpec((None, lc, w), lambda bi, h, i: (bi, 0, steps + h)),
        pl.BlockSpec((None, lc, w), lambda bi, h, i: (bi, 0, h)),
    ]
    args = [q, kc, vc]
    if kl is not None:
        ll = kl.shape[1]
        in_specs += [
            pl.BlockSpec((None, ll, w), lambda bi, h, i: (bi, 0, steps + h)),
            pl.BlockSpec((None, ll, w), lambda bi, h, i: (bi, 0, h)),
        ]
        args += [kl, vl]
    small = lambda n: pl.BlockSpec((1, n), lambda bi, h, i: (0, 0))
    in_specs += [small(DIFF_HEAD_DIM)] * 4 + [small(2 * DIFF_HEAD_DIM)]
    args += [a.reshape(1, -1) for a in lams] + [g_sub.reshape(1, -1)]
    return pl.pallas_call(
        functools.partial(_diff_attn_kernel, lambda_init=lambda_init, has_lat=kl is not None,
                          row_chunk=min(ATTN_ROW_CHUNK, tq), hps=hps),
        grid=(b, steps, lq // tq),
        in_specs=in_specs,
        out_specs=pl.BlockSpec((None, tq, w), lambda bi, h, i: (bi, i, h)),
        out_shape=jax.ShapeDtypeStruct((b, lq, steps * w), BF16),
        compiler_params=_params("parallel", "arbitrary", "arbitrary"),
        name="diff_attention",
    )(*args)


def _outproj_kernel(*refs, nparts, alpha):
    y_refs = refs[:nparts]
    w_refs = refs[nparts:2 * nparts]
    x_ref, gate_ref, g_ref, b_ref, o_ref = refs[2 * nparts:]
    row_chunk = min(ROW_CHUNK, o_ref.shape[0])
    rows = lambda r: slice(r * row_chunk, (r + 1) * row_chunk)

    def project(r):
        y = _dot(y_refs[0][rows(r), :], w_refs[0][...])
        for p in range(1, nparts):
            y = y + _dot(y_refs[p][rows(r), :], w_refs[p][...])
        return y

    def finish(r, y):
        res = alpha * x_ref[rows(r), :] + gate_ref[...] * y
        o_ref[rows(r), :] = _layer_norm(res, g_ref[...], b_ref[...])

    _skewed_chunks(o_ref.shape[0] // row_chunk, project, finish)


def _outproj(ys, w_o, x, mod, sub, g, b, alpha):
    t, d = x.shape
    tm = _divisor_tile(t if mod.is_ctx else mod.seq, OUT_TOKEN_TILE, 8)
    kp = ys[0].shape[1]
    assert all(y.shape[1] == kp for y in ys) and kp * len(ys) == w_o.shape[0]
    vec = pl.BlockSpec((1, d), lambda i: (0, 0))
    in_specs = [pl.BlockSpec((tm, kp), lambda i: (i, 0)) for _ in ys]
    in_specs += [pl.BlockSpec((kp, d), functools.partial(lambda i, p: (p, 0), p=p), pipeline_mode=pl.Buffered(1))
                 for p in range(len(ys))]
    in_specs += [pl.BlockSpec((tm, d), lambda i: (i, 0)), mod.spec(d, sub, 2, tm), vec, vec]
    return pl.pallas_call(
        functools.partial(_outproj_kernel, nparts=len(ys), alpha=alpha),
        grid=(t // tm,),
        in_specs=in_specs,
        out_specs=pl.BlockSpec((tm, d), lambda i: (i, 0)),
        out_shape=jax.ShapeDtypeStruct((t, d), F32),
        compiler_params=_params("parallel"),
        name="mixer_out_proj",
    )(*ys, *([w_o] * len(ys)), x, mod.table, g.reshape(1, d), b.reshape(1, d))


def _rope_tables(rows, rot_dim):
    r, col = jnp.meshgrid(jnp.arange(rows, dtype=F32), jnp.arange(GRID_W, dtype=F32), indexing="ij")
    n_freq = rot_dim // 4
    inv_freq = ROPE_THETA ** (-jnp.arange(n_freq, dtype=F32) / n_freq)
    ang = jnp.concatenate([r.reshape(-1, 1) * inv_freq, col.reshape(-1, 1) * inv_freq], -1)
    cos, sin = jnp.cos(ang), jnp.sin(ang)
    pad = LANES // 2 - rot_dim // 2
    cos = jnp.pad(cos, ((0, 0), (0, pad)), constant_values=1.0)
    sin = jnp.pad(sin, ((0, 0), (0, pad)))
    return jnp.concatenate([cos, cos], -1), jnp.concatenate([-sin, sin], -1)


def _spread_rot(w):
    half = MLA_ROPE // 2
    z = jnp.zeros((w.shape[0], LANES // 2 - half), w.dtype)
    return jnp.concatenate([w[:, :half], z, w[:, half:], z], -1)


def kernel(x, c, ctx, c_ctx, w_ada, b_ada, ln_g, ln_b, ffn_w1, ffn_w3, ffn_w2, mg_w_in, mla_g_cq, mla_g_ckv,
           mla_w_uq, mla_w_ukv, gqa_g_q, gqa_g_k, mg_w_o, diff_w_in, diff_lq1, diff_lk1, diff_lq2, diff_lk2,
           diff_g_sub, diff_w_o):
    bsz, seq, d = x.shape
    lc = ctx.shape[1]
    depth = w_ada.shape[0]
    assert bsz + 1 <= MOD_ROWS and seq % GRID_W == 0
    alpha = (2 * depth) ** 0.25
    rows = seq // GRID_W
    rope_mla = _rope_tables(rows, MLA_ROPE)
    rope_hd = _rope_tables(rows, GQA_HEAD_DIM)

    cc = jnp.concatenate([c, c_ctx[None], jnp.zeros((MOD_ROWS - bsz - 1, d), F32)], 0)
    mod_table = _ada(cc, w_ada, b_ada).reshape(depth * MOD_ROWS * 3 * N_SUB, 1, d)

    x_lat = x.reshape(bsz * seq, d)
    x_ctx = ctx.reshape(bsz * lc, d)
    w1b, w3b, w2b = ffn_w1.astype(BF16), ffn_w3.astype(BF16), ffn_w2.astype(BF16)

    for i in range(depth):
        need_ctx = i < depth - 1
        mod_l = _Mod(mod_table, i, bsz, seq, False)
        mod_c = _Mod(mod_table, i, bsz, seq, True)
        ffn = lambda xs, m, sub, k: _ffn(xs, m, sub, w1b, w3b, w2b, k, ln_g[i, sub], ln_b[i, sub], alpha)

        x_lat = ffn(x_lat, mod_l, 0, 0)
        x_ctx = ffn(x_ctx, mod_c, 0, 0)

        if i % 2 == 0:
            e = i // 2
            w_in = mg_w_in[e]
            o_kr = MLA_Q_RANK + MLA_KV_RANK
            o_gq = o_kr + MLA_ROPE
            n_gqk = (GQA_HEADS + GQA_KV_HEADS) * GQA_HEAD_DIM
            w_mla = jnp.concatenate([w_in[:, :o_kr], _spread_rot(w_in[:, o_kr:o_gq])], -1).astype(BF16)
            w_gqa = w_in[:, o_gq:].astype(BF16)
            wuq = mla_w_uq[e].reshape(MLA_Q_RANK, MLA_HEADS, MLA_NOPE + MLA_ROPE)
            wuq = jnp.concatenate(
                [wuq[:, :, :MLA_NOPE],
                 _spread_rot(wuq[:, :, MLA_NOPE:].reshape(-1, MLA_ROPE)).reshape(MLA_Q_RANK, MLA_HEADS, LANES)],
                -1).reshape(MLA_Q_RANK, MLA_HEADS * 2 * LANES).astype(BF16)
            wukv = mla_w_ukv[e].reshape(MLA_KV_RANK, MLA_HEADS, MLA_NOPE + MLA_V)
            wkn = wukv[:, :, :MLA_NOPE].reshape(MLA_KV_RANK, -1).astype(BF16)
            wv = wukv[:, :, MLA_NOPE:].reshape(MLA_KV_RANK, -1).astype(BF16)
            n_gv = GQA_KV_HEADS * GQA_HEAD_DIM
            gain = jnp.concatenate([jnp.tile(gqa_g_q[e] * (GQA_SCALE * LOG2_E), GQA_HEADS),
                                    jnp.tile(gqa_g_k[e], GQA_KV_HEADS), jnp.ones((n_gv,), F32)])[None]
            w_o = mg_w_o[e].astype(BF16)

            def project(xs, m, rp_mla, rp_hd):
                zm = _proj(xs, m, 1, w_mla, 0, w_mla.shape[1], F32)
                qkv = _proj(xs, m, 1, w_gqa, 0, n_gqk + n_gv, BF16, gain=gain, headnorm=True, rope=rp_hd,
                            plain_tail=n_gv)
                qa, ka, va = _mla_up(zm, seq, mla_g_cq[e], mla_g_ckv[e], wuq, wkn, wv, rp_mla,
                                     MLA_SCALE * LOG2_E)
                n = xs.shape[0] // bsz
                return [a.reshape(bsz, n, -1) for a in (qa, ka, va, qkv)]

            qa, ka, va, qkv = project(x_lat, mod_l, rope_mla, rope_hd)
            qa_c, ka_c, va_c, qkv_c = project(x_ctx, mod_c, None, None)
            mla = dict(heads=MLA_HEADS, group=1, dqk=2 * LANES, dv=MLA_V, q_col0=0, k_col0=0, v_col0=0)
            gqa = dict(heads=GQA_HEADS, group=GQA_HEADS // GQA_KV_HEADS, dqk=GQA_HEAD_DIM, dv=GQA_HEAD_DIM,
                       q_col0=0, k_col0=GQA_HEADS * GQA_HEAD_DIM, v_col0=n_gqk)
            ya = _attention(qa, ka_c, va_c, ka, va, **mla, hps=ATTN_HEADS_PER_STEP)
            yb = _attention(qkv, qkv_c, qkv_c, qkv, qkv, **gqa, hps=ATTN_HEADS_PER_STEP)
            ys_lat = [ya.reshape(bsz * seq, -1), yb.reshape(bsz * seq, -1)]
            if need_ctx:
                ya_c = _attention(qa_c, ka_c, va_c, None, None, **mla, hps=MLA_HEADS)
                yb_c = _attention(qkv_c, qkv_c, qkv_c, None, None, **gqa, hps=GQA_HEADS)
                ys_ctx = [ya_c.reshape(bsz * lc, -1), yb_c.reshape(bsz * lc, -1)]
        else:
            o = i // 2
            lambda_init = 0.8 - 0.6 * math.exp(-0.3 * i)
            w_in = diff_w_in[o].astype(BF16)
            n_qk = 2 * DIFF_HEADS * 2 * DIFF_HEAD_DIM
            w_o = diff_w_o[o].astype(BF16)
            lams = (diff_lq1[o], diff_lk1[o], diff_lq2[o], diff_lk2[o])
            qk_scale = jnp.concatenate([jnp.full((n_qk // 2,), DIFF_SCALE * LOG2_E, F32),
                                        jnp.ones((n_qk // 2,), F32)])[None]

            def project(xs, m, rp):
                qk = _proj(xs, m, 1, w_in, 0, n_qk, BF16, gain=qk_scale, rope=rp)
                v = _proj(xs, m, 1, w_in, n_qk, w_in.shape[1] - n_qk, BF16)
                n = xs.shape[0] // bsz
                return qk.reshape(bsz, n, -1), v.reshape(bsz, n, -1)

            qk, v = project(x_lat, mod_l, rope_hd)
            qk_c, v_c = project(x_ctx, mod_c, None)
            y = _diff_attention(qk, qk_c, v_c, qk, v, lams, diff_g_sub[o], lambda_init, hps=DIFF_HEADS_PER_STEP)
            ys_lat = [y.reshape(bsz * seq, -1)]
            if need_ctx:
                y_c = _diff_attention(qk_c, qk_c, v_c, None, None, lams, diff_g_sub[o], lambda_init)
                ys_ctx = [y_c.reshape(bsz * lc, -1)]

        x_lat = _outproj(ys_lat, w_o, x_lat, mod_l, 1, ln_g[i, 1], ln_b[i, 1], alpha)
        x_lat = ffn(x_lat, mod_l, 2, 1)
        if need_ctx:
            x_ctx = _outproj(ys_ctx, w_o, x_ctx, mod_c, 1, ln_g[i, 1], ln_b[i, 1], alpha)
            x_ctx = ffn(x_ctx, mod_c, 2, 1)
    return x_lat.reshape(bsz, seq, d)
```

```python
import functools
import math

import jax
import jax.numpy as jnp
from jax import lax
from jax.experimental import pallas as pl
from jax.experimental.pallas import tpu as pltpu

F32 = jnp.float32
BF16 = jnp.bfloat16

GRID_W = 64
ROPE_THETA = 10000.0
NORM_EPS = 1e-6
N_SUB = 3
HALF_STEP = 0.5
MLA_HEADS = 8
MLA_Q_RANK = 512
MLA_KV_RANK = 256
MLA_NOPE = 128
MLA_ROPE = 64
MLA_V = 128
MLA_SCALE = (MLA_NOPE + MLA_ROPE) ** -0.5
GQA_HEADS = 8
GQA_KV_HEADS = 2
GQA_HEAD_DIM = 128
GQA_SCALE = GQA_HEAD_DIM ** -0.5
DIFF_HEADS = 8
DIFF_HEAD_DIM = 128
DIFF_SCALE = DIFF_HEAD_DIM ** -0.5

LANES = 128
MOD_ROWS = 32
VMEM_LIMIT_BYTES = 56 * 1024 * 1024

TOKEN_TILE = 1024
OUT_TOKEN_TILE = 512
FF_TILE = 512
PROJ_TILE = 2048
ADA_TILE = 1024
Q_TILE = 2048
ROW_CHUNK = 256
ATTN_ROW_CHUNK = 512
ATTN_HEADS_PER_STEP = 4
DIFF_HEADS_PER_STEP = 2
COL_CHUNK = 512
LOG2_E = math.log2(math.e)


def _divisor_tile(n, pref, mult):
    if n <= pref:
        return n
    t = (pref // mult) * mult
    while t >= mult:
        if n % t == 0:
            return t
        t -= mult
    raise ValueError(f"no tile for {n} (pref {pref}, mult {mult})")


def _params(*sem):
    return pltpu.CompilerParams(dimension_semantics=sem, vmem_limit_bytes=VMEM_LIMIT_BYTES)


def _layer_norm(r, g, b):
    mu = jnp.mean(r, axis=-1, keepdims=True)
    d = r - mu
    var = jnp.mean(d * d, axis=-1, keepdims=True)
    return d * lax.rsqrt(var + NORM_EPS) * g + b


def _rms_norm(x, g):
    return x * lax.rsqrt(jnp.mean(x * x, axis=-1, keepdims=True) + NORM_EPS) * g


def _rope_block(blk, cos, sin):
    return blk * cos + pltpu.roll(blk, LANES // 2, 1) * sin


def _skewed_chunks(n, matmul_fn, finish_fn):
    nxt = matmul_fn(0)
    for c in range(n):
        cur, nxt = nxt, (matmul_fn(c + 1) if c + 1 < n else None)
        finish_fn(c, cur)


def _dot(a, b):
    return jnp.dot(a, b, preferred_element_type=F32)


def _dot_nt(a, b):
    return lax.dot_general(a, b, (((1,), (1,)), ((), ())), preferred_element_type=F32)


def _ada_kernel(c_ref, w_ref, b_ref, o_ref):
    c = c_ref[...]
    s = (c * jax.nn.sigmoid(c)).astype(BF16)
    o_ref[...] = _dot(s, w_ref[...].astype(BF16)) + b_ref[...]


def _ada(cc, w_ada, b_ada):
    depth, d, n = w_ada.shape
    tn = _divisor_tile(n, ADA_TILE, LANES)
    return pl.pallas_call(
        _ada_kernel,
        grid=(depth, n // tn),
        in_specs=[
            pl.BlockSpec((MOD_ROWS, d), lambda l, j: (0, 0)),
            pl.BlockSpec((None, d, tn), lambda l, j: (l, 0, j)),
            pl.BlockSpec((None, 1, tn), lambda l, j: (l, 0, j)),
        ],
        out_specs=pl.BlockSpec((None, MOD_ROWS, tn), lambda l, j: (l, 0, j)),
        out_shape=jax.ShapeDtypeStruct((depth, MOD_ROWS, n), F32),
        compiler_params=_params("arbitrary", "arbitrary"),
        name="ada_mod",
    )(cc, w_ada, b_ada.reshape(depth, 1, n))


class _Mod:
    def __init__(self, table, layer, batch, seq, is_ctx):
        self.table, self.layer, self.batch, self.seq, self.is_ctx = table, layer, batch, seq, is_ctx

    def spec(self, d, sub, kind, tm):
        base = self.layer * MOD_ROWS
        off = 3 * sub + kind
        if self.is_ctx:
            row = (base + self.batch) * 3 * N_SUB + off
            return pl.BlockSpec((None, 1, d), lambda i, *_: (row, 0, 0))
        tpb = self.seq // tm
        return pl.BlockSpec((None, 1, d), lambda i, *_: ((base + i // tpb) * 3 * N_SUB + off, 0, 0))


def _ffn_kernel(x_ref, shift_ref, scale_ref, gate_ref, w1_ref, w3_ref, w2_ref, g_ref, b_ref, o_ref,
                hb_ref, *, alpha, nf):
    f = pl.program_id(1)
    tm = x_ref.shape[0]

    def modulate(rows):
        hb_ref[rows, :] = (x_ref[rows, :] * (1.0 + scale_ref[...]) + shift_ref[...]).astype(BF16)

    def swiglu_chunk(rows):
        hb = hb_ref[rows, :]
        a = _dot(hb, w1_ref[...])
        u = _dot(hb, w3_ref[...])
        act = (a * jax.nn.sigmoid(a) * u).astype(BF16)
        return _dot(act, w2_ref[...])

    def post_norm(rows, y):
        r = alpha * x_ref[rows, :] + gate_ref[...] * (HALF_STEP * y)
        o_ref[rows, :] = _layer_norm(r, g_ref[...], b_ref[...])

    def step(first, last):
        rc = min(2 * ROW_CHUNK if first and not last else ROW_CHUNK, tm) if (first or last) else tm
        rows = lambda r: slice(r * rc, (r + 1) * rc)
        n = tm // rc
        if first:
            modulate(rows(0))

        def matmuls(r):
            if first and r + 1 < n:
                modulate(rows(r + 1))
            y = swiglu_chunk(rows(r))
            return y if first else o_ref[rows(r), :] + y

        def finish(r, y):
            if last:
                post_norm(rows(r), y)
            else:
                o_ref[rows(r), :] = y

        _skewed_chunks(n, matmuls, finish)

    if nf == 1:
        step(True, True)
        return
    pl.when(f == 0)(lambda: step(True, False))
    pl.when(f == nf - 1)(lambda: step(False, True))
    if nf > 2:
        pl.when(jnp.logical_and(f > 0, f < nf - 1))(lambda: step(False, False))


def _ffn(x, mod, sub, w1, w3, w2, half, g, b, alpha):
    t, d = x.shape
    f = w1.shape[-1]
    layer = mod.layer
    tm = _divisor_tile(t if mod.is_ctx else mod.seq, TOKEN_TILE, 8)
    tf = _divisor_tile(f, FF_TILE, LANES)
    nf = f // tf
    row = pl.BlockSpec((tm, d), lambda i, k: (i, 0))
    vec = pl.BlockSpec((1, d), lambda i, k: (0, 0))
    return pl.pallas_call(
        functools.partial(_ffn_kernel, alpha=alpha, nf=nf),
        grid=(t // tm, nf),
        in_specs=[
            row,
            mod.spec(d, sub, 0, tm), mod.spec(d, sub, 1, tm), mod.spec(d, sub, 2, tm),
            pl.BlockSpec((None, None, d, tf), lambda i, k: (layer, half, 0, k)),
            pl.BlockSpec((None, None, d, tf), lambda i, k: (layer, half, 0, k)),
            pl.BlockSpec((None, None, tf, d), lambda i, k: (layer, half, k, 0)),
            vec, vec,
        ],
        out_specs=row,
        out_shape=jax.ShapeDtypeStruct((t, d), F32),
        scratch_shapes=[pltpu.VMEM((tm, d), BF16)],
        compiler_params=_params("parallel", "arbitrary"),
        name="ffn_half_step",
    )(x, mod.table, mod.table, mod.table, w1, w3, w2, g.reshape(1, d), b.reshape(1, d))


def _proj_kernel(*refs, headnorm, colscale, rope, plain_tail):
    x_ref, shift_ref, scale_ref, w_ref = refs[:4]
    rest = list(refs[4:])
    gain_ref = rest.pop(0) if (headnorm or colscale) else None
    cos_ref, sin_ref = (rest.pop(0), rest.pop(0)) if rope else (None, None)
    o_ref, hb_ref = rest

    @pl.when(pl.program_id(1) == 0)
    def _():
        hb_ref[...] = (x_ref[...] * (1.0 + scale_ref[...]) + shift_ref[...]).astype(BF16)

    if not (headnorm or colscale or rope):
        o_ref[...] = _dot(hb_ref[...], w_ref[...]).astype(o_ref.dtype)
        return
    tn = o_ref.shape[1]
    cw = next(w for w in (COL_CHUNK, COL_CHUNK // 2, LANES) if tn % w == 0)

    def finish(c, z):
        col0 = [c * cw + kb * LANES for kb in range(cw // LANES)]
        hw = z[0].shape[1]
        blks = [z[(kb * LANES) // hw][:, (kb * LANES) % hw:(kb * LANES) % hw + LANES] for kb in range(cw // LANES)]
        live = [s < tn - plain_tail for s in col0]
        if headnorm:
            ms = [jnp.mean(b * b, axis=-1, keepdims=True) if on else None for b, on in zip(blks, live)]
            rs = [lax.rsqrt(m + NORM_EPS) if on else None for m, on in zip(ms, live)]
            blks = [b * r * gain_ref[:, s:s + LANES] if on else b for b, r, s, on in zip(blks, rs, col0, live)]
        elif colscale:
            blks = [b * gain_ref[:, s:s + LANES] if on else b for b, s, on in zip(blks, col0, live)]
        if rope:
            rolled = [pltpu.roll(b, LANES // 2, 1) if on else None for b, on in zip(blks, live)]
            blks = [b * cos_ref[...] + r * sin_ref[...] if on else b for b, r, on in zip(blks, rolled, live)]
        for b, s in zip(blks, col0):
            o_ref[:, s:s + LANES] = b.astype(o_ref.dtype)

    nh = 2 if cw >= 2 * LANES else 1
    hw = cw // nh
    _skewed_chunks(
        tn // cw,
        lambda c: [_dot(hb_ref[...], w_ref[:, c * cw + h * hw:c * cw + (h + 1) * hw]) for h in range(nh)],
        finish)


def _proj(x, mod, sub, w, col0, ncols, out_dtype, gain=None, headnorm=False, rope=None, plain_tail=0):
    t, d = x.shape
    tm = _divisor_tile(t if mod.is_ctx else mod.seq, TOKEN_TILE, 8)
    tn = _divisor_tile(ncols, PROJ_TILE, LANES)
    assert col0 % tn == 0 and (plain_tail == 0 or tn == ncols)
    joff = col0 // tn
    in_specs = [
        pl.BlockSpec((tm, d), lambda i, j: (i, 0)),
        mod.spec(d, sub, 0, tm), mod.spec(d, sub, 1, tm),
        pl.BlockSpec((d, tn), lambda i, j: (0, j + joff)),
    ]
    args = [x, mod.table, mod.table, w]
    if gain is not None:
        in_specs.append(pl.BlockSpec((1, tn), lambda i, j: (0, j)))
        args.append(gain)
    if rope is not None:
        tpb = mod.seq // tm
        tab = pl.BlockSpec((tm, LANES), lambda i, j: (i % tpb, 0))
        in_specs += [tab, tab]
        args += list(rope)
    return pl.pallas_call(
        functools.partial(_proj_kernel, headnorm=headnorm, colscale=gain is not None and not headnorm,
                          rope=rope is not None, plain_tail=plain_tail),
        grid=(t // tm, ncols // tn),
        in_specs=in_specs,
        out_specs=pl.BlockSpec((tm, tn), lambda i, j: (i, j)),
        out_shape=jax.ShapeDtypeStruct((t, ncols), out_dtype),
        scratch_shapes=[pltpu.VMEM((tm, d), BF16)],
        compiler_params=_params("parallel", "arbitrary"),
        name="mixer_in_proj",
    )(*args)


def _mla_up_kernel(*refs, rope, qscale):
    zm_ref, gcq_ref, gckv_ref, wuq_ref, wkn_ref, wv_ref = refs[:6]
    rest = list(refs[6:])
    cos_ref, sin_ref = (rest.pop(0), rest.pop(0)) if rope else (None, None)
    qa_ref, ka_ref, va_ref = rest
    hd = 2 * LANES

    def rot(blk):
        return _rope_block(blk, cos_ref[...], sin_ref[...]) if rope else blk

    nq = _rms_norm(zm_ref[:, :MLA_Q_RANK], gcq_ref[...]).astype(BF16)
    q = _dot(nq, wuq_ref[...]) * qscale
    for h in range(MLA_HEADS):
        qa_ref[:, h * hd:h * hd + LANES] = q[:, h * hd:h * hd + LANES].astype(BF16)
        qa_ref[:, h * hd + LANES:(h + 1) * hd] = rot(q[:, h * hd + LANES:(h + 1) * hd]).astype(BF16)

    nkv = _rms_norm(zm_ref[:, MLA_Q_RANK:MLA_Q_RANK + MLA_KV_RANK], gckv_ref[...]).astype(BF16)
    kn = _dot(nkv, wkn_ref[...])
    va_ref[...] = _dot(nkv, wv_ref[...]).astype(BF16)
    kr = rot(zm_ref[:, MLA_Q_RANK + MLA_KV_RANK:MLA_Q_RANK + MLA_KV_RANK + LANES]).astype(BF16)
    for h in range(MLA_HEADS):
        ka_ref[:, h * hd:h * hd + LANES] = kn[:, h * LANES:(h + 1) * LANES].astype(BF16)
        ka_ref[:, h * hd + LANES:(h + 1) * hd] = kr


def _mla_up(zm, seq, g_cq, g_ckv, wuq, wkn, wv, rope, qscale):
    t, nz = zm.shape
    tm = _divisor_tile(seq if rope is not None else t, TOKEN_TILE, 8)
    hq = MLA_HEADS * 2 * LANES
    hv = MLA_HEADS * MLA_V
    full = lambda a: pl.BlockSpec(a.shape, lambda i: (0, 0))
    g_cq = g_cq.reshape(1, -1)
    g_ckv = g_ckv.reshape(1, -1)
    in_specs = [pl.BlockSpec((tm, nz), lambda i: (i, 0)), full(g_cq), full(g_ckv), full(wuq), full(wkn), full(wv)]
    args = [zm, g_cq, g_ckv, wuq, wkn, wv]
    if rope is not None:
        tpb = seq // tm
        tab = pl.BlockSpec((tm, LANES), lambda i: (i % tpb, 0))
        in_specs += [tab, tab]
        args += list(rope)
    return pl.pallas_call(
        functools.partial(_mla_up_kernel, rope=rope is not None, qscale=qscale),
        grid=(t // tm,),
        in_specs=in_specs,
        out_specs=[pl.BlockSpec((tm, hq), lambda i: (i, 0)), pl.BlockSpec((tm, hq), lambda i: (i, 0)),
                   pl.BlockSpec((tm, hv), lambda i: (i, 0))],
        out_shape=[jax.ShapeDtypeStruct((t, hq), BF16), jax.ShapeDtypeStruct((t, hq), BF16),
                   jax.ShapeDtypeStruct((t, hv), BF16)],
        compiler_params=_params("parallel"),
        name="mla_up_proj",
    )(*args)


def _raw_scores(q, kc_ref, kl_ref, kcols):
    sc = _dot_nt(q, kc_ref[:, kcols])
    sl = None if kl_ref is None else _dot_nt(q, kl_ref[:, kcols])
    return sc, sl


def _softmax_numerators(scores):
    sc, sl = scores
    m = jnp.max(sc, axis=-1, keepdims=True)
    if sl is None:
        return jnp.exp2(sc - m), None
    m = jnp.maximum(m, jnp.max(sl, axis=-1, keepdims=True))
    return jnp.exp2(sc - m), jnp.exp2(sl - m)


def _row_sum(ec, el):
    den = jnp.sum(ec, axis=-1, keepdims=True)
    return den if el is None else den + jnp.sum(el, axis=-1, keepdims=True)


def _attn_kernel(*refs, has_lat, row_chunk, hps, group, dqk, dv):
    if has_lat:
        q_ref, kc_ref, vc_ref, kl_ref, vl_ref, o_ref = refs
    else:
        (q_ref, kc_ref, vc_ref, o_ref), kl_ref, vl_ref = refs, None, None
    nr = q_ref.shape[0] // row_chunk
    rows = lambda r: slice(r * row_chunk, (r + 1) * row_chunk)
    with_ones = lambda v: jnp.concatenate([v, jnp.ones_like(v)], axis=1)
    n_kv = vc_ref.shape[1] // dv
    vc1 = [with_ones(vc_ref[:, g * dv:(g + 1) * dv]) for g in range(n_kv)]
    vl1 = [with_ones(vl_ref[:, g * dv:(g + 1) * dv]) for g in range(n_kv)] if has_lat else None

    def scores(c):
        h, r = divmod(c, nr)
        g = h // group
        return _raw_scores(q_ref[rows(r), h * dqk:(h + 1) * dqk], kc_ref, kl_ref, slice(g * dqk, (g + 1) * dqk))

    def finish(c, s):
        h, r = divmod(c, nr)
        g = h // group
        ec, el = _softmax_numerators(s)
        o = _dot(ec.astype(BF16), vc1[g])
        if has_lat:
            o = o + _dot(el.astype(BF16), vl1[g])
        o_ref[rows(r), h * dv:(h + 1) * dv] = (o[:, :dv] / o[:, dv:]).astype(o_ref.dtype)

    _skewed_chunks(hps * nr, scores, finish)


def _attention(q, kc, vc, kl, vl, *, heads, group, dqk, dv, q_col0, k_col0, v_col0, hps=1):
    assert dv == LANES and heads % hps == 0 and (hps == 1 or hps % group == 0)
    b, lq, _ = q.shape
    lc = kc.shape[1]
    tq = _divisor_tile(lq, Q_TILE, 8)
    n_kv = max(hps // group, 1)
    kv_of = (lambda h: h // group) if hps == 1 else (lambda h: h)
    qw, kw, vw = hps * dqk, n_kv * dqk, n_kv * dv
    assert q_col0 % qw == 0 and k_col0 % kw == 0 and v_col0 % vw == 0
    qo, ko, vo = q_col0 // qw, k_col0 // kw, v_col0 // vw
    in_specs = [
        pl.BlockSpec((None, tq, qw), lambda bi, h, i: (bi, i, qo + h)),
        pl.BlockSpec((None, lc, kw), lambda bi, h, i: (bi, 0, ko + kv_of(h))),
        pl.BlockSpec((None, lc, vw), lambda bi, h, i: (bi, 0, vo + kv_of(h))),
    ]
    args = [q, kc, vc]
    if kl is not None:
        ll = kl.shape[1]
        in_specs += [
            pl.BlockSpec((None, ll, kw), lambda bi, h, i: (bi, 0, ko + kv_of(h))),
            pl.BlockSpec((None, ll, vw), lambda bi, h, i: (bi, 0, vo + kv_of(h))),
        ]
        args += [kl, vl]
    return pl.pallas_call(
        functools.partial(_attn_kernel, has_lat=kl is not None, row_chunk=min(ATTN_ROW_CHUNK, tq), hps=hps,
                          group=group if hps > 1 else hps, dqk=dqk, dv=dv),
        grid=(b, heads // hps, lq // tq),
        in_specs=in_specs,
        out_specs=pl.BlockSpec((None, tq, hps * dv), lambda bi, h, i: (bi, i, h)),
        out_shape=jax.ShapeDtypeStruct((b, lq, heads * dv), BF16),
        compiler_params=_params("parallel", "arbitrary", "arbitrary"),
        name="softmax_attention",
    )(*args)


def _diff_attn_kernel(*refs, lambda_init, has_lat, row_chunk, hps):
    if has_lat:
        q_ref, kc_ref, vc_ref, kl_ref, vl_ref, lq1, lk1, lq2, lk2, gs_ref, o_ref = refs
    else:
        (q_ref, kc_ref, vc_ref, lq1, lk1, lq2, lk2, gs_ref, o_ref), kl_ref, vl_ref = refs, None, None
    lam = (jnp.exp(jnp.sum(lq1[...] * lk1[...], axis=-1, keepdims=True))
           - jnp.exp(jnp.sum(lq2[...] * lk2[...], axis=-1, keepdims=True)) + lambda_init)
    hd = DIFF_HEAD_DIM
    nr = q_ref.shape[0] // row_chunk
    rows = lambda r: slice(r * row_chunk, (r + 1) * row_chunk)

    def scores(c):
        h, r = divmod(c, nr)
        c0, c1 = slice(2 * h * hd, (2 * h + 1) * hd), slice((2 * h + 1) * hd, (2 * h + 2) * hd)
        return (_raw_scores(q_ref[rows(r), c0], kc_ref, kl_ref, c0),
                _raw_scores(q_ref[rows(r), c1], kc_ref, kl_ref, c1))

    def finish(c, s01):
        h, r = divmod(c, nr)
        vcols = slice(2 * h * hd, (2 * h + 2) * hd)
        ec0, el0 = _softmax_numerators(s01[0])
        ec1, el1 = _softmax_numerators(s01[1])
        den0 = _row_sum(ec0, el0)
        ratio = lam * den0 / _row_sum(ec1, el1)
        y = _dot((ec0 - ec1 * ratio).astype(BF16), vc_ref[:, vcols])
        if has_lat:
            y = y + _dot((el0 - el1 * ratio).astype(BF16), vl_ref[:, vcols])
        y = y / den0
        o_ref[rows(r), vcols] = (_rms_norm(y, gs_ref[...]) * (1.0 - lambda_init)).astype(o_ref.dtype)

    _skewed_chunks(hps * nr, scores, finish)


def _diff_attention(q, kc, vc, kl, vl, lams, g_sub, lambda_init, hps=1):
    assert DIFF_HEADS % hps == 0
    b, lq, _ = q.shape
    lc = kc.shape[1]
    tq = _divisor_tile(lq, Q_TILE, 8)
    w = hps * 2 * DIFF_HEAD_DIM
    steps = DIFF_HEADS // hps
    in_specs = [
        pl.BlockSpec((None, tq, w), lambda bi, h, i: (bi, i, h)),
        pl.BlockSpec((None, lc, w), lambda bi, h, i: (bi, 0, steps + h)),
        pl.BlockSpec((None, lc, w), lambda bi, h, i: (bi, 0, h)),
    ]
    args = [q, kc, vc]
    if kl is not None:
        ll = kl.shape[1]
        in_specs += [
            pl.BlockSpec((None, ll, w), lambda bi, h, i: (bi, 0, steps + h)),
            pl.BlockSpec((None, ll, w), lambda bi, h, i: (bi, 0, h)),
        ]
        args += [kl, vl]
    small = lambda n: pl.BlockSpec((1, n), lambda bi, h, i: (0, 0))
    in_specs += [small(DIFF_HEAD_DIM)] * 4 + [small(2 * DIFF_HEAD_DIM)]
    args += [a.reshape(1, -1) for a in lams] + [g_sub.reshape(1, -1)]
    return pl.pallas_call(
        functools.partial(_diff_attn_kernel, lambda_init=lambda_init, has_lat=kl is not None,
                          row_chunk=min(ATTN_ROW_CHUNK, tq), hps=hps),
        grid=(b, steps, lq // tq),
        in_specs=in_specs,
        out_specs=pl.BlockSpec((None, tq, w), lambda bi, h, i: (bi, i, h)),
        out_shape=jax.ShapeDtypeStruct((b, lq, steps * w), BF16),
        compiler_params=_params("parallel", "arbitrary", "arbitrary"),
        name="diff_attention",
    )(*args)


def _outproj_kernel(*refs, nparts, alpha):
    y_refs = refs[:nparts]
    w_refs = refs[nparts:2 * nparts]
    x_ref, gate_ref, g_ref, b_ref, o_ref = refs[2 * nparts:]
    row_chunk = min(ROW_CHUNK, o_ref.shape[0])
    rows = lambda r: slice(r * row_chunk, (r + 1) * row_chunk)

    def project(r):
        y = _dot(y_refs[0][rows(r), :], w_refs[0][...])
        for p in range(1, nparts):
            y = y + _dot(y_refs[p][rows(r), :], w_refs[p][...])
        return y

    def finish(r, y):
        res = alpha * x_ref[rows(r), :] + gate_ref[...] * y
        o_ref[rows(r), :] = _layer_norm(res, g_ref[...], b_ref[...])

    _skewed_chunks(o_ref.shape[0] // row_chunk, project, finish)


def _outproj(ys, w_o, x, mod, sub, g, b, alpha):
    t, d = x.shape
    tm = _divisor_tile(t if mod.is_ctx else mod.seq, OUT_TOKEN_TILE, 8)
    kp = ys[0].shape[1]
    assert all(y.shape[1] == kp for y in ys) and kp * len(ys) == w_o.shape[0]
    vec = pl.BlockSpec((1, d), lambda i: (0, 0))
    in_specs = [pl.BlockSpec((tm, kp), lambda i: (i, 0)) for _ in ys]
    in_specs += [pl.BlockSpec((kp, d), functools.partial(lambda i, p: (p, 0), p=p), pipeline_mode=pl.Buffered(1))
                 for p in range(len(ys))]
    in_specs += [pl.BlockSpec((tm, d), lambda i: (i, 0)), mod.spec(d, sub, 2, tm), vec, vec]
    return pl.pallas_call(
        functools.partial(_outproj_kernel, nparts=len(ys), alpha=alpha),
        grid=(t // tm,),
        in_specs=in_specs,
        out_specs=pl.BlockSpec((tm, d), lambda i: (i, 0)),
        out_shape=jax.ShapeDtypeStruct((t, d), F32),
        compiler_params=_params("parallel"),
        name="mixer_out_proj",
    )(*ys, *([w_o] * len(ys)), x, mod.table, g.reshape(1, d), b.reshape(1, d))


def _rope_tables(rows, rot_dim):
    r, col = jnp.meshgrid(jnp.arange(rows, dtype=F32), jnp.arange(GRID_W, dtype=F32), indexing="ij")
    n_freq = rot_dim // 4
    inv_freq = ROPE_THETA ** (-jnp.arange(n_freq, dtype=F32) / n_freq)
    ang = jnp.concatenate([r.reshape(-1, 1) * inv_freq, col.reshape(-1, 1) * inv_freq], -1)
    cos, sin = jnp.cos(ang), jnp.sin(ang)
    pad = LANES // 2 - rot_dim // 2
    cos = jnp.pad(cos, ((0, 0), (0, pad)), constant_values=1.0)
    sin = jnp.pad(sin, ((0, 0), (0, pad)))
    return jnp.concatenate([cos, cos], -1), jnp.concatenate([-sin, sin], -1)


def _spread_rot(w):
    half = MLA_ROPE // 2
    z = jnp.zeros((w.shape[0], LANES // 2 - half), w.dtype)
    return jnp.concatenate([w[:, :half], z, w[:, half:], z], -1)


def kernel(x, c, ctx, c_ctx, w_ada, b_ada, ln_g, ln_b, ffn_w1, ffn_w3, ffn_w2, mg_w_in, mla_g_cq, mla_g_ckv,
           mla_w_uq, mla_w_ukv, gqa_g_q, gqa_g_k, mg_w_o, diff_w_in, diff_lq1, diff_lk1, diff_lq2, diff_lk2,
           diff_g_sub, diff_w_o):
    bsz, seq, d = x.shape
    lc = ctx.shape[1]
    depth = w_ada.shape[0]
    assert bsz + 1 <= MOD_ROWS and seq % GRID_W == 0
    alpha = (2 * depth) ** 0.25
    rows = seq // GRID_W
    rope_mla = _rope_tables(rows, MLA_ROPE)
    rope_hd = _rope_tables(rows, GQA_HEAD_DIM)

    cc = jnp.concatenate([c, c_ctx[None], jnp.zeros((MOD_ROWS - bsz - 1, d), F32)], 0)
    mod_table = _ada(cc, w_ada, b_ada).reshape(depth * MOD_ROWS * 3 * N_SUB, 1, d)

    x_lat = x.reshape(bsz * seq, d)
    x_ctx = ctx.reshape(bsz * lc, d)
    w1b, w3b, w2b = ffn_w1.astype(BF16), ffn_w3.astype(BF16), ffn_w2.astype(BF16)

    for i in range(depth):
        need_ctx = i < depth - 1
        mod_l = _Mod(mod_table, i, bsz, seq, False)
        mod_c = _Mod(mod_table, i, bsz, seq, True)
        ffn = lambda xs, m, sub, k: _ffn(xs, m, sub, w1b, w3b, w2b, k, ln_g[i, sub], ln_b[i, sub], alpha)

        x_lat = ffn(x_lat, mod_l, 0, 0)
        x_ctx = ffn(x_ctx, mod_c, 0, 0)

        if i % 2 == 0:
            e = i // 2
            w_in = mg_w_in[e]
            o_kr = MLA_Q_RANK + MLA_KV_RANK
            o_gq = o_kr + MLA_ROPE
            n_gqk = (GQA_HEADS + GQA_KV_HEADS) * GQA_HEAD_DIM
            w_mla = jnp.concatenate([w_in[:, :o_kr], _spread_rot(w_in[:, o_kr:o_gq])], -1).astype(BF16)
            w_gqa = w_in[:, o_gq:].astype(BF16)
            wuq = mla_w_uq[e].reshape(MLA_Q_RANK, MLA_HEADS, MLA_NOPE + MLA_ROPE)
            wuq = jnp.concatenate(
                [wuq[:, :, :MLA_NOPE],
                 _spread_rot(wuq[:, :, MLA_NOPE:].reshape(-1, MLA_ROPE)).reshape(MLA_Q_RANK, MLA_HEADS, LANES)],
                -1).reshape(MLA_Q_RANK, MLA_HEADS * 2 * LANES).astype(BF16)
            wukv = mla_w_ukv[e].reshape(MLA_KV_RANK, MLA_HEADS, MLA_NOPE + MLA_V)
            wkn = wukv[:, :, :MLA_NOPE].reshape(MLA_KV_RANK, -1).astype(BF16)
            wv = wukv[:, :, MLA_NOPE:].reshape(MLA_KV_RANK, -1).astype(BF16)
            n_gv = GQA_KV_HEADS * GQA_HEAD_DIM
            gain = jnp.concatenate([jnp.tile(gqa_g_q[e] * (GQA_SCALE * LOG2_E), GQA_HEADS),
                                    jnp.tile(gqa_g_k[e], GQA_KV_HEADS), jnp.ones((n_gv,), F32)])[None]
            w_o = mg_w_o[e].astype(BF16)

            def project(xs, m, rp_mla, rp_hd):
                zm = _proj(xs, m, 1, w_mla, 0, w_mla.shape[1], F32)
                qkv = _proj(xs, m, 1, w_gqa, 0, n_gqk + n_gv, BF16, gain=gain, headnorm=True, rope=rp_hd,
                            plain_tail=n_gv)
                qa, ka, va = _mla_up(zm, seq, mla_g_cq[e], mla_g_ckv[e], wuq, wkn, wv, rp_mla,
                                     MLA_SCALE * LOG2_E)
                n = xs.shape[0] // bsz
                return [a.reshape(bsz, n, -1) for a in (qa, ka, va, qkv)]

            qa, ka, va, qkv = project(x_lat, mod_l, rope_mla, rope_hd)
            qa_c, ka_c, va_c, qkv_c = project(x_ctx, mod_c, None, None)
            mla = dict(heads=MLA_HEADS, group=1, dqk=2 * LANES, dv=MLA_V, q_col0=0, k_col0=0, v_col0=0)
            gqa = dict(heads=GQA_HEADS, group=GQA_HEADS // GQA_KV_HEADS, dqk=GQA_HEAD_DIM, dv=GQA_HEAD_DIM,
                       q_col0=0, k_col0=GQA_HEADS * GQA_HEAD_DIM, v_col0=n_gqk)
            ya = _attention(qa, ka_c, va_c, ka, va, **mla, hps=ATTN_HEADS_PER_STEP)
            yb = _attention(qkv, qkv_c, qkv_c, qkv, qkv, **gqa, hps=ATTN_HEADS_PER_STEP)
            ys_lat = [ya.reshape(bsz * seq, -1), yb.reshape(bsz * seq, -1)]
            if need_ctx:
                ya_c = _attention(qa_c, ka_c, va_c, None, None, **mla, hps=MLA_HEADS)
                yb_c = _attention(qkv_c, qkv_c, qkv_c, None, None, **gqa, hps=GQA_HEADS)
                ys_ctx = [ya_c.reshape(bsz * lc, -1), yb_c.reshape(bsz * lc, -1)]
        else:
            o = i // 2
            lambda_init = 0.8 - 0.6 * math.exp(-0.3 * i)
            w_in = diff_w_in[o].astype(BF16)
            n_qk = 2 * DIFF_HEADS * 2 * DIFF_HEAD_DIM
            w_o = diff_w_o[o].astype(BF16)
            lams = (diff_lq1[o], diff_lk1[o], diff_lq2[o], diff_lk2[o])
            qk_scale = jnp.concatenate([jnp.full((n_qk // 2,), DIFF_SCALE * LOG2_E, F32),
                                        jnp.ones((n_qk // 2,), F32)])[None]

            def project(xs, m, rp):
                qk = _proj(xs, m, 1, w_in, 0, n_qk, BF16, gain=qk_scale, rope=rp)
                v = _proj(xs, m, 1, w_in, n_qk, w_in.shape[1] - n_qk, BF16)
                n = xs.shape[0] // bsz
                return qk.reshape(bsz, n, -1), v.reshape(bsz, n, -1)

            qk, v = project(x_lat, mod_l, rope_hd)
            qk_c, v_c = project(x_ctx, mod_c, None)
            y = _diff_attention(qk, qk_c, v_c, qk, v, lams, diff_g_sub[o], lambda_init, hps=DIFF_HEADS_PER_STEP)
            ys_lat = [y.reshape(bsz * seq, -1)]
            if need_ctx:
                y_c = _diff_attention(qk_c, qk_c, v_c, None, None, lams, diff_g_sub[o], lambda_init)
                ys_ctx = [y_c.reshape(bsz * lc, -1)]

        x_lat = _outproj(ys_lat, w_o, x_lat, mod_l, 1, ln_g[i, 1], ln_b[i, 1], alpha)
        x_lat = ffn(x_lat, mod_l, 2, 1)
        if need_ctx:
            x_ctx = _outproj(ys_ctx, w_o, x_ctx, mod_c, 1, ln_g[i, 1], ln_b[i, 1], alpha)
            x_ctx = ffn(x_ctx, mod_c, 2, 1)
    return x_lat.reshape(bsz, seq, d)
```
